```python
import math
import jax
import jax.numpy as jnp
from jax import lax
import numpy as np

D_MODEL = 1024
BATCH = 4
SEQ = 4096
DEPTH = 2

HEAD_DIM = 64
ROPE_DIM = HEAD_DIM // 4
ROPE_THETA = 500000.0
NORM_EPS = 1e-6
Q_BLOCK = 128

A_HEADS = D_MODEL // 128
A_WIDTH = A_HEADS * HEAD_DIM
A_PATTERNS = ((128, 1), (512, 4), (2048, 16))

B_WIDTH = D_MODEL - A_WIDTH
B_GROUP = 16
B_GROUPS = B_WIDTH // B_GROUP
B_STATE = 64
DT_MIN = 1e-3
DT_MAX = 1e-1

C_HEADS = D_MODEL // HEAD_DIM
C_NOPE = HEAD_DIM - ROPE_DIM
C_VDIM = HEAD_DIM
C_LATENT = D_MODEL // 4
IDX_HEADS = 8
IDX_DIM = 64
TOPK_MAX = 256
C_SPLITS = (C_HEADS * ROPE_DIM, C_HEADS * C_NOPE, C_LATENT, ROPE_DIM, IDX_HEADS * IDX_DIM, IDX_DIM)
C_IN = sum(C_SPLITS) + IDX_HEADS

FF_DENSE = ((8 * D_MODEL // 3 + 127) // 128) * 128
N_EXPERTS = 8
TOP_K_EXPERTS = 2
FF_EXPERT = 7 * D_MODEL // 2

kernel_name = 'hybrid_dilated_s5_dsa_moe_adaln'


def rms_norm(x, g):
    xf = x.astype(jnp.float32)
    y = xf * lax.rsqrt(jnp.mean(xf * xf, axis=-1, keepdims=True) + NORM_EPS)
    return (y * g.astype(jnp.float32)).astype(x.dtype)


def modulate(x, shift, scale):
    return x * (1 + scale[:, None, :]) + shift[:, None, :]


def rope_tables(seq):
    pos = jnp.arange(seq, dtype=jnp.float32)
    inv = ROPE_THETA ** (-jnp.arange(0, ROPE_DIM, 2, dtype=jnp.float32) / ROPE_DIM)
    ang = pos[:, None] * inv[None, :]
    return jnp.cos(ang), jnp.sin(ang)


def apply_rope(x, cos, sin):
    half = ROPE_DIM // 2
    xf = x.astype(jnp.float32)
    x1, x2, rest = xf[..., :half], xf[..., half:ROPE_DIM], xf[..., ROPE_DIM:]
    cs, sn = cos[:, None, :], sin[:, None, :]
    out = jnp.concatenate([x1 * cs - x2 * sn, x2 * cs + x1 * sn, rest], axis=-1)
    return out.astype(x.dtype)


def to_blocks(a):
    bn, s = a.shape[:2]
    return jnp.moveaxis(a.reshape((bn, s // Q_BLOCK, Q_BLOCK) + a.shape[2:]), 1, 0)


def from_blocks(a):
    a = jnp.moveaxis(a, 0, 1)
    return a.reshape((a.shape[0], a.shape[1] * a.shape[2]) + a.shape[3:])


def gather_rows(a, idx):
    return jax.vmap(lambda ab, ib: ab[ib])(a, idx)


def swiglu(u, w_gate, w_up, w_down):
    return (jax.nn.silu(u @ w_gate) * (u @ w_up)) @ w_down


def dilated_attention(q, k, v):
    bn, s, h, dh = q.shape
    f32 = jnp.float32
    scale = dh ** -0.5

    def block(args):
        blk, qb = args
        t = blk * Q_BLOCK + jnp.arange(Q_BLOCK)
        qf = qb.astype(f32) * scale
        ms, ls, os_ = [], [], []
        for window, dil in A_PATTERNS:
            n_keys = window // dil + 1
            src = t[:, None] - dil * jnp.arange(n_keys)[None, :]
            valid = src >= 0
            src = jnp.maximum(src, 0)
            kg = jnp.take(k, src, axis=1).astype(f32)
            vg = jnp.take(v, src, axis=1).astype(f32)
            sc = jnp.einsum('bqhd,bqnhd->bhqn', qf, kg)
            sc = jnp.where(valid[None, None], sc, -jnp.inf)
            m = jnp.max(sc, axis=-1)
            p = jnp.exp(sc - m[..., None])
            ms.append(m)
            ls.append(jnp.sum(p, axis=-1))
            os_.append(jnp.einsum('bhqn,bqnhd->bhqd', p, vg))
        m_st = jnp.stack(ms)
        wts = jnp.exp(m_st - jnp.max(m_st, axis=0, keepdims=True))
        den = jnp.sum(wts * jnp.stack(ls), axis=0)
        num = jnp.einsum('gbhq,gbhqd->bhqd', wts, jnp.stack(os_))
        out = num / den[..., None]
        return jnp.swapaxes(out, 1, 2).astype(q.dtype)

    out = lax.map(block, (jnp.arange(s // Q_BLOCK), to_blocks(q)))
    return from_blocks(out)


def _complex_affine_combine(e1, e2):
    a1r, a1i, b1r, b1i = e1
    a2r, a2i, b2r, b2i = e2
    ar = a2r * a1r - a2i * a1i
    ai = a2r * a1i + a2i * a1r
    br = a2r * b1r - a2i * b1i + b2r
    bi = a2r * b1i + a2i * b1r + b2i
    return ar, ai, br, bi


def s5_mixer(u, a_re, a_im, log_dt, b_re, b_im, c_re, c_im, d_skip, w_glu, b_glu):
    bn, s, _ = u.shape
    f32 = jnp.float32
    ug = u.reshape(bn, s, B_GROUPS, B_GROUP).astype(f32)
    dt = jnp.exp(log_dt.astype(f32))[:, None]
    lr, li = a_re.astype(f32), a_im.astype(f32)
    mag = jnp.exp(lr * dt)
    abar_r = mag * jnp.cos(li * dt)
    abar_i = mag * jnp.sin(li * dt)
    den = lr * lr + li * li
    nr = abar_r - 1.0
    fr = (nr * lr + abar_i * li) / den
    fi = (abar_i * lr - nr * li) / den
    br, bi = b_re.astype(f32), b_im.astype(f32)
    bbar_r = fr[..., None] * br - fi[..., None] * bi
    bbar_i = fr[..., None] * bi + fi[..., None] * br
    bu_r = jnp.einsum('bsgc,gpc->bsgp', ug, bbar_r)
    bu_i = jnp.einsum('bsgc,gpc->bsgp', ug, bbar_i)
    full = bu_r.shape
    _, _, xr, xi = lax.associative_scan(
        _complex_affine_combine,
        (jnp.broadcast_to(abar_r, full), jnp.broadcast_to(abar_i, full), bu_r, bu_i),
        axis=1)
    y = (jnp.einsum('bsgp,gcp->bsgc', xr, c_re.astype(f32))
         - jnp.einsum('bsgp,gcp->bsgc', xi, c_im.astype(f32))
         + d_skip.astype(f32) * ug)
    y = jax.nn.gelu(y.reshape(bn, s, B_WIDTH)).astype(u.dtype)
    return y * jax.nn.sigmoid(y @ w_glu + b_glu)


def even_mixer(u, cos, sin, w_in, w_out, a_re, a_im, log_dt, b_re, b_im, c_re, c_im,
               d_skip, w_glu, b_glu):
    bn, s, _ = u.shape
    proj = u @ w_in
    q, k, v, s_in = jnp.split(proj, [A_WIDTH, 2 * A_WIDTH, 3 * A_WIDTH], axis=-1)
    shp = (bn, s, A_HEADS, HEAD_DIM)
    q = apply_rope(q.reshape(shp), cos, sin)
    k = apply_rope(k.reshape(shp), cos, sin)
    v = v.reshape(shp)
    y_a = dilated_attention(q, k, v).reshape(bn, s, A_WIDTH)
    y_b = s5_mixer(s_in, a_re, a_im, log_dt, b_re, b_im, c_re, c_im, d_skip, w_glu, b_glu)
    return jnp.concatenate([y_a, y_b], axis=-1) @ w_out


def dsa_mixer(u, cos, sin, w_in, w_out, g_kv, w_uk, w_uv):
    bn, s, _ = u.shape
    f32 = jnp.float32
    topk = min(TOPK_MAX, s // 4)
    proj = u @ w_in
    cuts = [int(v) for v in np.cumsum(C_SPLITS)]
    q_r, q_n, ckv, k_r, q_i, k_i, w_i = jnp.split(proj, cuts, axis=-1)
    q_r = apply_rope(q_r.reshape(bn, s, C_HEADS, ROPE_DIM), cos, sin)
    q_n = q_n.reshape(bn, s, C_HEADS, C_NOPE)
    ckv = rms_norm(ckv, g_kv)
    k_r = apply_rope(k_r.reshape(bn, s, 1, ROPE_DIM), cos, sin)[:, :, 0]
    q_i = apply_rope(q_i.reshape(bn, s, IDX_HEADS, IDX_DIM), cos, sin)
    k_i = apply_rope(k_i.reshape(bn, s, 1, IDX_DIM), cos, sin)[:, :, 0]
    w_i = w_i * IDX_HEADS ** -0.5
    key_pos = jnp.arange(s)

    def block(args):
        blk, qr_b, qn_b, qi_b, wi_b = args
        t = blk * Q_BLOCK + jnp.arange(Q_BLOCK)
        logits = jnp.einsum('bqhd,bsd->bqhs', qi_b.astype(f32), k_i.astype(f32)) * IDX_DIM ** -0.5
        score = jnp.einsum('bqh,bqhs->bqs', wi_b.astype(f32), jax.nn.relu(logits))
        causal = key_pos[None, :] <= t[:, None]
        score = jnp.where(causal[None], score, -jnp.inf)
        _, idx = lax.top_k(score, topk)
        valid = idx <= t[None, :, None]
        ckv_g = gather_rows(ckv, idx).astype(f32)
        kr_g = gather_rows(k_r, idx).astype(f32)
        q_lat = jnp.einsum('bqhn,chn->bqhc', qn_b.astype(f32), w_uk.astype(f32))
        sc = (jnp.einsum('bqhc,bqkc->bhqk', q_lat, ckv_g)
              + jnp.einsum('bqhr,bqkr->bhqk', qr_b.astype(f32), kr_g)) * HEAD_DIM ** -0.5
        sc = jnp.where(valid[:, None], sc, -jnp.inf)
        p = jax.nn.softmax(sc, axis=-1)
        o_lat = jnp.einsum('bhqk,bqkc->bqhc', p, ckv_g)
        return jnp.einsum('bqhc,chd->bqhd', o_lat, w_uv.astype(f32)).astype(u.dtype)

    out = lax.map(block, (jnp.arange(s // Q_BLOCK), to_blocks(q_r), to_blocks(q_n),
                          to_blocks(q_i), to_blocks(w_i)))
    out = from_blocks(out).reshape(bn, s, C_HEADS * C_VDIM)
    return out @ w_out


def moe_swiglu(u, w_router, b_router, w_gate, w_up, w_down):
    bn, s, d = u.shape
    f32 = jnp.float32
    tok = u.reshape(bn * s, d)
    logits = tok.astype(f32) @ w_router.astype(f32) + b_router.astype(f32)
    top_val, top_idx = lax.top_k(logits, TOP_K_EXPERTS)
    gates = jax.nn.softmax(top_val, axis=-1)
    combine = jnp.einsum('tk,tke->te', gates, jax.nn.one_hot(top_idx, N_EXPERTS, dtype=f32))
    out = jnp.zeros((bn * s, d), f32)
    for e in range(N_EXPERTS):
        out = out + combine[:, e:e + 1] * swiglu(tok, w_gate[e], w_up[e], w_down[e]).astype(f32)
    return out.astype(u.dtype).reshape(bn, s, d)


def setup_inputs(seed: int = 0) -> dict:
    key = jax.random.key(seed)
    keys = iter(jax.random.split(key, 48))
    f32 = jnp.float32

    def nrm(shape, scale):
        return jax.random.normal(next(keys), shape, f32) * scale

    def gain(shape):
        return 1.0 + nrm(shape, 0.05)

    ne, no = (DEPTH + 1) // 2, DEPTH // 2
    d = D_MODEL
    n_idx = jnp.arange(B_STATE, dtype=f32)
    log_dt = math.log(DT_MIN) + jax.random.uniform(next(keys), (ne, B_GROUPS), f32) * (
        math.log(DT_MAX) - math.log(DT_MIN))
    return {
        'x': nrm((BATCH, SEQ, d), 1.0),
        'c': nrm((BATCH, d), 1.0),
        'w_ada': nrm((DEPTH, d, 6 * d), 0.5 * d ** -0.5),
        'b_ada': nrm((DEPTH, 6 * d), 0.02),
        'g_mix': gain((DEPTH, d)),
        'g_ffn': gain((DEPTH, d)),
        'g_final': gain((d,)),
        'e_w_in': nrm((ne, d, 3 * A_WIDTH + B_WIDTH), d ** -0.5),
        'e_w_out': nrm((ne, A_WIDTH + B_WIDTH, d), (A_WIDTH + B_WIDTH) ** -0.5),
        's5_a_re': -0.5 * jnp.exp(nrm((ne, B_GROUPS, B_STATE), 0.05)),
        's5_a_im': math.pi * n_idx + nrm((ne, B_GROUPS, B_STATE), 0.01),
        's5_log_dt': log_dt,
        's5_b_re': nrm((ne, B_GROUPS, B_STATE, B_GROUP), (2 * B_GROUP) ** -0.5),
        's5_b_im': nrm((ne, B_GROUPS, B_STATE, B_GROUP), (2 * B_GROUP) ** -0.5),
        's5_c_re': nrm((ne, B_GROUPS, B_GROUP, B_STATE), (2 * B_STATE) ** -0.5),
        's5_c_im': nrm((ne, B_GROUPS, B_GROUP, B_STATE), (2 * B_STATE) ** -0.5),
        's5_d': nrm((ne, B_GROUPS, B_GROUP), 1.0),
        's5_w_glu': nrm((ne, B_WIDTH, B_WIDTH), B_WIDTH ** -0.5),
        's5_b_glu': nrm((ne, B_WIDTH), 0.02),
        'ff_w_gate': nrm((ne, d, FF_DENSE), d ** -0.5),
        'ff_w_up': nrm((ne, d, FF_DENSE), d ** -0.5),
        'ff_w_down': nrm((ne, FF_DENSE, d), FF_DENSE ** -0.5),
        'o_w_in': nrm((no, d, C_IN), d ** -0.5),
        'o_w_out': nrm((no, C_HEADS * C_VDIM, d), (C_HEADS * C_VDIM) ** -0.5),
        'mla_g_kv': gain((no, C_LATENT)),
        'mla_w_uk': nrm((no, C_LATENT, C_HEADS, C_NOPE), C_LATENT ** -0.5),
        'mla_w_uv': nrm((no, C_LATENT, C_HEADS, C_VDIM), C_LATENT ** -0.5),
        'moe_w_router': nrm((no, d, N_EXPERTS), d ** -0.5),
        'moe_b_router': nrm((no, N_EXPERTS), 0.01),
        'moe_w_gate': nrm((no, N_EXPERTS, d, FF_EXPERT), d ** -0.5),
        'moe_w_up': nrm((no, N_EXPERTS, d, FF_EXPERT), d ** -0.5),
        'moe_w_down': nrm((no, N_EXPERTS, FF_EXPERT, d), FF_EXPERT ** -0.5),
    }


def reference(x, c, w_ada, b_ada, g_mix, g_ffn, g_final,
              e_w_in, e_w_out, s5_a_re, s5_a_im, s5_log_dt, s5_b_re, s5_b_im,
              s5_c_re, s5_c_im, s5_d, s5_w_glu, s5_b_glu,
              ff_w_gate, ff_w_up, ff_w_down,
              o_w_in, o_w_out, mla_g_kv, mla_w_uk, mla_w_uv,
              moe_w_router, moe_b_router, moe_w_gate, moe_w_up, moe_w_down):
    s = x.shape[1]
    cos, sin = rope_tables(s)
    c_act = jax.nn.silu(c.astype(jnp.float32))
    h = x
    for layer in range(DEPTH):
        mod = (c_act @ w_ada[layer].astype(jnp.float32)
               + b_ada[layer].astype(jnp.float32)).astype(x.dtype)
        sh_m, sc_m, gt_m, sh_f, sc_f, gt_f = jnp.split(mod, 6, axis=-1)
        i = layer // 2
        u = modulate(rms_norm(h, g_mix[layer]), sh_m, sc_m)
        if layer % 2 == 0:
            y = even_mixer(u, cos, sin, e_w_in[i], e_w_out[i], s5_a_re[i], s5_a_im[i],
                           s5_log_dt[i], s5_b_re[i], s5_b_im[i], s5_c_re[i], s5_c_im[i],
                           s5_d[i], s5_w_glu[i], s5_b_glu[i])
        else:
            y = dsa_mixer(u, cos, sin, o_w_in[i], o_w_out[i], mla_g_kv[i], mla_w_uk[i], mla_w_uv[i])
        h = h + gt_m[:, None, :] * y
        u = modulate(rms_norm(h, g_ffn[layer]), sh_f, sc_f)
        if layer % 2 == 0:
            y = swiglu(u, ff_w_gate[i], ff_w_up[i], ff_w_down[i])
        else:
            y = moe_swiglu(u, moe_w_router[i], moe_b_router[i], moe_w_gate[i], moe_w_up[i], moe_w_down[i])
        h = h + gt_f[:, None, :] * y
    return rms_norm(h, g_final)
```

```python
import functools
import math

import jax
import jax.numpy as jnp
import numpy as np
from jax import lax
from jax.experimental import pallas as pl
from jax.experimental.pallas import tpu as pltpu

F32 = jnp.float32
BF16 = jnp.bfloat16
I32 = jnp.int32
HIGHEST = lax.Precision.HIGHEST

D_MODEL = 1024
HEAD_DIM = 64
ROPE_DIM = 16
ROPE_THETA = 500000.0
NORM_EPS = 1e-6
A_HEADS = 8
A_WIDTH = 512
A_PATTERNS = ((128, 1), (512, 4), (2048, 16))
B_WIDTH = 512
B_GROUP = 16
B_GROUPS = 32
B_STATE = 64
C_HEADS = 16
C_NOPE = 48
C_LATENT = 256
IDX_HEADS = 8
IDX_DIM = 64
TOPK_MAX = 256
FF_DENSE = 2816
N_EXPERTS = 8
FF_EXPERT = 3584

LANES = 128
Q_BLOCK = 128
S5_CHUNK = 16
VMEM_LIMIT = 56 * 1024 * 1024
NEG_BIG = -1e30
INT_MIN = -(2 ** 31)


def _cparams(sem, vmem=VMEM_LIMIT):
    return pltpu.CompilerParams(dimension_semantics=sem, vmem_limit_bytes=vmem)


def _dot(a, b):
    return jnp.dot(a, b, preferred_element_type=F32)


def _dot_nt(a, b):
    return lax.dot_general(a, b, (((1,), (1,)), ((), ())), preferred_element_type=F32)


def _norm_mod(x, g, shift, scale):
    ms = jnp.mean(x * x, axis=-1, keepdims=True)
    y = x * lax.rsqrt(ms + NORM_EPS) * g
    return y * (1.0 + scale) + shift


def _rope128(x, ct, sa, sb):
    return x * ct + pltpu.roll(x, LANES - ROPE_DIM // 2, 1) * sa + pltpu.roll(x, ROPE_DIM // 2, 1) * sb


def _adaln_body(c_ref, w_ref, b_ref, o_ref):
    c = c_ref[...]
    ca = c * jax.nn.sigmoid(c)
    o_ref[0] = jnp.dot(ca, w_ref[0], preferred_element_type=F32, precision=HIGHEST) + b_ref[0]


def _adaln(c, w_ada, b_ada):
    depth, d, d6 = w_ada.shape
    bn = c.shape[0]
    rows = 8
    cp = jnp.zeros((rows, d), F32).at[:bn].set(c)
    tn = 1536
    out = pl.pallas_call(
        _adaln_body,
        grid=(depth, d6 // tn),
        in_specs=[pl.BlockSpec((rows, d), lambda l, j: (0, 0)),
                  pl.BlockSpec((1, d, tn), lambda l, j: (l, 0, j)),
                  pl.BlockSpec((1, 1, tn), lambda l, j: (l, 0, j))],
        out_specs=pl.BlockSpec((1, rows, tn), lambda l, j: (l, 0, j)),
        out_shape=jax.ShapeDtypeStruct((depth, rows, d6), F32),
        compiler_params=_cparams(("arbitrary", "arbitrary")),
        name="adaln",
    )(cp, w_ada, b_ada.reshape(depth, 1, d6))
    return out[:, :bn]


def _rope_tables(seq, period):
    half = ROPE_DIM // 2
    pos = jnp.arange(seq, dtype=F32)
    inv = ROPE_THETA ** (-jnp.arange(0, ROPE_DIM, 2, dtype=F32) / ROPE_DIM)
    ang = pos[:, None] * inv[None, :]
    cos, sin = jnp.cos(ang), jnp.sin(ang)
    lane = np.arange(LANES) % period
    first = lane < half
    second = (lane >= half) & (lane < ROPE_DIM)
    idx = np.where(first, lane, np.where(second, lane - half, 0))
    cos_l, sin_l = cos[:, idx], sin[:, idx]
    ct = jnp.where(first | second, cos_l, 1.0)
    sa = jnp.where(first, -sin_l, 0.0)
    sb = jnp.where(second, sin_l, 0.0)
    return ct, sa, sb


def _proj0_body(x_ref, g_ref, sh_ref, sc_ref, w_ref, ct_ref, sa_ref, sb_ref, qkv_ref, s_ref):
    u = _norm_mod(x_ref[...], g_ref[...], sh_ref[0], sc_ref[0]).astype(BF16)
    ct, sa, sb = ct_ref[...], sa_ref[...], sb_ref[...]
    for j in range(3):
        acc = _dot(u, w_ref[:, j * A_WIDTH:(j + 1) * A_WIDTH])
        for c in range(A_WIDTH // LANES):
            a = acc[:, c * LANES:(c + 1) * LANES]
            if j < 2:
                a = _rope128(a, ct, sa, sb)
            if j == 0:
                a = a * (HEAD_DIM ** -0.5)
            qkv_ref[:, j * A_WIDTH + c * LANES:j * A_WIDTH + (c + 1) * LANES] = a.astype(BF16)
    s_ref[...] = _dot(u, w_ref[:, 3 * A_WIDTH:])


def _proj0(x2, g, shift, scale, w_bf, tabs, seq, tm=512):
    t, d = x2.shape
    n = w_bf.shape[1]
    tpb = seq // tm
    ct, sa, sb = tabs
    tab_spec = pl.BlockSpec((tm, LANES), lambda i: (i % tpb, 0))
    mod_spec = pl.BlockSpec((1, 1, d), lambda i: (i // tpb, 0, 0))
    return pl.pallas_call(
        _proj0_body,
        grid=(t // tm,),
        in_specs=[pl.BlockSpec((tm, d), lambda i: (i, 0)),
                  pl.BlockSpec((1, d), lambda i: (0, 0)),
                  mod_spec, mod_spec,
                  pl.BlockSpec((d, n), lambda i: (0, 0)),
                  tab_spec, tab_spec, tab_spec],
        out_specs=[pl.BlockSpec((tm, 3 * A_WIDTH), lambda i: (i, 0)),
                   pl.BlockSpec((tm, B_WIDTH), lambda i: (i, 0))],
        out_shape=[jax.ShapeDtypeStruct((t, 3 * A_WIDTH), BF16),
                   jax.ShapeDtypeStruct((t, B_WIDTH), F32)],
        compiler_params=_cparams(("parallel",)),
        name="proj0",
    )(x2, g, shift, scale, w_bf, ct, sa, sb)


def _dil_body(has_state, final, q_ref, kp_ref, kc_ref, vp_ref, vc_ref, *rest):
    if has_state:
        m_in, l_in, o_in = rest[:3]
        rest = rest[3:]
    outs = rest
    i = pl.program_id(2)
    q = q_ref[0]
    kcat = jnp.concatenate([kp_ref[0], kc_ref[0]], axis=0)
    vcat = jnp.concatenate([vp_ref[0], vc_ref[0]], axis=0)
    row = lax.broadcasted_iota(I32, (Q_BLOCK, 2 * Q_BLOCK), 0)
    col = lax.broadcasted_iota(I32, (Q_BLOCK, 2 * Q_BLOCK), 1)
    rel = row + Q_BLOCK - col
    valid = (rel >= 0) & (rel <= Q_BLOCK) & ((col >= Q_BLOCK) | (i > 0))
    lane = lax.broadcasted_iota(I32, (1, LANES), 1)
    for p in range(A_WIDTH // LANES):
        sl = slice(p * LANES, (p + 1) * LANES)
        qp, kp, vp = q[:, sl], kcat[:, sl], vcat[:, sl]
        m_pair = l_pair = o_pair = None
        for hh in range(LANES // HEAD_DIM):
            hm = (lane // HEAD_DIM) == hh
            qh = jnp.where(hm, qp, jnp.zeros_like(qp))
            s = _dot_nt(qh, kp)
            s = jnp.where(valid, s, -jnp.inf)
            m = jnp.max(s, axis=1, keepdims=True)
            pr = jnp.exp(s - m)
            l = jnp.sum(pr, axis=1, keepdims=True)
            o = _dot(pr.astype(BF16), vp)
            if hh == 0:
                m_pair = jnp.broadcast_to(m, (Q_BLOCK, LANES))
                l_pair = jnp.broadcast_to(l, (Q_BLOCK, LANES))
                o_pair = o
            else:
                m_pair = jnp.where(hm, m, m_pair)
                l_pair = jnp.where(hm, l, l_pair)
                o_pair = jnp.where(hm, o, o_pair)
        if has_state:
            m0, l0, o0 = m_in[0, :, sl], l_in[0, :, sl], o_in[0, :, sl]
            mm = jnp.maximum(m0, m_pair)
            w0, w1 = jnp.exp(m0 - mm), jnp.exp(m_pair - mm)
            l_pair = w0 * l0 + w1 * l_pair
            o_pair = w0 * o0 + w1 * o_pair
            m_pair = mm
        if final:
            outs[0][0, :, sl] = (o_pair / l_pair).astype(outs[0].dtype)
        else:
            outs[0][0, :, sl] = m_pair
            outs[1][0, :, sl] = l_pair
            outs[2][0, :, sl] = o_pair


def _dilated_branch(qkv3, dil, state, final, bn, seq):
    sd = seq // dil
    nb = sd // Q_BLOCK
    qkv_v = qkv3.reshape(bn, sd, dil * 3 * A_WIDTH)
    blk = (1, Q_BLOCK, A_WIDTH)
    q_spec = pl.BlockSpec(blk, lambda b, r, i: (b, i, r * 3))
    kp_spec = pl.BlockSpec(blk, lambda b, r, i: (b, jnp.maximum(i - 1, 0), r * 3 + 1))
    kc_spec = pl.BlockSpec(blk, lambda b, r, i: (b, i, r * 3 + 1))
    vp_spec = pl.BlockSpec(blk, lambda b, r, i: (b, jnp.maximum(i - 1, 0), r * 3 + 2))
    vc_spec = pl.BlockSpec(blk, lambda b, r, i: (b, i, r * 3 + 2))
    st_spec = pl.BlockSpec(blk, lambda b, r, i: (b, i, r))
    in_specs = [q_spec, kp_spec, kc_spec, vp_spec, vc_spec]
    args = [qkv_v] * 5
    if state is not None:
        in_specs += [st_spec] * 3
        args += [s.reshape(bn, sd, dil * A_WIDTH) for s in state]
    if final:
        out_specs = [st_spec]
        out_shape = [jax.ShapeDtypeStruct((bn, sd, dil * A_WIDTH), BF16)]
    else:
        out_specs = [st_spec] * 3
        out_shape = [jax.ShapeDtypeStruct((bn, sd, dil * A_WIDTH), F32)] * 3
    outs = pl.pallas_call(
        functools.partial(_dil_body, state is not None, final),
        grid=(bn, dil, nb),
        in_specs=in_specs, out_specs=out_specs, out_shape=out_shape,
        compiler_params=_cparams(("parallel", "parallel", "arbitrary")),
        name=f"dilated_d{dil}",
    )(*args)
    return [o.reshape(bn, seq, A_WIDTH) for o in outs]


def _dilated_attention(qkv, bn, seq):
    qkv3 = qkv.reshape(bn, seq, 3 * A_WIDTH)
    state = None
    for gi, (_, dil) in enumerate(A_PATTERNS):
        final = gi == len(A_PATTERNS) - 1
        res = _dilated_branch(qkv3, dil, state, final, bn, seq)
        state = res
    return state[0].reshape(bn * seq, A_WIDTH)


def _s5_prepare(a_re, a_im, log_dt, b_re, b_im, c_re, c_im, d_skip, nchunk):
    dt = jnp.exp(log_dt.astype(F32))[:, None]
    lr, li = a_re.astype(F32), a_im.astype(F32)
    mag = jnp.exp(lr * dt)
    ar = mag * jnp.cos(li * dt)
    ai = mag * jnp.sin(li * dt)
    den = lr * lr + li * li
    nr = ar - 1.0
    fr = (nr * lr + ai * li) / den
    fi = (ai * lr - nr * li) / den
    br, bi = b_re.astype(F32), b_im.astype(F32)
    bbr = fr[..., None] * br - fi[..., None] * bi
    bbi = fr[..., None] * bi + fi[..., None] * br
    nblk = B_WIDTH // LANES
    gpb = B_GROUPS // nblk
    eye = jnp.eye(gpb, dtype=F32)

    def bdiag_in(m):
        m = jnp.swapaxes(m.reshape(nblk, gpb, B_STATE, B_GROUP), 2, 3)
        m = m[:, :, :, None, :] * eye[None, :, None, :, None]
        return m.reshape(nblk, gpb * B_GROUP, gpb * B_STATE)

    def bdiag_out(m):
        m = jnp.swapaxes(m.reshape(nblk, gpb, B_GROUP, B_STATE), 2, 3)
        m = m[:, :, :, None, :] * eye[None, :, None, :, None]
        return m.reshape(nblk, gpb * B_STATE, gpb * B_GROUP)

    pr, pi = ar, ai
    for _ in range(int(math.log2(S5_CHUNK))):
        pr, pi = pr * pr - pi * pi, 2.0 * pr * pi
    pows_r, pows_i = [], []
    for _ in range(max(1, int(math.ceil(math.log2(nchunk))))):
        pows_r.append(pr.reshape(1, -1))
        pows_i.append(pi.reshape(1, -1))
        pr, pi = pr * pr - pi * pi, 2.0 * pr * pi
    return dict(
        ar=ar.reshape(1, -1), ai=ai.reshape(1, -1),
        b_re=bdiag_in(bbr).astype(BF16), b_im=bdiag_in(bbi).astype(BF16),
        c_re=bdiag_out(c_re.astype(F32)).astype(BF16), c_im=bdiag_out(-c_im.astype(F32)).astype(BF16),
        d=d_skip.astype(F32).reshape(1, B_WIDTH),
        pows_r=jnp.concatenate(pows_r, axis=0), pows_i=jnp.concatenate(pows_i, axis=0))


def _s5_local_body(u_ref, bre_ref, bim_ref, ar_ref, ai_ref, sre_ref, sim_ref):
    tr = u_ref.shape[0]
    sw = bre_ref.shape[2]
    for blk in range(B_WIDTH // LANES):
        arb = ar_ref[:, blk * sw:(blk + 1) * sw]
        aib = ai_ref[:, blk * sw:(blk + 1) * sw]
        sr = jnp.zeros((tr, sw), F32)
        si = jnp.zeros((tr, sw), F32)
        for t in range(S5_CHUNK):
            ub = u_ref[:, t * B_WIDTH + blk * LANES:t * B_WIDTH + (blk + 1) * LANES].astype(BF16)
            sr, si = (arb * sr - aib * si + _dot(ub, bre_ref[blk]),
                      arb * si + aib * sr + _dot(ub, bim_ref[blk]))
        sre_ref[:, blk * sw:(blk + 1) * sw] = sr
        sim_ref[:, blk * sw:(blk + 1) * sw] = si


def _s5_scan_body(nsteps, sre_ref, sim_ref, pr_ref, pi_ref, xre_ref, xim_ref):
    nc = sre_ref.shape[0]
    sw = 512
    row = lax.broadcasted_iota(I32, (nc, 1), 0)
    for cb in range(sre_ref.shape[1] // sw):
        sl = slice(cb * sw, (cb + 1) * sw)
        xr, xi = sre_ref[:, sl], sim_ref[:, sl]
        for s in range(nsteps):
            sh = 1 << s
            ok = row >= sh
            rr = jnp.where(ok, pltpu.roll(xr, sh, 0), 0.0)
            ri = jnp.where(ok, pltpu.roll(xi, sh, 0), 0.0)
            pr, pi = pr_ref[s:s + 1, sl], pi_ref[s:s + 1, sl]
            xr, xi = xr + pr * rr - pi * ri, xi + pr * ri + pi * rr
        ok = row >= 1
        xre_ref[:, sl] = jnp.where(ok, pltpu.roll(xr, 1, 0), 0.0)
        xim_ref[:, sl] = jnp.where(ok, pltpu.roll(xi, 1, 0), 0.0)


def _s5_out_body(u_ref, xre_ref, xim_ref, bre_ref, bim_ref, cre_ref, cim_ref, ar_ref, ai_ref,
                 d_ref, wglu_ref, bglu_ref, out_ref, y_scr):
    sw = bre_ref.shape[2]
    for blk in range(B_WIDTH // LANES):
        arb = ar_ref[:, blk * sw:(blk + 1) * sw]
        aib = ai_ref[:, blk * sw:(blk + 1) * sw]
        xr = xre_ref[:, blk * sw:(blk + 1) * sw]
        xi = xim_ref[:, blk * sw:(blk + 1) * sw]
        db = d_ref[:, blk * LANES:(blk + 1) * LANES]
        for t in range(S5_CHUNK):
            cs = slice(t * B_WIDTH + blk * LANES, t * B_WIDTH + (blk + 1) * LANES)
            uf = u_ref[:, cs]
            ub = uf.astype(BF16)
            xr, xi = (arb * xr - aib * xi + _dot(ub, bre_ref[blk]),
                      arb * xi + aib * xr + _dot(ub, bim_ref[blk]))
            y = _dot(xr.astype(BF16), cre_ref[blk]) + _dot(xi.astype(BF16), cim_ref[blk]) + db * uf
            y_scr[:, cs] = jax.nn.gelu(y, approximate=True)
    for t in range(S5_CHUNK):
        cs = slice(t * B_WIDTH, (t + 1) * B_WIDTH)
        y = y_scr[:, cs]
        z = _dot(y.astype(BF16), wglu_ref[...]) + bglu_ref[...]
        out_ref[:, cs] = (y * jax.nn.sigmoid(z)).astype(out_ref.dtype)


def _s5_mixer(s_in, prm, w_glu_bf, b_glu, bn, seq, tr=128):
    t = bn * seq
    nrow = t // S5_CHUNK
    ncb = seq // S5_CHUNK
    width = S5_CHUNK * B_WIDTH
    sdim = B_GROUPS * B_STATE
    uc = s_in.reshape(nrow, width)
    nblk = B_WIDTH // LANES
    const3 = lambda i: (0, 0, 0)
    const2 = lambda i: (0, 0)
    b_spec = pl.BlockSpec((nblk, LANES, sdim // nblk), const3)
    c_spec = pl.BlockSpec((nblk, sdim // nblk, LANES), const3)
    a_spec = pl.BlockSpec((1, sdim), const2)
    s_re, s_im = pl.pallas_call(
        _s5_local_body,
        grid=(nrow // tr,),
        in_specs=[pl.BlockSpec((tr, width), lambda i: (i, 0)), b_spec, b_spec, a_spec, a_spec],
        out_specs=[pl.BlockSpec((tr, sdim), lambda i: (i, 0))] * 2,
        out_shape=[jax.ShapeDtypeStruct((nrow, sdim), F32)] * 2,
        compiler_params=_cparams(("parallel",)),
        name="s5_local",
    )(uc, prm['b_re'], prm['b_im'], prm['ar'], prm['ai'])
    nsteps = prm['pows_r'].shape[0]
    x_re, x_im = pl.pallas_call(
        functools.partial(_s5_scan_body, nsteps),
        grid=(bn,),
        in_specs=[pl.BlockSpec((ncb, sdim), lambda b: (b, 0))] * 2
                 + [pl.BlockSpec((nsteps, sdim), lambda b: (0, 0))] * 2,
        out_specs=[pl.BlockSpec((ncb, sdim), lambda b: (b, 0))] * 2,
        out_shape=[jax.ShapeDtypeStruct((nrow, sdim), F32)] * 2,
        compiler_params=_cparams(("parallel",)),
        name="s5_scan",
    )(s_re, s_im, prm['pows_r'], prm['pows_i'])
    y = pl.pallas_call(
        _s5_out_body,
        grid=(nrow // tr,),
        in_specs=[pl.BlockSpec((tr, width), lambda i: (i, 0)),
                  pl.BlockSpec((tr, sdim), lambda i: (i, 0)),
                  pl.BlockSpec((tr, sdim), lambda i: (i, 0)),
                  b_spec, b_spec, c_spec, c_spec, a_spec, a_spec,
                  pl.BlockSpec((1, B_WIDTH), const2),
                  pl.BlockSpec((B_WIDTH, B_WIDTH), const2),
                  pl.BlockSpec((1, B_WIDTH), const2)],
        out_specs=pl.BlockSpec((tr, width), lambda i: (i, 0)),
        out_shape=jax.ShapeDtypeStruct((nrow, width), BF16),
        scratch_shapes=[pltpu.VMEM((tr, width), F32)],
        compiler_params=_cparams(("parallel",)),
        name="s5_out",
    )(uc, x_re, x_im, prm['b_re'], prm['b_im'], prm['c_re'], prm['c_im'], prm['ar'], prm['ai'],
      prm['d'], w_glu_bf, b_glu.reshape(1, B_WIDTH).astype(F32))
    return y.reshape(t, B_WIDTH)


def _outproj_body(nparts, *refs):
    parts = refs[:nparts]
    w_ref, h_ref, gt_ref, o_ref = refs[nparts:]
    acc = None
    off = 0
    for p in parts:
        k = p.shape[1]
        d = _dot(p[...].astype(BF16), w_ref[off:off + k, :])
        acc = d if acc is None else acc + d
        off += k
    o_ref[...] = h_ref[...] + gt_ref[0] * acc


def _outproj(parts, w_bf, h2, gate, seq, tm=512):
    t, d = h2.shape
    tpb = seq // tm
    in_specs = [pl.BlockSpec((tm, p.shape[1]), lambda i: (i, 0)) for p in parts]
    in_specs += [pl.BlockSpec(w_bf.shape, lambda i: (0, 0)),
                 pl.BlockSpec((tm, d), lambda i: (i, 0)),
                 pl.BlockSpec((1, 1, d), lambda i: (i // tpb, 0, 0))]
    return pl.pallas_call(
        functools.partial(_outproj_body, len(parts)),
        grid=(t // tm,),
        in_specs=in_specs,
        out_specs=pl.BlockSpec((tm, d), lambda i: (i, 0)),
        out_shape=jax.ShapeDtypeStruct((t, d), F32),
        compiler_params=_cparams(("parallel",)),
        name="outproj",
    )(*parts, w_bf, h2, gate)


def _ffn_body(h_ref, g_ref, sh_ref, sc_ref, gt_ref, wg_ref, wu_ref, wd_ref, o_ref, u_scr, acc_scr):
    f = pl.program_id(1)

    @pl.when(f == 0)
    def _():
        u_scr[...] = _norm_mod(h_ref[...], g_ref[...], sh_ref[0], sc_ref[0]).astype(BF16)
        acc_scr[...] = jnp.zeros_like(acc_scr)

    u = u_scr[...]
    gp = _dot(u, wg_ref[...])
    up = _dot(u, wu_ref[...])
    a = (gp * jax.nn.sigmoid(gp) * up).astype(BF16)
    acc_scr[...] += _dot(a, wd_ref[...])

    @pl.when(f == pl.num_programs(1) - 1)
    def _():
        o_ref[...] = h_ref[...] + gt_ref[0] * acc_scr[...]


def _ffn_dense(h2, g, shift, scale, gate, wg, wu, wd, seq, tm=512, tf=256):
    t, d = h2.shape
    ff = wg.shape[1]
    tpb = seq // tm
    mod_spec = pl.BlockSpec((1, 1, d), lambda i, f: (i // tpb, 0, 0))
    return pl.pallas_call(
        _ffn_body,
        grid=(t // tm, ff // tf),
        in_specs=[pl.BlockSpec((tm, d), lambda i, f: (i, 0)),
                  pl.BlockSpec((1, d), lambda i, f: (0, 0)),
                  mod_spec, mod_spec, mod_spec,
                  pl.BlockSpec((d, tf), lambda i, f: (0, f)),
                  pl.BlockSpec((d, tf), lambda i, f: (0, f)),
                  pl.BlockSpec((tf, d), lambda i, f: (f, 0))],
        out_specs=pl.BlockSpec((tm, d), lambda i, f: (i, 0)),
        out_shape=jax.ShapeDtypeStruct((t, d), F32),
        scratch_shapes=[pltpu.VMEM((tm, d), BF16), pltpu.VMEM((tm, d), F32)],
        compiler_params=_cparams(("parallel", "arbitrary")),
        name="ffn_dense",
    )(h2, g, shift, scale, gate, wg, wu, wd)


P1_QR = 0
P1_QN = 256
P1_CKV = 1280
P1_KR = 1536
P1_QI = 1792
P1_KI = 2304
P1_WI = 2432
P1_COLS = 2560


def _proj1_layout(w_in):
    d = w_in.shape[0]
    c0 = 0
    w_qr = w_in[:, c0:c0 + C_HEADS * ROPE_DIM]; c0 += C_HEADS * ROPE_DIM
    w_qn = w_in[:, c0:c0 + C_HEADS * C_NOPE]; c0 += C_HEADS * C_NOPE
    w_ckv = w_in[:, c0:c0 + C_LATENT]; c0 += C_LATENT
    w_kr = w_in[:, c0:c0 + ROPE_DIM]; c0 += ROPE_DIM
    w_qi = w_in[:, c0:c0 + IDX_HEADS * IDX_DIM]; c0 += IDX_HEADS * IDX_DIM
    w_ki = w_in[:, c0:c0 + IDX_DIM]; c0 += IDX_DIM
    w_wi = w_in[:, c0:c0 + IDX_HEADS]
    w_qn = jnp.pad(w_qn.reshape(d, C_HEADS, C_NOPE), ((0, 0), (0, 0), (0, HEAD_DIM - C_NOPE)))
    w_qn = w_qn.reshape(d, C_HEADS * HEAD_DIM)
    w_kr = jnp.tile(w_kr, (1, C_HEADS))
    w_ki = jnp.tile(w_ki, (1, 2))
    w_wi = jnp.pad(w_wi, ((0, 0), (0, LANES - IDX_HEADS)))
    return jnp.concatenate([w_qr, w_qn, w_ckv, w_kr, w_qi, w_ki, w_wi], axis=1)


def _proj1_body(x_ref, g_ref, sh_ref, sc_ref, w_ref, gkv_ref,
                ct16_ref, sa16_ref, sb16_ref, ct64_ref, sa64_ref, sb64_ref,
                qa_ref, kp_ref, qi_ref, ki_ref, wi_ref):
    u = _norm_mod(x_ref[...], g_ref[...], sh_ref[0], sc_ref[0]).astype(BF16)
    t16 = (ct16_ref[...], sa16_ref[...], sb16_ref[...])
    t64 = (ct64_ref[...], sa64_ref[...], sb64_ref[...])
    qscale = HEAD_DIM ** -0.5

    def cols(lo, hi):
        return _dot(u, w_ref[:, lo:hi])

    a = cols(P1_QR, P1_QN)
    for c in range(2):
        blk = _rope128(a[:, c * LANES:(c + 1) * LANES], *t16) * qscale
        qa_ref[:, c * LANES:(c + 1) * LANES] = blk.astype(BF16)
    qa_ref[:, P1_QN:P1_CKV] = (cols(P1_QN, P1_CKV) * qscale).astype(BF16)
    ckv = cols(P1_CKV, P1_KR)
    ms = jnp.mean(ckv * ckv, axis=-1, keepdims=True)
    kp_ref[:, 0:C_LATENT] = (ckv * lax.rsqrt(ms + NORM_EPS) * gkv_ref[...]).astype(BF16)
    a = cols(P1_KR, P1_QI)
    for c in range(2):
        blk = _rope128(a[:, c * LANES:(c + 1) * LANES], *t16)
        kp_ref[:, C_LATENT + c * LANES:C_LATENT + (c + 1) * LANES] = blk.astype(BF16)
    a = cols(P1_QI, P1_KI)
    for c in range(4):
        qi_ref[:, c * LANES:(c + 1) * LANES] = _rope128(a[:, c * LANES:(c + 1) * LANES], *t64).astype(BF16)
    ki_ref[...] = _rope128(cols(P1_KI, P1_WI), *t64).astype(BF16)
    wi_ref[...] = cols(P1_WI, P1_COLS) * (IDX_HEADS ** -0.5) * (IDX_DIM ** -0.5)


def _proj1(x2, g, shift, scale, w_bf, gkv, t16, t64, seq, tm=512):
    t, d = x2.shape
    tpb = seq // tm
    tab_spec = pl.BlockSpec((tm, LANES), lambda i: (i % tpb, 0))
    mod_spec = pl.BlockSpec((1, 1, d), lambda i: (i // tpb, 0, 0))
    widths = (P1_CKV, 2 * C_LATENT, IDX_HEADS * IDX_DIM, LANES, LANES)
    dtypes = (BF16, BF16, BF16, BF16, F32)
    return pl.pallas_call(
        _proj1_body,
        grid=(t // tm,),
        in_specs=[pl.BlockSpec((tm, d), lambda i: (i, 0)),
                  pl.BlockSpec((1, d), lambda i: (0, 0)),
                  mod_spec, mod_spec,
                  pl.BlockSpec((d, P1_COLS), lambda i: (0, 0)),
                  pl.BlockSpec((1, C_LATENT), lambda i: (0, 0))] + [tab_spec] * 6,
        out_specs=[pl.BlockSpec((tm, w), lambda i: (i, 0)) for w in widths],
        out_shape=[jax.ShapeDtypeStruct((t, w), dt) for w, dt in zip(widths, dtypes)],
        compiler_params=_cparams(("parallel",)),
        name="proj1",
    )(x2, g, shift, scale, w_bf, gkv, *t16, *t64)


IDX_KT = 512


def _idx_body(topk, qi_ref, ki_ref, wi_ref, mask_ref, key_scr):
    i = pl.program_id(1)
    seq = ki_ref.shape[1]
    nkt = seq // IDX_KT
    q = qi_ref[0]
    w = wi_ref[0]
    lane = lax.broadcasted_iota(I32, (1, LANES), 1)
    qpos = i * Q_BLOCK + lax.broadcasted_iota(I32, (Q_BLOCK, 1), 0)
    lhs = []
    for h in range(IDX_HEADS):
        qp = q[:, (h // 2) * LANES:(h // 2 + 1) * LANES]
        lhs.append(jnp.where((lane // IDX_DIM) == (h % 2), qp, jnp.zeros_like(qp)))
    wcol = [w[:, h:h + 1] for h in range(IDX_HEADS)]
    for kt in range(nkt):
        kk = ki_ref[0, kt * IDX_KT:(kt + 1) * IDX_KT, :]
        acc = jnp.zeros((Q_BLOCK, IDX_KT), F32)
        for h in range(IDX_HEADS):
            acc = acc + wcol[h] * jnp.maximum(_dot_nt(lhs[h], kk), 0.0)
        acc = acc + 0.0
        bits = lax.bitcast_convert_type(acc, I32)
        key = jnp.where(bits < 0, bits ^ jnp.int32(0x7FFFFFFF), bits)
        kpos = kt * IDX_KT + lax.broadcasted_iota(I32, (1, IDX_KT), 1)
        key_scr[:, kt * IDX_KT:(kt + 1) * IDX_KT] = jnp.where(kpos <= qpos, key, jnp.int32(INT_MIN))

    kq = jnp.minimum(qpos + 1, topk).astype(F32)

    def count(pred):
        return jnp.sum(jnp.where(pred, 1.0, 0.0), axis=1, keepdims=True)

    def bit_step(it, ans):
        cand = ans | lax.shift_left(jnp.int32(1), 31 - it)
        thr = cand ^ jnp.int32(INT_MIN)
        cnt = count(key_scr[...] >= thr)
        return jnp.where(cnt >= kq, cand, ans)

    ans = lax.fori_loop(0, 32, bit_step, jnp.zeros((Q_BLOCK, 1), I32))
    thr = ans ^ jnp.int32(INT_MIN)
    keys = key_scr[...]
    need = kq - count(keys > thr)
    kidx = lax.broadcasted_iota(I32, (1, seq), 1)
    nbits = int(math.log2(seq))

    def tie_step(it, ans2):
        cand = ans2 | lax.shift_left(jnp.int32(1), nbits - 1 - it)
        cnt = count((key_scr[...] == thr) & (kidx < cand))
        return jnp.where(cnt < need, cand, ans2)

    jcut = lax.fori_loop(0, nbits, tie_step, jnp.zeros((Q_BLOCK, 1), I32))
    sel = (keys > thr) | ((keys == thr) & (kidx <= jcut))
    selb = jnp.where(sel, 1.0, 0.0).astype(BF16)
    for kt in range(nkt):
        mask_ref[0, 0, kt] = selb[:, kt * IDX_KT:(kt + 1) * IDX_KT]


def _dsa_select(qi, ki, wi, bn, seq):
    topk = min(TOPK_MAX, seq // 4)
    nqb = seq // Q_BLOCK
    nkt = seq // IDX_KT
    return pl.pallas_call(
        functools.partial(_idx_body, topk),
        grid=(bn, nqb),
        in_specs=[pl.BlockSpec((1, Q_BLOCK, IDX_HEADS * IDX_DIM), lambda b, i: (b, i, 0)),
                  pl.BlockSpec((1, seq, LANES), lambda b, i: (b, 0, 0)),
                  pl.BlockSpec((1, Q_BLOCK, LANES), lambda b, i: (b, i, 0))],
        out_specs=pl.BlockSpec((1, 1, nkt, Q_BLOCK, IDX_KT), lambda b, i: (b, i, 0, 0, 0)),
        out_shape=jax.ShapeDtypeStruct((bn, nqb, nkt, Q_BLOCK, IDX_KT), BF16),
        scratch_shapes=[pltpu.VMEM((Q_BLOCK, seq), I32)],
        compiler_params=_cparams(("parallel", "arbitrary")),
        name="dsa_select",
    )(qi.reshape(bn, seq, -1), ki.reshape(bn, seq, -1), wi.reshape(bn, seq, -1))


def _dsa_attn_body(qa_ref, kp_ref, mask_ref, wuk_ref, wuv_ref, o_ref, qp_scr, m_scr, l_scr, acc_scr):
    i = pl.program_id(1)
    rows = C_HEADS * Q_BLOCK
    qr = qa_ref[0, :, 0:P1_QN]
    lane = lax.broadcasted_iota(I32, (1, C_HEADS * ROPE_DIM), 1)
    for h in range(C_HEADS):
        p = h // 2
        qn = qa_ref[0, :, P1_QN + p * LANES:P1_QN + (p + 1) * LANES]
        qlat = _dot(qn, wuk_ref[h])
        qp_scr[h * Q_BLOCK:(h + 1) * Q_BLOCK, 0:C_LATENT] = qlat.astype(BF16)
        qp_scr[h * Q_BLOCK:(h + 1) * Q_BLOCK, C_LATENT:2 * C_LATENT] = jnp.where(
            (lane // ROPE_DIM) == h, qr, jnp.zeros_like(qr))
    m_scr[...] = jnp.full(m_scr.shape, NEG_BIG, F32)
    l_scr[...] = jnp.zeros_like(l_scr)
    acc_scr[...] = jnp.zeros_like(acc_scr)

    def step(kt, carry):
        kk = kp_ref[0, pl.ds(pl.multiple_of(kt * IDX_KT, IDX_KT), IDX_KT), :]
        s = _dot_nt(qp_scr[...], kk)
        keep = mask_ref[0, 0, kt] > 0
        s = jnp.where(keep[None], s.reshape(C_HEADS, Q_BLOCK, IDX_KT), NEG_BIG).reshape(rows, IDX_KT)
        m_old = m_scr[...]
        m_new = jnp.maximum(m_old, jnp.max(s, axis=1, keepdims=True))
        alpha = jnp.exp(m_old - m_new)
        pr = jnp.exp(s - m_new)
        l_scr[...] = alpha * l_scr[...] + jnp.sum(pr, axis=1, keepdims=True)
        acc_scr[...] = alpha * acc_scr[...] + _dot(pr.astype(BF16), kk[:, 0:C_LATENT])
        m_scr[...] = m_new
        return carry

    lax.fori_loop(0, (i * Q_BLOCK) // IDX_KT + 1, step, 0)
    olat = (acc_scr[...] / l_scr[...]).astype(BF16)
    for p in range(C_HEADS // 2):
        o = (_dot(olat[(2 * p) * Q_BLOCK:(2 * p + 1) * Q_BLOCK], wuv_ref[2 * p])
             + _dot(olat[(2 * p + 1) * Q_BLOCK:(2 * p + 2) * Q_BLOCK], wuv_ref[2 * p + 1]))
        o_ref[0, :, p * LANES:(p + 1) * LANES] = o.astype(o_ref.dtype)


def _dsa_attention(qa, kp, mask, wuk_x, wuv_x, bn, seq):
    nqb = seq // Q_BLOCK
    nkt = seq // IDX_KT
    rows = C_HEADS * Q_BLOCK
    return pl.pallas_call(
        _dsa_attn_body,
        grid=(bn, nqb),
        in_specs=[pl.BlockSpec((1, Q_BLOCK, P1_CKV), lambda b, i: (b, i, 0)),
                  pl.BlockSpec((1, seq, 2 * C_LATENT), lambda b, i: (b, 0, 0)),
                  pl.BlockSpec((1, 1, nkt, Q_BLOCK, IDX_KT), lambda b, i: (b, i, 0, 0, 0)),
                  pl.BlockSpec((C_HEADS, LANES, C_LATENT), lambda b, i: (0, 0, 0)),
                  pl.BlockSpec((C_HEADS, C_LATENT, LANES), lambda b, i: (0, 0, 0))],
        out_specs=pl.BlockSpec((1, Q_BLOCK, C_HEADS * HEAD_DIM), lambda b, i: (b, i, 0)),
        out_shape=jax.ShapeDtypeStruct((bn, seq, C_HEADS * HEAD_DIM), BF16),
        scratch_shapes=[pltpu.VMEM((rows, 2 * C_LATENT), BF16),
                        pltpu.VMEM((rows, 1), F32), pltpu.VMEM((rows, 1), F32),
                        pltpu.VMEM((rows, C_LATENT), F32)],
        compiler_params=_cparams(("parallel", "arbitrary")),
        name="dsa_attn",
    )(qa.reshape(bn, seq, -1), kp.reshape(bn, seq, -1), mask, wuk_x, wuv_x)


def _mla_weights(w_uk, w_uv):
    wuk = jnp.transpose(w_uk, (1, 2, 0))
    wuk_x = jnp.zeros((C_HEADS, LANES, C_LATENT), F32)
    wuv = jnp.transpose(w_uv, (1, 0, 2))
    wuv_x = jnp.zeros((C_HEADS, C_LATENT, LANES), F32)
    for h in range(C_HEADS):
        o = (h % 2) * HEAD_DIM
        wuk_x = wuk_x.at[h, o:o + C_NOPE, :].set(wuk[h])
        wuv_x = wuv_x.at[h, :, o:o + HEAD_DIM].set(wuv[h])
    return wuk_x.astype(BF16), wuv_x.astype(BF16)


def _router_body(h_ref, g_ref, sh_ref, sc_ref, wr_ref, br_ref, route_ref, up_ref, cnt_ref, carry_scr):
    i = pl.program_id(0)
    tm = h_ref.shape[0]

    @pl.when(i == 0)
    def _():
        carry_scr[...] = jnp.zeros_like(carry_scr)

    u = _norm_mod(h_ref[...], g_ref[...], sh_ref[0], sc_ref[0])
    logits = jnp.dot(u, wr_ref[...], preferred_element_type=F32, precision=HIGHEST) + br_ref[...]
    lane = lax.broadcasted_iota(I32, (tm, LANES), 1).astype(F32)
    m1 = jnp.max(logits, axis=1, keepdims=True)
    e1 = jnp.min(jnp.where(logits == m1, lane, float(LANES)), axis=1, keepdims=True)
    rest = jnp.where(lane == e1, NEG_BIG * 2, logits)
    m2 = jnp.max(rest, axis=1, keepdims=True)
    e2 = jnp.min(jnp.where(rest == m2, lane, float(LANES)), axis=1, keepdims=True)
    ex = jnp.exp(m2 - m1)
    g1 = 1.0 / (1.0 + ex)
    g2 = ex / (1.0 + ex)
    onehot = ((lane == e1) | (lane == e2))
    oh_bf = jnp.where(onehot, 1.0, 0.0).astype(BF16)
    r = lax.broadcasted_iota(I32, (tm, tm), 0)
    c = lax.broadcasted_iota(I32, (tm, tm), 1)
    tri = jnp.where(c < r, 1.0, 0.0).astype(BF16)
    prefix = _dot(tri, oh_bf) + carry_scr[...]
    rank1 = jnp.sum(jnp.where(lane == e1, prefix, 0.0), axis=1, keepdims=True)
    rank2 = jnp.sum(jnp.where(lane == e2, prefix, 0.0), axis=1, keepdims=True)
    carry_scr[...] = carry_scr[...] + jnp.sum(jnp.where(onehot, 1.0, 0.0), axis=0, keepdims=True)
    cnt_ref[...] = carry_scr[...]
    vals = [e1, e2, g1, g2, rank1, rank2]
    route = jnp.zeros((tm, LANES), F32)
    for k, v in enumerate(vals):
        route = jnp.where(lane == float(k), v, route)
    route_ref[...] = route
    half = u.shape[1] // 2
    lo = lax.bitcast_convert_type(u[:, :half].astype(BF16).astype(F32), jnp.uint32)
    hi = lax.bitcast_convert_type(u[:, half:].astype(BF16).astype(F32), jnp.uint32)
    up_ref[...] = (hi & jnp.uint32(0xFFFF0000)) | (lo >> 16)


def _router(h2, g, shift, scale, w_router, b_router, seq, tm=512):
    t, d = h2.shape
    tpb = seq // tm
    wr = jnp.pad(w_router.astype(F32), ((0, 0), (0, LANES - N_EXPERTS)))
    br = jnp.pad(b_router.astype(F32), (0, LANES - N_EXPERTS), constant_values=NEG_BIG).reshape(1, LANES)
    mod_spec = pl.BlockSpec((1, 1, d), lambda i: (i // tpb, 0, 0))
    return pl.pallas_call(
        _router_body,
        grid=(t // tm,),
        in_specs=[pl.BlockSpec((tm, d), lambda i: (i, 0)),
                  pl.BlockSpec((1, d), lambda i: (0, 0)),
                  mod_spec, mod_spec,
                  pl.BlockSpec((d, LANES), lambda i: (0, 0)),
                  pl.BlockSpec((1, LANES), lambda i: (0, 0))],
        out_specs=[pl.BlockSpec((tm, LANES), lambda i: (i, 0)),
                   pl.BlockSpec((tm, d // 2), lambda i: (i, 0)),
                   pl.BlockSpec((1, LANES), lambda i: (0, 0))],
        out_shape=[jax.ShapeDtypeStruct((t, LANES), F32),
                   jax.ShapeDtypeStruct((t, d // 2), jnp.uint32),
                   jax.ShapeDtypeStruct((1, LANES), F32)],
        scratch_shapes=[pltpu.VMEM((1, LANES), F32)],
        compiler_params=_cparams(("arbitrary",)),
        name="moe_router",
    )(h2, g, shift, scale, wr, br)


def _dispatch_body(p1_ref, p2_ref, up_ref, xs_in_ref, xs_ref, sem):
    del xs_in_ref
    tm = up_ref.shape[0]

    def copy(r, dst):
        return pltpu.make_async_copy(up_ref.at[pl.ds(r, 1)], xs_ref.at[pl.ds(dst, 1)], sem)

    def start(r, c):
        copy(r, p1_ref[r]).start()
        copy(r, p2_ref[r]).start()
        return c

    lax.fori_loop(0, tm, start, 0)

    def wait(r, c):
        copy(r, p1_ref[r]).wait()
        copy(r, p2_ref[r]).wait()
        return c

    lax.fori_loop(0, tm, wait, 0)


def _dispatch(pos1, pos2, up, nrows, tm=256):
    t, w = up.shape
    xs0 = jnp.zeros((nrows, w), jnp.uint32)
    smem_spec = pl.BlockSpec((tm,), lambda i: (i,), memory_space=pltpu.SMEM)
    return pl.pallas_call(
        _dispatch_body,
        grid=(t // tm,),
        in_specs=[smem_spec, smem_spec,
                  pl.BlockSpec((tm, w), lambda i: (i, 0)),
                  pl.BlockSpec(memory_space=pl.ANY)],
        out_specs=pl.BlockSpec(memory_space=pl.ANY),
        out_shape=jax.ShapeDtypeStruct((nrows, w), jnp.uint32),
        scratch_shapes=[pltpu.SemaphoreType.DMA(())],
        input_output_aliases={3: 0},
        compiler_params=_cparams(("arbitrary",)),
        name="moe_dispatch",
    )(pos1, pos2, up, xs0)


def _expert_body(te_ref, nv_ref, xs_ref, wg_ref, wu_ref, wd_ref, y_ref, xlo_scr, xhi_scr, acc_scr):
    j = pl.program_id(0)
    f = pl.program_id(1)
    live = j < nv_ref[0]
    half = xlo_scr.shape[1]

    @pl.when(f == 0)
    def _():
        w = xs_ref[...]
        xlo_scr[...] = lax.bitcast_convert_type(w << 16, F32).astype(BF16)
        xhi_scr[...] = lax.bitcast_convert_type(w & jnp.uint32(0xFFFF0000), F32).astype(BF16)
        acc_scr[...] = jnp.zeros_like(acc_scr)

    @pl.when(live)
    def _():
        xlo, xhi = xlo_scr[...], xhi_scr[...]
        gp = _dot(xlo, wg_ref[0, 0:half, :]) + _dot(xhi, wg_ref[0, half:, :])
        up = _dot(xlo, wu_ref[0, 0:half, :]) + _dot(xhi, wu_ref[0, half:, :])
        a = (gp * jax.nn.sigmoid(gp) * up).astype(BF16)
        acc_scr[...] += _dot(a, wd_ref[0])

    @pl.when(f == pl.num_programs(1) - 1)
    def _():
        y_ref[...] = acc_scr[...]


def _experts(tile_expert, n_live, xs, wg, wu, wd, tm, tf=512):
    nrows, half = xs.shape
    d = 2 * half
    ff = wg.shape[2]
    nf = ff // tf
    nt = nrows // tm

    def f_eff(j, f, nv):
        return jnp.where(j < nv[0], f, nf - 1)

    grid_spec = pltpu.PrefetchScalarGridSpec(
        num_scalar_prefetch=2,
        grid=(nt, nf),
        in_specs=[pl.BlockSpec((tm, half), lambda j, f, te, nv: (j, 0)),
                  pl.BlockSpec((1, d, tf), lambda j, f, te, nv: (te[j], 0, f_eff(j, f, nv))),
                  pl.BlockSpec((1, d, tf), lambda j, f, te, nv: (te[j], 0, f_eff(j, f, nv))),
                  pl.BlockSpec((1, tf, d), lambda j, f, te, nv: (te[j], f_eff(j, f, nv), 0))],
        out_specs=pl.BlockSpec((tm, d), lambda j, f, te, nv: (j, 0)),
        scratch_shapes=[pltpu.VMEM((tm, half), BF16), pltpu.VMEM((tm, half), BF16),
                        pltpu.VMEM((tm, d), F32)])
    return pl.pallas_call(
        _expert_body,
        grid_spec=grid_spec,
        out_shape=jax.ShapeDtypeStruct((nrows, d), F32),
        compiler_params=_cparams(("arbitrary", "arbitrary")),
        name="moe_experts",
    )(tile_expert, n_live, xs, wg, wu, wd)


def _combine_body(p1_ref, p2_ref, route_ref, h_ref, gt_ref, gf_ref, y_ref, o_ref, y1_scr, y2_scr, sem):
    tm = h_ref.shape[0]

    def copy(src, dst_scr, r):
        return pltpu.make_async_copy(y_ref.at[pl.ds(src, 1)], dst_scr.at[pl.ds(r, 1)], sem)

    def start(r, c):
        copy(p1_ref[r], y1_scr, r).start()
        copy(p2_ref[r], y2_scr, r).start()
        return c

    lax.fori_loop(0, tm, start, 0)

    def wait(r, c):
        copy(p1_ref[r], y1_scr, r).wait()
        copy(p2_ref[r], y2_scr, r).wait()
        return c

    lax.fori_loop(0, tm, wait, 0)
    route = route_ref[...]
    g1, g2 = route[:, 2:3], route[:, 3:4]
    y = g1 * y1_scr[...] + g2 * y2_scr[...]
    hn = h_ref[...] + gt_ref[0] * y
    ms = jnp.mean(hn * hn, axis=-1, keepdims=True)
    o_ref[...] = hn * lax.rsqrt(ms + NORM_EPS) * gf_ref[...]


def _combine(pos1, pos2, route, h2, gate, g_final, y_sorted, seq, tm=256):
    t, d = h2.shape
    tpb = seq // tm
    smem_spec = pl.BlockSpec((tm,), lambda i: (i,), memory_space=pltpu.SMEM)
    return pl.pallas_call(
        _combine_body,
        grid=(t // tm,),
        in_specs=[smem_spec, smem_spec,
                  pl.BlockSpec((tm, LANES), lambda i: (i, 0)),
                  pl.BlockSpec((tm, d), lambda i: (i, 0)),
                  pl.BlockSpec((1, 1, d), lambda i: (i // tpb, 0, 0)),
                  pl.BlockSpec((1, d), lambda i: (0, 0)),
                  pl.BlockSpec(memory_space=pl.ANY)],
        out_specs=pl.BlockSpec((tm, d), lambda i: (i, 0)),
        out_shape=jax.ShapeDtypeStruct((t, d), F32),
        scratch_shapes=[pltpu.VMEM((tm, d), F32), pltpu.VMEM((tm, d), F32),
                        pltpu.SemaphoreType.DMA(())],
        compiler_params=_cparams(("arbitrary",)),
        name="moe_combine",
    )(pos1, pos2, route, h2, gate, g_final, y_sorted)


MOE_TILE = 512


def _moe_layout(route, counts):
    e1 = route[:, 0].astype(I32)
    e2 = route[:, 1].astype(I32)
    r1 = route[:, 4].astype(I32)
    r2 = route[:, 5].astype(I32)
    cnt = counts[0, :N_EXPERTS].astype(I32)
    tiles = (cnt + MOE_TILE - 1) // MOE_TILE
    tile_end = jnp.cumsum(tiles)
    start = (tile_end - tiles) * MOE_TILE
    pos1 = start[e1] + r1
    pos2 = start[e2] + r2
    nt = route.shape[0] * 2 // MOE_TILE + N_EXPERTS
    n_live = tile_end[-1]
    tid = jnp.minimum(jnp.arange(nt, dtype=I32), n_live - 1)
    tile_expert = jnp.sum((tid[:, None] >= tile_end[None, :]).astype(I32), axis=1)
    return pos1, pos2, tile_expert.astype(I32), n_live.reshape(1).astype(I32), nt * MOE_TILE


def kernel(x, c, w_ada, b_ada, g_mix, g_ffn, g_final, e_w_in, e_w_out, s5_a_re, s5_a_im, s5_log_dt,
           s5_b_re, s5_b_im, s5_c_re, s5_c_im, s5_d, s5_w_glu, s5_b_glu, ff_w_gate, ff_w_up,
           ff_w_down, o_w_in, o_w_out, mla_g_kv, mla_w_uk, mla_w_uv, moe_w_router, moe_b_router,
           moe_w_gate, moe_w_up, moe_w_down):
    bn, seq, d = x.shape
    t = bn * seq
    mod = _adaln(c, w_ada, b_ada)

    def mod_vec(layer, k):
        return mod[layer, :, k * d:(k + 1) * d].reshape(bn, 1, d)

    tabs64 = _rope_tables(seq, HEAD_DIM)
    tabs16 = _rope_tables(seq, ROPE_DIM)
    h = x.reshape(t, d)

    qkv, s_in = _proj0(h, g_mix[0].reshape(1, d), mod_vec(0, 0), mod_vec(0, 1),
                       e_w_in[0].astype(BF16), tabs64, seq)
    y_a = _dilated_attention(qkv, bn, seq)
    prm = _s5_prepare(s5_a_re[0], s5_a_im[0], s5_log_dt[0], s5_b_re[0], s5_b_im[0],
                      s5_c_re[0], s5_c_im[0], s5_d[0], seq // S5_CHUNK)
    y_b = _s5_mixer(s_in, prm, s5_w_glu[0].astype(BF16), s5_b_glu[0], bn, seq)
    h = _outproj([y_a, y_b], e_w_out[0].astype(BF16), h, mod_vec(0, 2), seq)
    h = _ffn_dense(h, g_ffn[0].reshape(1, d), mod_vec(0, 3), mod_vec(0, 4), mod_vec(0, 5),
                   ff_w_gate[0].astype(BF16), ff_w_up[0].astype(BF16), ff_w_down[0].astype(BF16), seq)

    w1 = _proj1_layout(o_w_in[0]).astype(BF16)
    qa, kp, qi, ki, wi = _proj1(h, g_mix[1].reshape(1, d), mod_vec(1, 0), mod_vec(1, 1), w1,
                                mla_g_kv[0].reshape(1, C_LATENT).astype(F32), tabs16, tabs64, seq)
    mask = _dsa_select(qi, ki, wi, bn, seq)
    wuk_x, wuv_x = _mla_weights(mla_w_uk[0], mla_w_uv[0])
    o_attn = _dsa_attention(qa, kp, mask, wuk_x, wuv_x, bn, seq)
    h = _outproj([o_attn.reshape(t, C_HEADS * HEAD_DIM)], o_w_out[0].astype(BF16), h, mod_vec(1, 2), seq)

    route, up, counts = _router(h, g_ffn[1].reshape(1, d), mod_vec(1, 3), mod_vec(1, 4),
                                moe_w_router[0], moe_b_router[0], seq)
    pos1, pos2, tile_expert, n_live, nrows = _moe_layout(route, counts)
    xs = _dispatch(pos1, pos2, up, nrows)
    y_sorted = _experts(tile_expert, n_live, xs, moe_w_gate[0].astype(BF16), moe_w_up[0].astype(BF16),
                        moe_w_down[0].astype(BF16), MOE_TILE)
    out = _combine(pos1, pos2, route, h, mod_vec(1, 5), g_final.reshape(1, d), y_sorted, seq)
    return out.reshape(bn, seq, d)
```

```python
import functools
import math

import jax
import jax.numpy as jnp
import numpy as np
from jax import lax
from jax.experimental import pallas as pl
from jax.experimental.pallas import tpu as pltpu

F32 = jnp.float32
BF16 = jnp.bfloat16
I32 = jnp.int32
HIGHEST = lax.Precision.HIGHEST

D_MODEL = 1024
HEAD_DIM = 64
ROPE_DIM = 16
ROPE_THETA = 500000.0
NORM_EPS = 1e-6
A_HEADS = 8
A_WIDTH = 512
A_PATTERNS = ((128, 1), (512, 4), (2048, 16))
B_WIDTH = 512
B_GROUP = 16
B_GROUPS = 32
B_STATE = 64
C_HEADS = 16
C_NOPE = 48
C_LATENT = 256
IDX_HEADS = 8
IDX_DIM = 64
TOPK_MAX = 256
FF_DENSE = 2816
N_EXPERTS = 8
FF_EXPERT = 3584

LANES = 128
Q_BLOCK = 128
S5_CHUNK = 16
VMEM_LIMIT = 56 * 1024 * 1024
NEG_BIG = -1e30
INT_MIN = -(2 ** 31)


def _cparams(sem, vmem=VMEM_LIMIT):
    return pltpu.CompilerParams(dimension_semantics=sem, vmem_limit_bytes=vmem)


def _dot(a, b):
    return jnp.dot(a, b, preferred_element_type=F32)


def _dot_nt(a, b):
    return lax.dot_general(a, b, (((1,), (1,)), ((), ())), preferred_element_type=F32)


def _norm_mod(x, g, shift, scale):
    ms = jnp.mean(x * x, axis=-1, keepdims=True)
    y = x * lax.rsqrt(ms + NORM_EPS) * g
    return y * (1.0 + scale) + shift


def _rope128(x, ct, sa, sb):
    return x * ct + pltpu.roll(x, LANES - ROPE_DIM // 2, 1) * sa + pltpu.roll(x, ROPE_DIM // 2, 1) * sb


def _adaln_body(c_ref, w_ref, b_ref, o_ref):
    c = c_ref[...]
    ca = c * jax.nn.sigmoid(c)
    o_ref[0] = jnp.dot(ca, w_ref[0], preferred_element_type=F32, precision=HIGHEST) + b_ref[0]


def _adaln(c, w_ada, b_ada):
    depth, d, d6 = w_ada.shape
    bn = c.shape[0]
    rows = 8
    cp = jnp.zeros((rows, d), F32).at[:bn].set(c)
    tn = 1536
    out = pl.pallas_call(
        _adaln_body,
        grid=(depth, d6 // tn),
        in_specs=[pl.BlockSpec((rows, d), lambda l, j: (0, 0)),
                  pl.BlockSpec((1, d, tn), lambda l, j: (l, 0, j)),
                  pl.BlockSpec((1, 1, tn), lambda l, j: (l, 0, j))],
        out_specs=pl.BlockSpec((1, rows, tn), lambda l, j: (l, 0, j)),
        out_shape=jax.ShapeDtypeStruct((depth, rows, d6), F32),
        compiler_params=_cparams(("arbitrary", "arbitrary")),
        name="adaln",
    )(cp, w_ada, b_ada.reshape(depth, 1, d6))
    return out[:, :bn]


def _rope_tables(seq, period):
    half = ROPE_DIM // 2
    pos = jnp.arange(seq, dtype=F32)
    inv = ROPE_THETA ** (-jnp.arange(0, ROPE_DIM, 2, dtype=F32) / ROPE_DIM)
    ang = pos[:, None] * inv[None, :]
    cos, sin = jnp.cos(ang), jnp.sin(ang)
    lane = np.arange(LANES) % period
    first = lane < half
    second = (lane >= half) & (lane < ROPE_DIM)
    idx = np.where(first, lane, np.where(second, lane - half, 0))
    cos_l, sin_l = cos[:, idx], sin[:, idx]
    ct = jnp.where(first | second, cos_l, 1.0)
    sa = jnp.where(first, -sin_l, 0.0)
    sb = jnp.where(second, sin_l, 0.0)
    return ct, sa, sb


def _proj0_body(x_ref, g_ref, sh_ref, sc_ref, w_ref, ct_ref, sa_ref, sb_ref, qkv_ref, s_ref):
    u = _norm_mod(x_ref[...], g_ref[...], sh_ref[0], sc_ref[0]).astype(BF16)
    ct, sa, sb = ct_ref[...], sa_ref[...], sb_ref[...]
    for j in range(3):
        acc = _dot(u, w_ref[:, j * A_WIDTH:(j + 1) * A_WIDTH])
        for c in range(A_WIDTH // LANES):
            a = acc[:, c * LANES:(c + 1) * LANES]
            if j < 2:
                a = _rope128(a, ct, sa, sb)
            if j == 0:
                a = a * (HEAD_DIM ** -0.5)
            qkv_ref[:, j * A_WIDTH + c * LANES:j * A_WIDTH + (c + 1) * LANES] = a.astype(BF16)
    s_ref[...] = _dot(u, w_ref[:, 3 * A_WIDTH:])


def _proj0(x2, g, shift, scale, w_bf, tabs, seq, tm=512):
    t, d = x2.shape
    n = w_bf.shape[1]
    tpb = seq // tm
    ct, sa, sb = tabs
    tab_spec = pl.BlockSpec((tm, LANES), lambda i: (i % tpb, 0))
    mod_spec = pl.BlockSpec((1, 1, d), lambda i: (i // tpb, 0, 0))
    return pl.pallas_call(
        _proj0_body,
        grid=(t // tm,),
        in_specs=[pl.BlockSpec((tm, d), lambda i: (i, 0)),
                  pl.BlockSpec((1, d), lambda i: (0, 0)),
                  mod_spec, mod_spec,
                  pl.BlockSpec((d, n), lambda i: (0, 0)),
                  tab_spec, tab_spec, tab_spec],
        out_specs=[pl.BlockSpec((tm, 3 * A_WIDTH), lambda i: (i, 0)),
                   pl.BlockSpec((tm, B_WIDTH), lambda i: (i, 0))],
        out_shape=[jax.ShapeDtypeStruct((t, 3 * A_WIDTH), BF16),
                   jax.ShapeDtypeStruct((t, B_WIDTH), F32)],
        compiler_params=_cparams(("parallel",)),
        name="proj0",
    )(x2, g, shift, scale, w_bf, ct, sa, sb)


def _dil_body(has_state, final, q_ref, kp_ref, kc_ref, vp_ref, vc_ref, *rest):
    if has_state:
        m_in, l_in, o_in = rest[:3]
        rest = rest[3:]
    outs = rest
    i = pl.program_id(2)
    q = q_ref[0]
    kcat = jnp.concatenate([kp_ref[0], kc_ref[0]], axis=0)
    vcat = jnp.concatenate([vp_ref[0], vc_ref[0]], axis=0)
    row = lax.broadcasted_iota(I32, (Q_BLOCK, 2 * Q_BLOCK), 0)
    col = lax.broadcasted_iota(I32, (Q_BLOCK, 2 * Q_BLOCK), 1)
    rel = row + Q_BLOCK - col
    valid = (rel >= 0) & (rel <= Q_BLOCK) & ((col >= Q_BLOCK) | (i > 0))
    lane = lax.broadcasted_iota(I32, (1, LANES), 1)
    for p in range(A_WIDTH // LANES):
        sl = slice(p * LANES, (p + 1) * LANES)
        qp, kp, vp = q[:, sl], kcat[:, sl], vcat[:, sl]
        m_pair = l_pair = o_pair = None
        for hh in range(LANES // HEAD_DIM):
            hm = (lane // HEAD_DIM) == hh
            qh = jnp.where(hm, qp, jnp.zeros_like(qp))
            s = _dot_nt(qh, kp)
            s = jnp.where(valid, s, -jnp.inf)
            m = jnp.max(s, axis=1, keepdims=True)
            pr = jnp.exp(s - m)
            l = jnp.sum(pr, axis=1, keepdims=True)
            o = _dot(pr.astype(BF16), vp)
            if hh == 0:
                m_pair = jnp.broadcast_to(m, (Q_BLOCK, LANES))
                l_pair = jnp.broadcast_to(l, (Q_BLOCK, LANES))
                o_pair = o
            else:
                m_pair = jnp.where(hm, m, m_pair)
                l_pair = jnp.where(hm, l, l_pair)
                o_pair = jnp.where(hm, o, o_pair)
        if has_state:
            m0, l0, o0 = m_in[0, :, sl], l_in[0, :, sl], o_in[0, :, sl]
            mm = jnp.maximum(m0, m_pair)
            w0, w1 = jnp.exp(m0 - mm), jnp.exp(m_pair - mm)
            l_pair = w0 * l0 + w1 * l_pair
            o_pair = w0 * o0 + w1 * o_pair
            m_pair = mm
        if final:
            outs[0][0, :, sl] = (o_pair / l_pair).astype(outs[0].dtype)
        else:
            outs[0][0, :, sl] = m_pair
            outs[1][0, :, sl] = l_pair
            outs[2][0, :, sl] = o_pair


def _dilated_branch(qkv3, dil, state, final, bn, seq):
    sd = seq // dil
    nb = sd // Q_BLOCK
    qkv_v = qkv3.reshape(bn, sd, dil * 3 * A_WIDTH)
    blk = (1, Q_BLOCK, A_WIDTH)
    q_spec = pl.BlockSpec(blk, lambda b, r, i: (b, i, r * 3))
    kp_spec = pl.BlockSpec(blk, lambda b, r, i: (b, jnp.maximum(i - 1, 0), r * 3 + 1))
    kc_spec = pl.BlockSpec(blk, lambda b, r, i: (b, i, r * 3 + 1))
    vp_spec = pl.BlockSpec(blk, lambda b, r, i: (b, jnp.maximum(i - 1, 0), r * 3 + 2))
    vc_spec = pl.BlockSpec(blk, lambda b, r, i: (b, i, r * 3 + 2))
    st_spec = pl.BlockSpec(blk, lambda b, r, i: (b, i, r))
    in_specs = [q_spec, kp_spec, kc_spec, vp_spec, vc_spec]
    args = [qkv_v] * 5
    if state is not None:
        in_specs += [st_spec] * 3
        args += [s.reshape(bn, sd, dil * A_WIDTH) for s in state]
    if final:
        out_specs = [st_spec]
        out_shape = [jax.ShapeDtypeStruct((bn, sd, dil * A_WIDTH), BF16)]
    else:
        out_specs = [st_spec] * 3
        out_shape = [jax.ShapeDtypeStruct((bn, sd, dil * A_WIDTH), F32)] * 3
    outs = pl.pallas_call(
        functools.partial(_dil_body, state is not None, final),
        grid=(bn, dil, nb),
        in_specs=in_specs, out_specs=out_specs, out_shape=out_shape,
        compiler_params=_cparams(("parallel", "parallel", "arbitrary")),
        name=f"dilated_d{dil}",
    )(*args)
    return [o.reshape(bn, seq, A_WIDTH) for o in outs]


def _dilated_attention(qkv, bn, seq):
    qkv3 = qkv.reshape(bn, seq, 3 * A_WIDTH)
    state = None
    for gi, (_, dil) in enumerate(A_PATTERNS):
        final = gi == len(A_PATTERNS) - 1
        res = _dilated_branch(qkv3, dil, state, final, bn, seq)
        state = res
    return state[0].reshape(bn * seq, A_WIDTH)


def _s5_prepare(a_re, a_im, log_dt, b_re, b_im, c_re, c_im, d_skip, nchunk):
    dt = jnp.exp(log_dt.astype(F32))[:, None]
    lr, li = a_re.astype(F32), a_im.astype(F32)
    mag = jnp.exp(lr * dt)
    ar = mag * jnp.cos(li * dt)
    ai = mag * jnp.sin(li * dt)
    den = lr * lr + li * li
    nr = ar - 1.0
    fr = (nr * lr + ai * li) / den
    fi = (ai * lr - nr * li) / den
    br, bi = b_re.astype(F32), b_im.astype(F32)
    bbr = fr[..., None] * br - fi[..., None] * bi
    bbi = fr[..., None] * bi + fi[..., None] * br
    nblk = B_WIDTH // LANES
    gpb = B_GROUPS // nblk
    eye = jnp.eye(gpb, dtype=F32)

    def bdiag_in(m):
        m = jnp.swapaxes(m.reshape(nblk, gpb, B_STATE, B_GROUP), 2, 3)
        m = m[:, :, :, None, :] * eye[None, :, None, :, None]
        return m.reshape(nblk, gpb * B_GROUP, gpb * B_STATE)

    def bdiag_out(m):
        m = jnp.swapaxes(m.reshape(nblk, gpb, B_GROUP, B_STATE), 2, 3)
        m = m[:, :, :, None, :] * eye[None, :, None, :, None]
        return m.reshape(nblk, gpb * B_STATE, gpb * B_GROUP)

    pr, pi = ar, ai
    for _ in range(int(math.log2(S5_CHUNK))):
        pr, pi = pr * pr - pi * pi, 2.0 * pr * pi
    pows_r, pows_i = [], []
    for _ in range(max(1, int(math.ceil(math.log2(nchunk))))):
        pows_r.append(pr.reshape(1, -1))
        pows_i.append(pi.reshape(1, -1))
        pr, pi = pr * pr - pi * pi, 2.0 * pr * pi
    return dict(
        ar=ar.reshape(1, -1), ai=ai.reshape(1, -1),
        b_re=bdiag_in(bbr).astype(BF16), b_im=bdiag_in(bbi).astype(BF16),
        c_re=bdiag_out(c_re.astype(F32)).astype(BF16), c_im=bdiag_out(-c_im.astype(F32)).astype(BF16),
        d=d_skip.astype(F32).reshape(1, B_WIDTH),
        pows_r=jnp.concatenate(pows_r, axis=0), pows_i=jnp.concatenate(pows_i, axis=0))


def _s5_local_body(u_ref, bre_ref, bim_ref, ar_ref, ai_ref, sre_ref, sim_ref):
    tr = u_ref.shape[0]
    sw = bre_ref.shape[2]
    for blk in range(B_WIDTH // LANES):
        arb = ar_ref[:, blk * sw:(blk + 1) * sw]
        aib = ai_ref[:, blk * sw:(blk + 1) * sw]
        sr = jnp.zeros((tr, sw), F32)
        si = jnp.zeros((tr, sw), F32)
        for t in range(S5_CHUNK):
            ub = u_ref[:, t * B_WIDTH + blk * LANES:t * B_WIDTH + (blk + 1) * LANES].astype(BF16)
            sr, si = (arb * sr - aib * si + _dot(ub, bre_ref[blk]),
                      arb * si + aib * sr + _dot(ub, bim_ref[blk]))
        sre_ref[:, blk * sw:(blk + 1) * sw] = sr
        sim_ref[:, blk * sw:(blk + 1) * sw] = si


def _s5_scan_body(nsteps, sre_ref, sim_ref, pr_ref, pi_ref, xre_ref, xim_ref):
    nc = sre_ref.shape[0]
    sw = 512
    row = lax.broadcasted_iota(I32, (nc, 1), 0)
    for cb in range(sre_ref.shape[1] // sw):
        sl = slice(cb * sw, (cb + 1) * sw)
        xr, xi = sre_ref[:, sl], sim_ref[:, sl]
        for s in range(nsteps):
            sh = 1 << s
            ok = row >= sh
            rr = jnp.where(ok, pltpu.roll(xr, sh, 0), 0.0)
            ri = jnp.where(ok, pltpu.roll(xi, sh, 0), 0.0)
            pr, pi = pr_ref[s:s + 1, sl], pi_ref[s:s + 1, sl]
            xr, xi = xr + pr * rr - pi * ri, xi + pr * ri + pi * rr
        ok = row >= 1
        xre_ref[:, sl] = jnp.where(ok, pltpu.roll(xr, 1, 0), 0.0)
        xim_ref[:, sl] = jnp.where(ok, pltpu.roll(xi, 1, 0), 0.0)


def _s5_out_body(u_ref, xre_ref, xim_ref, bre_ref, bim_ref, cre_ref, cim_ref, ar_ref, ai_ref,
                 d_ref, wglu_ref, bglu_ref, out_ref, y_scr):
    sw = bre_ref.shape[2]
    for blk in range(B_WIDTH // LANES):
        arb = ar_ref[:, blk * sw:(blk + 1) * sw]
        aib = ai_ref[:, blk * sw:(blk + 1) * sw]
        xr = xre_ref[:, blk * sw:(blk + 1) * sw]
        xi = xim_ref[:, blk * sw:(blk + 1) * sw]
        db = d_ref[:, blk * LANES:(blk + 1) * LANES]
        for t in range(S5_CHUNK):
            cs = slice(t * B_WIDTH + blk * LANES, t * B_WIDTH + (blk + 1) * LANES)
            uf = u_ref[:, cs]
            ub = uf.astype(BF16)
            xr, xi = (arb * xr - aib * xi + _dot(ub, bre_ref[blk]),
                      arb * xi + aib * xr + _dot(ub, bim_ref[blk]))
            y = _dot(xr.astype(BF16), cre_ref[blk]) + _dot(xi.astype(BF16), cim_ref[blk]) + db * uf
            y_scr[:, cs] = jax.nn.gelu(y, approximate=True)
    for t in range(S5_CHUNK):
        cs = slice(t * B_WIDTH, (t + 1) * B_WIDTH)
        y = y_scr[:, cs]
        z = _dot(y.astype(BF16), wglu_ref[...]) + bglu_ref[...]
        out_ref[:, cs] = (y * jax.nn.sigmoid(z)).astype(out_ref.dtype)


def _s5_mixer(s_in, prm, w_glu_bf, b_glu, bn, seq, tr=128):
    t = bn * seq
    nrow = t // S5_CHUNK
    ncb = seq // S5_CHUNK
    width = S5_CHUNK * B_WIDTH
    sdim = B_GROUPS * B_STATE
    uc = s_in.reshape(nrow, width)
    nblk = B_WIDTH // LANES
    const3 = lambda i: (0, 0, 0)
    const2 = lambda i: (0, 0)
    b_spec = pl.BlockSpec((nblk, LANES, sdim // nblk), const3)
    c_spec = pl.BlockSpec((nblk, sdim // nblk, LANES), const3)
    a_spec = pl.BlockSpec((1, sdim), const2)
    s_re, s_im = pl.pallas_call(
        _s5_local_body,
        grid=(nrow // tr,),
        in_specs=[pl.BlockSpec((tr, width), lambda i: (i, 0)), b_spec, b_spec, a_spec, a_spec],
        out_specs=[pl.BlockSpec((tr, sdim), lambda i: (i, 0))] * 2,
        out_shape=[jax.ShapeDtypeStruct((nrow, sdim), F32)] * 2,
        compiler_params=_cparams(("parallel",)),
        name="s5_local",
    )(uc, prm['b_re'], prm['b_im'], prm['ar'], prm['ai'])
    nsteps = prm['pows_r'].shape[0]
    x_re, x_im = pl.pallas_call(
        functools.partial(_s5_scan_body, nsteps),
        grid=(bn,),
        in_specs=[pl.BlockSpec((ncb, sdim), lambda b: (b, 0))] * 2
                 + [pl.BlockSpec((nsteps, sdim), lambda b: (0, 0))] * 2,
        out_specs=[pl.BlockSpec((ncb, sdim), lambda b: (b, 0))] * 2,
        out_shape=[jax.ShapeDtypeStruct((nrow, sdim), F32)] * 2,
        compiler_params=_cparams(("parallel",)),
        name="s5_scan",
    )(s_re, s_im, prm['pows_r'], prm['pows_i'])
    y = pl.pallas_call(
        _s5_out_body,
        grid=(nrow // tr,),
        in_specs=[pl.BlockSpec((tr, width), lambda i: (i, 0)),
                  pl.BlockSpec((tr, sdim), lambda i: (i, 0)),
                  pl.BlockSpec((tr, sdim), lambda i: (i, 0)),
                  b_spec, b_spec, c_spec, c_spec, a_spec, a_spec,
                  pl.BlockSpec((1, B_WIDTH), const2),
                  pl.BlockSpec((B_WIDTH, B_WIDTH), const2),
                  pl.BlockSpec((1, B_WIDTH), const2)],
        out_specs=pl.BlockSpec((tr, width), lambda i: (i, 0)),
        out_shape=jax.ShapeDtypeStruct((nrow, width), BF16),
        scratch_shapes=[pltpu.VMEM((tr, width), F32)],
        compiler_params=_cparams(("parallel",)),
        name="s5_out",
    )(uc, x_re, x_im, prm['b_re'], prm['b_im'], prm['c_re'], prm['c_im'], prm['ar'], prm['ai'],
      prm['d'], w_glu_bf, b_glu.reshape(1, B_WIDTH).astype(F32))
    return y.reshape(t, B_WIDTH)


def _outproj_body(nparts, *refs):
    parts = refs[:nparts]
    w_ref, h_ref, gt_ref, o_ref = refs[nparts:]
    acc = None
    off = 0
    for p in parts:
        k = p.shape[1]
        d = _dot(p[...].astype(BF16), w_ref[off:off + k, :])
        acc = d if acc is None else acc + d
        off += k
    o_ref[...] = h_ref[...] + gt_ref[0] * acc


def _outproj(parts, w_bf, h2, gate, seq, tm=512):
    t, d = h2.shape
    tpb = seq // tm
    in_specs = [pl.BlockSpec((tm, p.shape[1]), lambda i: (i, 0)) for p in parts]
    in_specs += [pl.BlockSpec(w_bf.shape, lambda i: (0, 0)),
                 pl.BlockSpec((tm, d), lambda i: (i, 0)),
                 pl.BlockSpec((1, 1, d), lambda i: (i // tpb, 0, 0))]
    return pl.pallas_call(
        functools.partial(_outproj_body, len(parts)),
        grid=(t // tm,),
        in_specs=in_specs,
        out_specs=pl.BlockSpec((tm, d), lambda i: (i, 0)),
        out_shape=jax.ShapeDtypeStruct((t, d), F32),
        compiler_params=_cparams(("parallel",)),
        name="outproj",
    )(*parts, w_bf, h2, gate)


def _ffn_body(h_ref, g_ref, sh_ref, sc_ref, gt_ref, wg_ref, wu_ref, wd_ref, o_ref, u_scr, acc_scr):
    f = pl.program_id(1)

    @pl.when(f == 0)
    def _():
        u_scr[...] = _norm_mod(h_ref[...], g_ref[...], sh_ref[0], sc_ref[0]).astype(BF16)
        acc_scr[...] = jnp.zeros_like(acc_scr)

    u = u_scr[...]
    gp = _dot(u, wg_ref[...])
    up = _dot(u, wu_ref[...])
    a = (gp * jax.nn.sigmoid(gp) * up).astype(BF16)
    acc_scr[...] += _dot(a, wd_ref[...])

    @pl.when(f == pl.num_programs(1) - 1)
    def _():
        o_ref[...] = h_ref[...] + gt_ref[0] * acc_scr[...]


def _ffn_dense(h2, g, shift, scale, gate, wg, wu, wd, seq, tm=512, tf=256):
    t, d = h2.shape
    ff = wg.shape[1]
    tpb = seq // tm
    mod_spec = pl.BlockSpec((1, 1, d), lambda i, f: (i // tpb, 0, 0))
    return pl.pallas_call(
        _ffn_body,
        grid=(t // tm, ff // tf),
        in_specs=[pl.BlockSpec((tm, d), lambda i, f: (i, 0)),
                  pl.BlockSpec((1, d), lambda i, f: (0, 0)),
                  mod_spec, mod_spec, mod_spec,
                  pl.BlockSpec((d, tf), lambda i, f: (0, f)),
                  pl.BlockSpec((d, tf), lambda i, f: (0, f)),
                  pl.BlockSpec((tf, d), lambda i, f: (f, 0))],
        out_specs=pl.BlockSpec((tm, d), lambda i, f: (i, 0)),
        out_shape=jax.ShapeDtypeStruct((t, d), F32),
        scratch_shapes=[pltpu.VMEM((tm, d), BF16), pltpu.VMEM((tm, d), F32)],
        compiler_params=_cparams(("parallel", "arbitrary")),
        name="ffn_dense",
    )(h2, g, shift, scale, gate, wg, wu, wd)


P1_QR = 0
P1_QN = 256
P1_CKV = 1280
P1_KR = 1536
P1_QI = 1792
P1_KI = 2304
P1_WI = 2432
P1_COLS = 2560


def _proj1_layout(w_in):
    d = w_in.shape[0]
    c0 = 0
    w_qr = w_in[:, c0:c0 + C_HEADS * ROPE_DIM]; c0 += C_HEADS * ROPE_DIM
    w_qn = w_in[:, c0:c0 + C_HEADS * C_NOPE]; c0 += C_HEADS * C_NOPE
    w_ckv = w_in[:, c0:c0 + C_LATENT]; c0 += C_LATENT
    w_kr = w_in[:, c0:c0 + ROPE_DIM]; c0 += ROPE_DIM
    w_qi = w_in[:, c0:c0 + IDX_HEADS * IDX_DIM]; c0 += IDX_HEADS * IDX_DIM
    w_ki = w_in[:, c0:c0 + IDX_DIM]; c0 += IDX_DIM
    w_wi = w_in[:, c0:c0 + IDX_HEADS]
    w_qn = jnp.pad(w_qn.reshape(d, C_HEADS, C_NOPE), ((0, 0), (0, 0), (0, HEAD_DIM - C_NOPE)))
    w_qn = w_qn.reshape(d, C_HEADS * HEAD_DIM)
    w_kr = jnp.tile(w_kr, (1, C_HEADS))
    w_ki = jnp.tile(w_ki, (1, 2))
    w_wi = jnp.pad(w_wi, ((0, 0), (0, LANES - IDX_HEADS)))
    return jnp.concatenate([w_qr, w_qn, w_ckv, w_kr, w_qi, w_ki, w_wi], axis=1)


def _proj1_body(x_ref, g_ref, sh_ref, sc_ref, w_ref, gkv_ref,
                ct16_ref, sa16_ref, sb16_ref, ct64_ref, sa64_ref, sb64_ref,
                qa_ref, kp_ref, qi_ref, ki_ref, wi_ref):
    u = _norm_mod(x_ref[...], g_ref[...], sh_ref[0], sc_ref[0]).astype(BF16)
    t16 = (ct16_ref[...], sa16_ref[...], sb16_ref[...])
    t64 = (ct64_ref[...], sa64_ref[...], sb64_ref[...])
    qscale = HEAD_DIM ** -0.5

    def cols(lo, hi):
        return _dot(u, w_ref[:, lo:hi])

    a = cols(P1_QR, P1_QN)
    for c in range(2):
        blk = _rope128(a[:, c * LANES:(c + 1) * LANES], *t16) * qscale
        qa_ref[:, c * LANES:(c + 1) * LANES] = blk.astype(BF16)
    qa_ref[:, P1_QN:P1_CKV] = (cols(P1_QN, P1_CKV) * qscale).astype(BF16)
    ckv = cols(P1_CKV, P1_KR)
    ms = jnp.mean(ckv * ckv, axis=-1, keepdims=True)
    kp_ref[:, 0:C_LATENT] = (ckv * lax.rsqrt(ms + NORM_EPS) * gkv_ref[...]).astype(BF16)
    a = cols(P1_KR, P1_QI)
    for c in range(2):
        blk = _rope128(a[:, c * LANES:(c + 1) * LANES], *t16)
        kp_ref[:, C_LATENT + c * LANES:C_LATENT + (c + 1) * LANES] = blk.astype(BF16)
    a = cols(P1_QI, P1_KI)
    for c in range(4):
        qi_ref[:, c * LANES:(c + 1) * LANES] = _rope128(a[:, c * LANES:(c + 1) * LANES], *t64).astype(BF16)
    ki_ref[...] = _rope128(cols(P1_KI, P1_WI), *t64).astype(BF16)
    wi_ref[...] = cols(P1_WI, P1_COLS) * (IDX_HEADS ** -0.5) * (IDX_DIM ** -0.5)


def _proj1(x2, g, shift, scale, w_bf, gkv, t16, t64, seq, tm=512):
    t, d = x2.shape
    tpb = seq // tm
    tab_spec = pl.BlockSpec((tm, LANES), lambda i: (i % tpb, 0))
    mod_spec = pl.BlockSpec((1, 1, d), lambda i: (i // tpb, 0, 0))
    widths = (P1_CKV, 2 * C_LATENT, IDX_HEADS * IDX_DIM, LANES, LANES)
    dtypes = (BF16, BF16, BF16, BF16, F32)
    return pl.pallas_call(
        _proj1_body,
        grid=(t // tm,),
        in_specs=[pl.BlockSpec((tm, d), lambda i: (i, 0)),
                  pl.BlockSpec((1, d), lambda i: (0, 0)),
                  mod_spec, mod_spec,
                  pl.BlockSpec((d, P1_COLS), lambda i: (0, 0)),
                  pl.BlockSpec((1, C_LATENT), lambda i: (0, 0))] + [tab_spec] * 6,
        out_specs=[pl.BlockSpec((tm, w), lambda i: (i, 0)) for w in widths],
        out_shape=[jax.ShapeDtypeStruct((t, w), dt) for w, dt in zip(widths, dtypes)],
        compiler_params=_cparams(("parallel",)),
        name="proj1",
    )(x2, g, shift, scale, w_bf, gkv, *t16, *t64)


IDX_KT = 512


def _idx_body(topk, qi_ref, ki_ref, wi_ref, mask_ref, sc_scr):
    i = pl.program_id(1)
    seq = ki_ref.shape[1]
    nkt = seq // IDX_KT
    nlive = (i * Q_BLOCK) // IDX_KT + 1
    q = qi_ref[0]
    wt = wi_ref[0].T
    lane = lax.broadcasted_iota(I32, (1, LANES), 1)
    qpos = i * Q_BLOCK + lane
    krow = lax.broadcasted_iota(I32, (IDX_KT, 1), 0)
    qpair = [jnp.concatenate([q[:, (2 * g) * LANES:(2 * g + 1) * LANES],
                              q[:, (2 * g + 1) * LANES:(2 * g + 2) * LANES]], axis=0) for g in range(2)]

    def score_tile(kt, c):
        kk = ki_ref[0, pl.ds(pl.multiple_of(kt * IDX_KT, IDX_KT), IDX_KT), :]
        zero = jnp.zeros_like(kk)
        kpart = [jnp.where(lane < IDX_DIM, kk, zero), jnp.where(lane >= IDX_DIM, kk, zero)]
        acc = jnp.zeros((IDX_KT, Q_BLOCK), F32)
        for g in range(2):
            for part in range(2):
                res = jnp.maximum(_dot_nt(kpart[part], qpair[g]), 0.0)
                for j in range(2):
                    h = 2 * (2 * g + j) + part
                    acc = acc + wt[h:h + 1, :] * res[:, j * Q_BLOCK:(j + 1) * Q_BLOCK]
        acc = acc + 0.0
        sc_scr[kt] = jnp.where(kt * IDX_KT + krow <= qpos, acc, -jnp.inf)
        return c

    lax.fori_loop(0, nlive, score_tile, 0)
    kq = jnp.minimum(qpos + 1, topk).astype(F32)

    def count(pred_fn):
        def body(kt, acc):
            ones = jnp.where(pred_fn(sc_scr[kt], kt), 1.0, 0.0)
            part = jnp.sum(ones.reshape(8, IDX_KT // 64, 8, Q_BLOCK), axis=1)
            return acc + jnp.sum(part, axis=0)
        part = lax.fori_loop(0, nlive, body, jnp.zeros((8, Q_BLOCK), F32))
        return jnp.sum(part, axis=0, keepdims=True)

    def key_to_f32(key):
        bits = jnp.where(key < 0, key ^ jnp.int32(0x7FFFFFFF), key)
        return lax.bitcast_convert_type(bits, F32)

    def bit_step(it, ans):
        cand = ans | lax.shift_left(jnp.int32(1), 31 - it)
        thr = key_to_f32(cand ^ jnp.int32(INT_MIN))
        cnt = count(lambda sc, kt: sc >= thr)
        return jnp.where(cnt >= kq, cand, ans)

    ans = lax.fori_loop(0, 32, bit_step, jnp.zeros((1, Q_BLOCK), I32))
    thr = key_to_f32(ans ^ jnp.int32(INT_MIN))
    n_ge = count(lambda sc, kt: sc >= thr)
    nbits = int(math.log2(seq))

    def tie_cut(_):
        need = kq - count(lambda sc, kt: sc > thr)

        def tie_step(it, ans2):
            cand = ans2 | lax.shift_left(jnp.int32(1), nbits - 1 - it)
            cnt = count(lambda sc, kt: (sc == thr) & (kt * IDX_KT + krow < cand))
            return jnp.where(cnt < need, cand, ans2)

        return lax.fori_loop(0, nbits, tie_step, jnp.zeros((1, Q_BLOCK), I32))

    excess = jnp.max(n_ge - kq) > 0.0
    jcut = lax.cond(excess, tie_cut, lambda _: jnp.full((1, Q_BLOCK), seq, I32), 0)

    def emit(kt, c):
        sc = sc_scr[kt]
        sel = (sc > thr) | ((sc == thr) & (kt * IDX_KT + krow <= jcut))
        mask_ref[0, 0, kt] = jnp.where(sel, 1.0, 0.0).T.astype(BF16)
        return c

    lax.fori_loop(0, nlive, emit, 0)

    def emit_dead(kt, c):
        mask_ref[0, 0, kt] = jnp.zeros((Q_BLOCK, IDX_KT), BF16)
        return c

    lax.fori_loop(nlive, nkt, emit_dead, 0)


def _dsa_select(qi, ki, wi, bn, seq):
    topk = min(TOPK_MAX, seq // 4)
    nqb = seq // Q_BLOCK
    nkt = seq // IDX_KT
    return pl.pallas_call(
        functools.partial(_idx_body, topk),
        grid=(bn, nqb),
        in_specs=[pl.BlockSpec((1, Q_BLOCK, IDX_HEADS * IDX_DIM), lambda b, i: (b, i, 0)),
                  pl.BlockSpec((1, seq, LANES), lambda b, i: (b, 0, 0)),
                  pl.BlockSpec((1, Q_BLOCK, LANES), lambda b, i: (b, i, 0))],
        out_specs=pl.BlockSpec((1, 1, nkt, Q_BLOCK, IDX_KT), lambda b, i: (b, i, 0, 0, 0)),
        out_shape=jax.ShapeDtypeStruct((bn, nqb, nkt, Q_BLOCK, IDX_KT), BF16),
        scratch_shapes=[pltpu.VMEM((nkt, IDX_KT, Q_BLOCK), F32)],
        compiler_params=_cparams(("parallel", "arbitrary")),
        name="dsa_select",
    )(qi.reshape(bn, seq, -1), ki.reshape(bn, seq, -1), wi.reshape(bn, seq, -1))


ATT_RB = 256


def _dsa_attn_body(qa_ref, kp_ref, mask_ref, wuk_ref, wuv_ref, o_ref, qp_scr, m_scr, l_scr, acc_scr,
                   s_scr):
    i = pl.program_id(1)
    rows = C_HEADS * Q_BLOCK
    qr = qa_ref[0, :, 0:P1_QN]
    lane = lax.broadcasted_iota(I32, (1, C_HEADS * ROPE_DIM), 1)
    for h in range(C_HEADS):
        p = h // 2
        qn = qa_ref[0, :, P1_QN + p * LANES:P1_QN + (p + 1) * LANES]
        qlat = _dot(qn, wuk_ref[h])
        qp_scr[h * Q_BLOCK:(h + 1) * Q_BLOCK, 0:C_LATENT] = qlat.astype(BF16)
        qp_scr[h * Q_BLOCK:(h + 1) * Q_BLOCK, C_LATENT:2 * C_LATENT] = jnp.where(
            (lane // ROPE_DIM) == h, qr, jnp.zeros_like(qr))
    m_scr[...] = jnp.full(m_scr.shape, NEG_BIG, F32)
    l_scr[...] = jnp.zeros_like(l_scr)
    acc_scr[...] = jnp.zeros_like(acc_scr)
    nblk = rows // ATT_RB

    def ktile(kt):
        return kp_ref[0, pl.ds(pl.multiple_of(kt * IDX_KT, IDX_KT), IDX_KT), :]

    def softmax_pv(b, s, kt, vv):
        rs = slice(b * ATT_RB, (b + 1) * ATT_RB)
        bias = (mask_ref[0, 0, kt].astype(F32) - 1.0) * (-NEG_BIG)
        s = s + jnp.concatenate([bias] * (ATT_RB // Q_BLOCK), axis=0)
        m_old = m_scr[rs, :]
        m_new = jnp.maximum(m_old, jnp.max(s, axis=1, keepdims=True))
        alpha = jnp.exp(m_old - m_new)
        pr = jnp.exp(s - m_new)
        l_scr[rs, :] = alpha * l_scr[rs, :] + jnp.sum(pr, axis=1, keepdims=True)
        m_scr[rs, :] = m_new
        acc_scr[rs, :] = alpha * acc_scr[rs, :] + _dot(pr.astype(BF16), vv)

    kk0 = ktile(0)
    for b in range(nblk):
        s_scr[b * ATT_RB:(b + 1) * ATT_RB, :] = _dot_nt(qp_scr[b * ATT_RB:(b + 1) * ATT_RB, :], kk0)

    def step(kt, carry):
        kk = ktile(kt)
        kk_next = ktile(kt + 1)
        vv = kk[:, 0:C_LATENT]
        for b in range(nblk):
            rs = slice(b * ATT_RB, (b + 1) * ATT_RB)
            s = s_scr[rs, :]
            s_scr[rs, :] = _dot_nt(qp_scr[rs, :], kk_next)
            softmax_pv(b, s, kt, vv)
        return carry

    last = (i * Q_BLOCK) // IDX_KT
    lax.fori_loop(0, last, step, 0)
    vv = ktile(last)[:, 0:C_LATENT]
    for b in range(nblk):
        softmax_pv(b, s_scr[b * ATT_RB:(b + 1) * ATT_RB, :], last, vv)
    olat = (acc_scr[...] / l_scr[...]).astype(BF16)
    for p in range(C_HEADS // 2):
        o = (_dot(olat[(2 * p) * Q_BLOCK:(2 * p + 1) * Q_BLOCK], wuv_ref[2 * p])
             + _dot(olat[(2 * p + 1) * Q_BLOCK:(2 * p + 2) * Q_BLOCK], wuv_ref[2 * p + 1]))
        o_ref[0, :, p * LANES:(p + 1) * LANES] = o.astype(o_ref.dtype)


def _dsa_attention(qa, kp, mask, wuk_x, wuv_x, bn, seq):
    nqb = seq // Q_BLOCK
    nkt = seq // IDX_KT
    rows = C_HEADS * Q_BLOCK
    return pl.pallas_call(
        _dsa_attn_body,
        grid=(bn, nqb),
        in_specs=[pl.BlockSpec((1, Q_BLOCK, P1_CKV), lambda b, i: (b, i, 0)),
                  pl.BlockSpec((1, seq, 2 * C_LATENT), lambda b, i: (b, 0, 0)),
                  pl.BlockSpec((1, 1, nkt, Q_BLOCK, IDX_KT), lambda b, i: (b, i, 0, 0, 0)),
                  pl.BlockSpec((C_HEADS, LANES, C_LATENT), lambda b, i: (0, 0, 0)),
                  pl.BlockSpec((C_HEADS, C_LATENT, LANES), lambda b, i: (0, 0, 0))],
        out_specs=pl.BlockSpec((1, Q_BLOCK, C_HEADS * HEAD_DIM), lambda b, i: (b, i, 0)),
        out_shape=jax.ShapeDtypeStruct((bn, seq, C_HEADS * HEAD_DIM), BF16),
        scratch_shapes=[pltpu.VMEM((rows, 2 * C_LATENT), BF16),
                        pltpu.VMEM((rows, 1), F32), pltpu.VMEM((rows, 1), F32),
                        pltpu.VMEM((rows, C_LATENT), F32),
                        pltpu.VMEM((rows, IDX_KT), F32)],
        compiler_params=_cparams(("parallel", "arbitrary")),
        name="dsa_attn",
    )(qa.reshape(bn, seq, -1), kp.reshape(bn, seq, -1), mask, wuk_x, wuv_x)


def _mla_weights(w_uk, w_uv):
    wuk = jnp.transpose(w_uk, (1, 2, 0))
    wuk_x = jnp.zeros((C_HEADS, LANES, C_LATENT), F32)
    wuv = jnp.transpose(w_uv, (1, 0, 2))
    wuv_x = jnp.zeros((C_HEADS, C_LATENT, LANES), F32)
    for h in range(C_HEADS):
        o = (h % 2) * HEAD_DIM
        wuk_x = wuk_x.at[h, o:o + C_NOPE, :].set(wuk[h])
        wuv_x = wuv_x.at[h, :, o:o + HEAD_DIM].set(wuv[h])
    return wuk_x.astype(BF16), wuv_x.astype(BF16)


def _router_body(h_ref, g_ref, sh_ref, sc_ref, wr_ref, br_ref, route_ref, up_ref, cnt_ref, carry_scr):
    i = pl.program_id(0)
    tm = h_ref.shape[0]

    @pl.when(i == 0)
    def _():
        carry_scr[...] = jnp.zeros_like(carry_scr)

    u = _norm_mod(h_ref[...], g_ref[...], sh_ref[0], sc_ref[0])
    logits = jnp.dot(u, wr_ref[...], preferred_element_type=F32, precision=HIGHEST) + br_ref[...]
    lane = lax.broadcasted_iota(I32, (tm, LANES), 1).astype(F32)
    m1 = jnp.max(logits, axis=1, keepdims=True)
    e1 = jnp.min(jnp.where(logits == m1, lane, float(LANES)), axis=1, keepdims=True)
    rest = jnp.where(lane == e1, NEG_BIG * 2, logits)
    m2 = jnp.max(rest, axis=1, keepdims=True)
    e2 = jnp.min(jnp.where(rest == m2, lane, float(LANES)), axis=1, keepdims=True)
    ex = jnp.exp(m2 - m1)
    g1 = 1.0 / (1.0 + ex)
    g2 = ex / (1.0 + ex)
    onehot = ((lane == e1) | (lane == e2))
    oh_bf = jnp.where(onehot, 1.0, 0.0).astype(BF16)
    r = lax.broadcasted_iota(I32, (tm, tm), 0)
    c = lax.broadcasted_iota(I32, (tm, tm), 1)
    tri = jnp.where(c < r, 1.0, 0.0).astype(BF16)
    prefix = _dot(tri, oh_bf) + carry_scr[...]
    rank1 = jnp.sum(jnp.where(lane == e1, prefix, 0.0), axis=1, keepdims=True)
    rank2 = jnp.sum(jnp.where(lane == e2, prefix, 0.0), axis=1, keepdims=True)
    carry_scr[...] = carry_scr[...] + jnp.sum(jnp.where(onehot, 1.0, 0.0), axis=0, keepdims=True)
    cnt_ref[...] = carry_scr[...]
    vals = [e1, e2, g1, g2, rank1, rank2]
    route = jnp.zeros((tm, LANES), F32)
    for k, v in enumerate(vals):
        route = jnp.where(lane == float(k), v, route)
    route_ref[...] = route
    up_ref[...] = u


def _router(h2, g, shift, scale, w_router, b_router, seq, tm=512):
    t, d = h2.shape
    tpb = seq // tm
    wr = jnp.pad(w_router.astype(F32), ((0, 0), (0, LANES - N_EXPERTS)))
    br = jnp.pad(b_router.astype(F32), (0, LANES - N_EXPERTS), constant_values=NEG_BIG).reshape(1, LANES)
    mod_spec = pl.BlockSpec((1, 1, d), lambda i: (i // tpb, 0, 0))
    return pl.pallas_call(
        _router_body,
        grid=(t // tm,),
        in_specs=[pl.BlockSpec((tm, d), lambda i: (i, 0)),
                  pl.BlockSpec((1, d), lambda i: (0, 0)),
                  mod_spec, mod_spec,
                  pl.BlockSpec((d, LANES), lambda i: (0, 0)),
                  pl.BlockSpec((1, LANES), lambda i: (0, 0))],
        out_specs=[pl.BlockSpec((tm, LANES), lambda i: (i, 0)),
                   pl.BlockSpec((tm, d), lambda i: (i, 0)),
                   pl.BlockSpec((1, LANES), lambda i: (0, 0))],
        out_shape=[jax.ShapeDtypeStruct((t, LANES), F32),
                   jax.ShapeDtypeStruct((t, d), F32),
                   jax.ShapeDtypeStruct((1, LANES), F32)],
        scratch_shapes=[pltpu.VMEM((1, LANES), F32)],
        compiler_params=_cparams(("arbitrary",)),
        name="moe_router",
    )(h2, g, shift, scale, wr, br)


def _dispatch_body(p1_ref, p2_ref, up_ref, xs_in_ref, xs_ref, sem):
    del xs_in_ref
    tm = up_ref.shape[0]

    def copy(r, dst):
        return pltpu.make_async_copy(up_ref.at[pl.ds(r, 1)], xs_ref.at[pl.ds(dst, 1)], sem)

    def start(r, c):
        copy(r, p1_ref[r]).start()
        copy(r, p2_ref[r]).start()
        return c

    lax.fori_loop(0, tm, start, 0)

    def wait(r, c):
        copy(r, p1_ref[r]).wait()
        copy(r, p2_ref[r]).wait()
        return c

    lax.fori_loop(0, tm, wait, 0)


def _dispatch(pos1, pos2, up, nrows, tm=256):
    t, w = up.shape
    xs0 = jnp.zeros((nrows, w), F32)
    smem_spec = pl.BlockSpec((tm,), lambda i: (i,), memory_space=pltpu.SMEM)
    return pl.pallas_call(
        _dispatch_body,
        grid=(t // tm,),
        in_specs=[smem_spec, smem_spec,
                  pl.BlockSpec((tm, w), lambda i: (i, 0)),
                  pl.BlockSpec(memory_space=pl.ANY)],
        out_specs=pl.BlockSpec(memory_space=pl.ANY),
        out_shape=jax.ShapeDtypeStruct((nrows, w), F32),
        scratch_shapes=[pltpu.SemaphoreType.DMA(())],
        input_output_aliases={3: 0},
        compiler_params=_cparams(("arbitrary",)),
        name="moe_dispatch",
    )(pos1, pos2, up, xs0)


def _expert_body(te_ref, nv_ref, xs_ref, wg_ref, wu_ref, wd_ref, y_ref, x_scr, acc_scr):
    j = pl.program_id(0)
    f = pl.program_id(1)
    live = j < nv_ref[0]

    @pl.when(f == 0)
    def _():
        x_scr[...] = xs_ref[...].astype(BF16)
        acc_scr[...] = jnp.zeros_like(acc_scr)

    @pl.when(live)
    def _():
        x = x_scr[...]
        gp = _dot(x, wg_ref[0])
        up = _dot(x, wu_ref[0])
        a = (gp * jax.nn.sigmoid(gp) * up).astype(BF16)
        acc_scr[...] += _dot(a, wd_ref[0])

    @pl.when(f == pl.num_programs(1) - 1)
    def _():
        y_ref[...] = acc_scr[...]


def _experts(tile_expert, n_live, xs, wg, wu, wd, tm, tf=512):
    nrows, d = xs.shape
    ff = wg.shape[2]
    nf = ff // tf
    nt = nrows // tm

    def f_eff(j, f, nv):
        return jnp.where(j < nv[0], f, nf - 1)

    grid_spec = pltpu.PrefetchScalarGridSpec(
        num_scalar_prefetch=2,
        grid=(nt, nf),
        in_specs=[pl.BlockSpec((tm, d), lambda j, f, te, nv: (j, 0)),
                  pl.BlockSpec((1, d, tf), lambda j, f, te, nv: (te[j], 0, f_eff(j, f, nv))),
                  pl.BlockSpec((1, d, tf), lambda j, f, te, nv: (te[j], 0, f_eff(j, f, nv))),
                  pl.BlockSpec((1, tf, d), lambda j, f, te, nv: (te[j], f_eff(j, f, nv), 0))],
        out_specs=pl.BlockSpec((tm, d), lambda j, f, te, nv: (j, 0)),
        scratch_shapes=[pltpu.VMEM((tm, d), BF16), pltpu.VMEM((tm, d), F32)])
    return pl.pallas_call(
        _expert_body,
        grid_spec=grid_spec,
        out_shape=jax.ShapeDtypeStruct((nrows, d), F32),
        compiler_params=_cparams(("arbitrary", "arbitrary")),
        name="moe_experts",
    )(tile_expert, n_live, xs, wg, wu, wd)


def _combine_body(p1_ref, p2_ref, route_ref, h_ref, gt_ref, gf_ref, y_ref, o_ref, y1_scr, y2_scr, sem):
    tm = h_ref.shape[0]

    def copy(src, dst_scr, r):
        return pltpu.make_async_copy(y_ref.at[pl.ds(src, 1)], dst_scr.at[pl.ds(r, 1)], sem)

    def start(r, c):
        copy(p1_ref[r], y1_scr, r).start()
        copy(p2_ref[r], y2_scr, r).start()
        return c

    lax.fori_loop(0, tm, start, 0)

    def wait(r, c):
        copy(p1_ref[r], y1_scr, r).wait()
        copy(p2_ref[r], y2_scr, r).wait()
        return c

    lax.fori_loop(0, tm, wait, 0)
    route = route_ref[...]
    g1, g2 = route[:, 2:3], route[:, 3:4]
    y = g1 * y1_scr[...] + g2 * y2_scr[...]
    hn = h_ref[...] + gt_ref[0] * y
    ms = jnp.mean(hn * hn, axis=-1, keepdims=True)
    o_ref[...] = hn * lax.rsqrt(ms + NORM_EPS) * gf_ref[...]


def _combine(pos1, pos2, route, h2, gate, g_final, y_sorted, seq, tm=256):
    t, d = h2.shape
    tpb = seq // tm
    smem_spec = pl.BlockSpec((tm,), lambda i: (i,), memory_space=pltpu.SMEM)
    return pl.pallas_call(
        _combine_body,
        grid=(t // tm,),
        in_specs=[smem_spec, smem_spec,
                  pl.BlockSpec((tm, LANES), lambda i: (i, 0)),
                  pl.BlockSpec((tm, d), lambda i: (i, 0)),
                  pl.BlockSpec((1, 1, d), lambda i: (i // tpb, 0, 0)),
                  pl.BlockSpec((1, d), lambda i: (0, 0)),
                  pl.BlockSpec(memory_space=pl.ANY)],
        out_specs=pl.BlockSpec((tm, d), lambda i: (i, 0)),
        out_shape=jax.ShapeDtypeStruct((t, d), F32),
        scratch_shapes=[pltpu.VMEM((tm, d), F32), pltpu.VMEM((tm, d), F32),
                        pltpu.SemaphoreType.DMA(())],
        compiler_params=_cparams(("arbitrary",)),
        name="moe_combine",
    )(pos1, pos2, route, h2, gate, g_final, y_sorted)


MOE_TILE = 512


def _moe_layout(route, counts):
    e1 = route[:, 0].astype(I32)
    e2 = route[:, 1].astype(I32)
    r1 = route[:, 4].astype(I32)
    r2 = route[:, 5].astype(I32)
    cnt = counts[0, :N_EXPERTS].astype(I32)
    tiles = (cnt + MOE_TILE - 1) // MOE_TILE
    tile_end = jnp.cumsum(tiles)
    start = (tile_end - tiles) * MOE_TILE
    pos1 = start[e1] + r1
    pos2 = start[e2] + r2
    nt = route.shape[0] * 2 // MOE_TILE + N_EXPERTS
    n_live = tile_end[-1]
    tid = jnp.minimum(jnp.arange(nt, dtype=I32), n_live - 1)
    tile_expert = jnp.sum((tid[:, None] >= tile_end[None, :]).astype(I32), axis=1)
    return pos1, pos2, tile_expert.astype(I32), n_live.reshape(1).astype(I32), nt * MOE_TILE


def kernel(x, c, w_ada, b_ada, g_mix, g_ffn, g_final, e_w_in, e_w_out, s5_a_re, s5_a_im, s5_log_dt,
           s5_b_re, s5_b_im, s5_c_re, s5_c_im, s5_d, s5_w_glu, s5_b_glu, ff_w_gate, ff_w_up,
           ff_w_down, o_w_in, o_w_out, mla_g_kv, mla_w_uk, mla_w_uv, moe_w_router, moe_b_router,
           moe_w_gate, moe_w_up, moe_w_down):
    bn, seq, d = x.shape
    t = bn * seq
    mod = _adaln(c, w_ada, b_ada)

    def mod_vec(layer, k):
        return mod[layer, :, k * d:(k + 1) * d].reshape(bn, 1, d)

    tabs64 = _rope_tables(seq, HEAD_DIM)
    tabs16 = _rope_tables(seq, ROPE_DIM)
    h = x.reshape(t, d)

    qkv, s_in = _proj0(h, g_mix[0].reshape(1, d), mod_vec(0, 0), mod_vec(0, 1),
                       e_w_in[0].astype(BF16), tabs64, seq)
    y_a = _dilated_attention(qkv, bn, seq)
    prm = _s5_prepare(s5_a_re[0], s5_a_im[0], s5_log_dt[0], s5_b_re[0], s5_b_im[0],
                      s5_c_re[0], s5_c_im[0], s5_d[0], seq // S5_CHUNK)
    y_b = _s5_mixer(s_in, prm, s5_w_glu[0].astype(BF16), s5_b_glu[0], bn, seq)
    h = _outproj([y_a, y_b], e_w_out[0].astype(BF16), h, mod_vec(0, 2), seq)
    h = _ffn_dense(h, g_ffn[0].reshape(1, d), mod_vec(0, 3), mod_vec(0, 4), mod_vec(0, 5),
                   ff_w_gate[0].astype(BF16), ff_w_up[0].astype(BF16), ff_w_down[0].astype(BF16), seq)

    w1 = _proj1_layout(o_w_in[0]).astype(BF16)
    qa, kp, qi, ki, wi = _proj1(h, g_mix[1].reshape(1, d), mod_vec(1, 0), mod_vec(1, 1), w1,
                                mla_g_kv[0].reshape(1, C_LATENT).astype(F32), tabs16, tabs64, seq)
    mask = _dsa_select(qi, ki, wi, bn, seq)
    wuk_x, wuv_x = _mla_weights(mla_w_uk[0], mla_w_uv[0])
    o_attn = _dsa_attention(qa, kp, mask, wuk_x, wuv_x, bn, seq)
    h = _outproj([o_attn.reshape(t, C_HEADS * HEAD_DIM)], o_w_out[0].astype(BF16), h, mod_vec(1, 2), seq)

    route, up, counts = _router(h, g_ffn[1].reshape(1, d), mod_vec(1, 3), mod_vec(1, 4),
                                moe_w_router[0], moe_b_router[0], seq)
    pos1, pos2, tile_expert, n_live, nrows = _moe_layout(route, counts)
    xs = _dispatch(pos1, pos2, up, nrows)
    y_sorted = _experts(tile_expert, n_live, xs, moe_w_gate[0].astype(BF16), moe_w_up[0].astype(BF16),
                        moe_w_down[0].astype(BF16), MOE_TILE)
    out = _combine(pos1, pos2, route, h, mod_vec(1, 5), g_final.reshape(1, d), y_sorted, seq)
    return out.reshape(bn, seq, d)
```

```python
import functools
import math

import jax
import jax.numpy as jnp
import numpy as np
from jax import lax
from jax.experimental import pallas as pl
from jax.experimental.pallas import tpu as pltpu

F32 = jnp.float32
BF16 = jnp.bfloat16
I32 = jnp.int32
HIGHEST = lax.Precision.HIGHEST

D_MODEL = 1024
HEAD_DIM = 64
ROPE_DIM = 16
ROPE_THETA = 500000.0
NORM_EPS = 1e-6
A_HEADS = 8
A_WIDTH = 512
A_PATTERNS = ((128, 1), (512, 4), (2048, 16))
B_WIDTH = 512
B_GROUP = 16
B_GROUPS = 32
B_STATE = 64
C_HEADS = 16
C_NOPE = 48
C_LATENT = 256
IDX_HEADS = 8
IDX_DIM = 64
TOPK_MAX = 256
FF_DENSE = 2816
N_EXPERTS = 8
FF_EXPERT = 3584

LANES = 128
Q_BLOCK = 128
S5_CHUNK = 16
VMEM_LIMIT = 56 * 1024 * 1024
NEG_BIG = -1e30
INT_MIN = -(2 ** 31)


def _cparams(sem, vmem=VMEM_LIMIT):
    return pltpu.CompilerParams(dimension_semantics=sem, vmem_limit_bytes=vmem)


def _dot(a, b):
    return jnp.dot(a, b, preferred_element_type=F32)


def _dot_nt(a, b):
    return lax.dot_general(a, b, (((1,), (1,)), ((), ())), preferred_element_type=F32)


def _norm_mod(x, g, shift, scale):
    ms = jnp.mean(x * x, axis=-1, keepdims=True)
    y = x * lax.rsqrt(ms + NORM_EPS) * g
    return y * (1.0 + scale) + shift


def _rope128(x, ct, sa, sb):
    return x * ct + pltpu.roll(x, LANES - ROPE_DIM // 2, 1) * sa + pltpu.roll(x, ROPE_DIM // 2, 1) * sb


def _adaln_body(c_ref, w_ref, b_ref, o_ref):
    c = c_ref[...]
    ca = c * jax.nn.sigmoid(c)
    o_ref[0] = jnp.dot(ca, w_ref[0], preferred_element_type=F32, precision=HIGHEST) + b_ref[0]


def _adaln(c, w_ada, b_ada):
    depth, d, d6 = w_ada.shape
    bn = c.shape[0]
    rows = 8
    cp = jnp.zeros((rows, d), F32).at[:bn].set(c)
    tn = 1536
    out = pl.pallas_call(
        _adaln_body,
        grid=(depth, d6 // tn),
        in_specs=[pl.BlockSpec((rows, d), lambda l, j: (0, 0)),
                  pl.BlockSpec((1, d, tn), lambda l, j: (l, 0, j)),
                  pl.BlockSpec((1, 1, tn), lambda l, j: (l, 0, j))],
        out_specs=pl.BlockSpec((1, rows, tn), lambda l, j: (l, 0, j)),
        out_shape=jax.ShapeDtypeStruct((depth, rows, d6), F32),
        compiler_params=_cparams(("arbitrary", "arbitrary")),
        name="adaln",
    )(cp, w_ada, b_ada.reshape(depth, 1, d6))
    return out[:, :bn]


def _rope_tables(seq, period):
    half = ROPE_DIM // 2
    pos = jnp.arange(seq, dtype=F32)
    inv = ROPE_THETA ** (-jnp.arange(0, ROPE_DIM, 2, dtype=F32) / ROPE_DIM)
    ang = pos[:, None] * inv[None, :]
    cos, sin = jnp.cos(ang), jnp.sin(ang)
    lane = np.arange(LANES) % period
    first = lane < half
    second = (lane >= half) & (lane < ROPE_DIM)
    idx = np.where(first, lane, np.where(second, lane - half, 0))
    cos_l, sin_l = cos[:, idx], sin[:, idx]
    ct = jnp.where(first | second, cos_l, 1.0)
    sa = jnp.where(first, -sin_l, 0.0)
    sb = jnp.where(second, sin_l, 0.0)
    return ct, sa, sb


def _proj0_body(x_ref, g_ref, sh_ref, sc_ref, w_ref, ct_ref, sa_ref, sb_ref, *rest):
    qkv_refs, s_ref, acc_scr = rest[:-2], rest[-2], rest[-1]
    tm = x_ref.shape[0]
    qw = 3 * A_WIDTH
    u = _norm_mod(x_ref[...], g_ref[...], sh_ref[0], sc_ref[0]).astype(BF16)
    ct, sa, sb = ct_ref[...], sa_ref[...], sb_ref[...]
    for j in range(3):
        acc = _dot(u, w_ref[:, j * A_WIDTH:(j + 1) * A_WIDTH])
        for c in range(A_WIDTH // LANES):
            a = acc[:, c * LANES:(c + 1) * LANES]
            if j < 2:
                a = _rope128(a, ct, sa, sb)
            if j == 0:
                a = a * (HEAD_DIM ** -0.5)
            acc_scr[j * (A_WIDTH // LANES) + c] = a
    acc = _dot(u, w_ref[:, qw:])
    nq = qw // LANES
    for c in range(B_WIDTH // LANES):
        acc_scr[nq + c] = acc[:, c * LANES:(c + 1) * LANES]
    for (_, dil), ref in zip(A_PATTERNS, qkv_refs):
        for r in range(dil):
            for c in range(nq):
                ref[:, r * qw + c * LANES:r * qw + (c + 1) * LANES] = (
                    acc_scr[c, pl.ds(r, tm // dil, stride=dil), :].astype(BF16))
    for r in range(S5_CHUNK):
        for c in range(B_WIDTH // LANES):
            s_ref[:, r * B_WIDTH + c * LANES:r * B_WIDTH + (c + 1) * LANES] = (
                acc_scr[nq + c, pl.ds(r, tm // S5_CHUNK, stride=S5_CHUNK), :])


def _proj0(x2, g, shift, scale, w_bf, tabs, seq, tm=512):
    t, d = x2.shape
    n = w_bf.shape[1]
    tpb = seq // tm
    ct, sa, sb = tabs
    tab_spec = pl.BlockSpec((tm, LANES), lambda i: (i % tpb, 0))
    mod_spec = pl.BlockSpec((1, 1, d), lambda i: (i // tpb, 0, 0))
    qw = 3 * A_WIDTH
    dils = [dil for _, dil in A_PATTERNS]
    out_specs = [pl.BlockSpec((tm // dil, dil * qw), lambda i: (i, 0)) for dil in dils]
    out_shape = [jax.ShapeDtypeStruct((t // dil, dil * qw), BF16) for dil in dils]
    out_specs.append(pl.BlockSpec((tm // S5_CHUNK, S5_CHUNK * B_WIDTH), lambda i: (i, 0)))
    out_shape.append(jax.ShapeDtypeStruct((t // S5_CHUNK, S5_CHUNK * B_WIDTH), F32))
    return pl.pallas_call(
        _proj0_body,
        grid=(t // tm,),
        in_specs=[pl.BlockSpec((tm, d), lambda i: (i, 0)),
                  pl.BlockSpec((1, d), lambda i: (0, 0)),
                  mod_spec, mod_spec,
                  pl.BlockSpec((d, n), lambda i: (0, 0)),
                  tab_spec, tab_spec, tab_spec],
        out_specs=out_specs,
        out_shape=out_shape,
        scratch_shapes=[pltpu.VMEM((n // LANES, tm, LANES), F32)],
        compiler_params=_cparams(("parallel",)),
        name="proj0",
    )(x2, g, shift, scale, w_bf, ct, sa, sb)


def _dil_body(q_ref, kp_ref, kc_ref, vp_ref, vc_ref, m_ref, l_ref, o_ref):
    i = pl.program_id(2)
    q = q_ref[0]
    kcat = jnp.concatenate([kp_ref[0], kc_ref[0]], axis=0)
    vcat = jnp.concatenate([vp_ref[0], vc_ref[0]], axis=0)
    row = lax.broadcasted_iota(I32, (Q_BLOCK, 2 * Q_BLOCK), 0)
    col = lax.broadcasted_iota(I32, (Q_BLOCK, 2 * Q_BLOCK), 1)
    rel = row + Q_BLOCK - col
    valid = (rel >= 0) & (rel <= Q_BLOCK) & ((col >= Q_BLOCK) | (i > 0))
    lane = lax.broadcasted_iota(I32, (1, LANES), 1)
    for p in range(A_WIDTH // LANES):
        sl = slice(p * LANES, (p + 1) * LANES)
        qp, kp, vp = q[:, sl], kcat[:, sl], vcat[:, sl]
        m_pair = l_pair = o_pair = None
        for hh in range(LANES // HEAD_DIM):
            hm = (lane // HEAD_DIM) == hh
            qh = jnp.where(hm, qp, jnp.zeros_like(qp))
            s = _dot_nt(qh, kp)
            s = jnp.where(valid, s, -jnp.inf)
            m = jnp.max(s, axis=1, keepdims=True)
            pr = jnp.exp(s - m)
            l = jnp.sum(pr, axis=1, keepdims=True)
            o = _dot(pr.astype(BF16), vp)
            if hh == 0:
                m_pair = jnp.broadcast_to(m, (Q_BLOCK, LANES))
                l_pair = jnp.broadcast_to(l, (Q_BLOCK, LANES))
                o_pair = o
            else:
                m_pair = jnp.where(hm, m, m_pair)
                l_pair = jnp.where(hm, l, l_pair)
                o_pair = jnp.where(hm, o, o_pair)
        m_ref[0, :, sl] = m_pair
        l_ref[0, :, sl] = l_pair
        o_ref[0, :, sl] = o_pair


def _dilated_branch(qkv_d, dil, bn, seq):
    sd = seq // dil
    nb = sd // Q_BLOCK
    blk = (1, Q_BLOCK, A_WIDTH)
    q_spec = pl.BlockSpec(blk, lambda b, r, i: (b, i, r * 3))
    kp_spec = pl.BlockSpec(blk, lambda b, r, i: (b, jnp.maximum(i - 1, 0), r * 3 + 1))
    kc_spec = pl.BlockSpec(blk, lambda b, r, i: (b, i, r * 3 + 1))
    vp_spec = pl.BlockSpec(blk, lambda b, r, i: (b, jnp.maximum(i - 1, 0), r * 3 + 2))
    vc_spec = pl.BlockSpec(blk, lambda b, r, i: (b, i, r * 3 + 2))
    st_spec = pl.BlockSpec(blk, lambda b, r, i: (b, i, r))
    outs = pl.pallas_call(
        _dil_body,
        grid=(bn, dil, nb),
        in_specs=[q_spec, kp_spec, kc_spec, vp_spec, vc_spec],
        out_specs=[st_spec] * 3,
        out_shape=[jax.ShapeDtypeStruct((bn, sd, dil * A_WIDTH), F32)] * 3,
        compiler_params=_cparams(("parallel", "parallel", "arbitrary")),
        name=f"dilated_d{dil}",
    )(*([qkv_d.reshape(bn, sd, dil * 3 * A_WIDTH)] * 5))
    return [o.reshape(bn * sd, dil * A_WIDTH) for o in outs]


def _dil_merge_body(*refs):
    nbr = len(A_PATTERNS)
    stats = refs[:3 * nbr]
    y_ref = refs[3 * nbr]
    scr = refs[3 * nbr + 1:]
    tm = y_ref.shape[0]
    ms, ls, os_ = [], [], []
    si = 0
    for gi, (_, dil) in enumerate(A_PATTERNS):
        trio = []
        for ref in stats[3 * gi:3 * gi + 3]:
            if dil == 1:
                trio.append(ref[...])
            else:
                nc = A_WIDTH // LANES
                for r in range(dil):
                    for c in range(nc):
                        scr[si][c, pl.ds(r, tm // dil, stride=dil), :] = (
                            ref[:, r * A_WIDTH + c * LANES:r * A_WIDTH + (c + 1) * LANES])
                trio.append(jnp.concatenate([scr[si][c] for c in range(nc)], axis=1))
                si += 1
        ms.append(trio[0])
        ls.append(trio[1])
        os_.append(trio[2])
    m_max = functools.reduce(jnp.maximum, ms)
    den = jnp.zeros_like(m_max)
    num = jnp.zeros_like(m_max)
    for m, l, o in zip(ms, ls, os_):
        w = jnp.exp(m - m_max)
        den = den + w * l
        num = num + w * o
    y_ref[...] = (num / den).astype(y_ref.dtype)


def _dilated_attention(qkv_views, bn, seq, tm=512):
    t = bn * seq
    stats, in_specs = [], []
    nscr = 0
    for (_, dil), qkv_d in zip(A_PATTERNS, qkv_views):
        stats += _dilated_branch(qkv_d, dil, bn, seq)
        in_specs += [pl.BlockSpec((tm // dil, dil * A_WIDTH), lambda i: (i, 0))] * 3
        nscr += 3 if dil > 1 else 0
    return pl.pallas_call(
        _dil_merge_body,
        grid=(t // tm,),
        in_specs=in_specs,
        out_specs=pl.BlockSpec((tm, A_WIDTH), lambda i: (i, 0)),
        out_shape=jax.ShapeDtypeStruct((t, A_WIDTH), BF16),
        scratch_shapes=[pltpu.VMEM((A_WIDTH // LANES, tm, LANES), F32)] * nscr,
        compiler_params=_cparams(("parallel",)),
        name="dilated_merge",
    )(*stats)


def _s5_prepare(a_re, a_im, log_dt, b_re, b_im, c_re, c_im, d_skip, nchunk):
    dt = jnp.exp(log_dt.astype(F32))[:, None]
    lr, li = a_re.astype(F32), a_im.astype(F32)
    mag = jnp.exp(lr * dt)
    ar = mag * jnp.cos(li * dt)
    ai = mag * jnp.sin(li * dt)
    den = lr * lr + li * li
    nr = ar - 1.0
    fr = (nr * lr + ai * li) / den
    fi = (ai * lr - nr * li) / den
    br, bi = b_re.astype(F32), b_im.astype(F32)
    bbr = fr[..., None] * br - fi[..., None] * bi
    bbi = fr[..., None] * bi + fi[..., None] * br
    nblk = B_WIDTH // LANES
    gpb = B_GROUPS // nblk
    eye = jnp.eye(gpb, dtype=F32)

    def bdiag_in(m):
        m = jnp.swapaxes(m.reshape(nblk, gpb, B_STATE, B_GROUP), 2, 3)
        m = m[:, :, :, None, :] * eye[None, :, None, :, None]
        return m.reshape(nblk, gpb * B_GROUP, gpb * B_STATE)

    def bdiag_out(m):
        m = jnp.swapaxes(m.reshape(nblk, gpb, B_GROUP, B_STATE), 2, 3)
        m = m[:, :, :, None, :] * eye[None, :, None, :, None]
        return m.reshape(nblk, gpb * B_STATE, gpb * B_GROUP)

    pr, pi = ar, ai
    for _ in range(int(math.log2(S5_CHUNK))):
        pr, pi = pr * pr - pi * pi, 2.0 * pr * pi
    pows_r, pows_i = [], []
    for _ in range(max(1, int(math.ceil(math.log2(nchunk))))):
        pows_r.append(pr.reshape(1, -1))
        pows_i.append(pi.reshape(1, -1))
        pr, pi = pr * pr - pi * pi, 2.0 * pr * pi
    return dict(
        ar=ar.reshape(1, -1), ai=ai.reshape(1, -1),
        b_re=bdiag_in(bbr).astype(BF16), b_im=bdiag_in(bbi).astype(BF16),
        c_re=bdiag_out(c_re.astype(F32)).astype(BF16), c_im=bdiag_out(-c_im.astype(F32)).astype(BF16),
        d=d_skip.astype(F32).reshape(1, B_WIDTH),
        pows_r=jnp.concatenate(pows_r, axis=0), pows_i=jnp.concatenate(pows_i, axis=0))


def _s5_local_body(u_ref, bre_ref, bim_ref, ar_ref, ai_ref, sre_ref, sim_ref):
    tr = u_ref.shape[0]
    sw = bre_ref.shape[2]
    for blk in range(B_WIDTH // LANES):
        arb = ar_ref[:, blk * sw:(blk + 1) * sw]
        aib = ai_ref[:, blk * sw:(blk + 1) * sw]
        sr = jnp.zeros((tr, sw), F32)
        si = jnp.zeros((tr, sw), F32)
        for t in range(S5_CHUNK):
            ub = u_ref[:, t * B_WIDTH + blk * LANES:t * B_WIDTH + (blk + 1) * LANES].astype(BF16)
            sr, si = (arb * sr - aib * si + _dot(ub, bre_ref[blk]),
                      arb * si + aib * sr + _dot(ub, bim_ref[blk]))
        sre_ref[:, blk * sw:(blk + 1) * sw] = sr
        sim_ref[:, blk * sw:(blk + 1) * sw] = si


def _s5_scan_body(nsteps, sre_ref, sim_ref, pr_ref, pi_ref, xre_ref, xim_ref):
    nc = sre_ref.shape[0]
    sw = 512
    row = lax.broadcasted_iota(I32, (nc, 1), 0)
    for cb in range(sre_ref.shape[1] // sw):
        sl = slice(cb * sw, (cb + 1) * sw)
        xr, xi = sre_ref[:, sl], sim_ref[:, sl]
        for s in range(nsteps):
            sh = 1 << s
            ok = row >= sh
            rr = jnp.where(ok, pltpu.roll(xr, sh, 0), 0.0)
            ri = jnp.where(ok, pltpu.roll(xi, sh, 0), 0.0)
            pr, pi = pr_ref[s:s + 1, sl], pi_ref[s:s + 1, sl]
            xr, xi = xr + pr * rr - pi * ri, xi + pr * ri + pi * rr
        ok = row >= 1
        xre_ref[:, sl] = jnp.where(ok, pltpu.roll(xr, 1, 0), 0.0)
        xim_ref[:, sl] = jnp.where(ok, pltpu.roll(xi, 1, 0), 0.0)


def _s5_out_body(u_ref, xre_ref, xim_ref, bre_ref, bim_ref, cre_ref, cim_ref, ar_ref, ai_ref,
                 d_ref, wglu_ref, bglu_ref, out_ref, y_scr):
    sw = bre_ref.shape[2]
    for blk in range(B_WIDTH // LANES):
        arb = ar_ref[:, blk * sw:(blk + 1) * sw]
        aib = ai_ref[:, blk * sw:(blk + 1) * sw]
        xr = xre_ref[:, blk * sw:(blk + 1) * sw]
        xi = xim_ref[:, blk * sw:(blk + 1) * sw]
        db = d_ref[:, blk * LANES:(blk + 1) * LANES]
        for t in range(S5_CHUNK):
            cs = slice(t * B_WIDTH + blk * LANES, t * B_WIDTH + (blk + 1) * LANES)
            uf = u_ref[:, cs]
            ub = uf.astype(BF16)
            xr, xi = (arb * xr - aib * xi + _dot(ub, bre_ref[blk]),
                      arb * xi + aib * xr + _dot(ub, bim_ref[blk]))
            y = _dot(xr.astype(BF16), cre_ref[blk]) + _dot(xi.astype(BF16), cim_ref[blk]) + db * uf
            y_scr[:, cs] = jax.nn.gelu(y, approximate=True)
    for t in range(S5_CHUNK):
        cs = slice(t * B_WIDTH, (t + 1) * B_WIDTH)
        y = y_scr[:, cs]
        z = _dot(y.astype(BF16), wglu_ref[...]) + bglu_ref[...]
        out_ref[:, cs] = (y * jax.nn.sigmoid(z)).astype(out_ref.dtype)


def _s5_mixer(s_in, prm, w_glu_bf, b_glu, bn, seq, tr=128):
    t = bn * seq
    nrow = t // S5_CHUNK
    ncb = seq // S5_CHUNK
    width = S5_CHUNK * B_WIDTH
    sdim = B_GROUPS * B_STATE
    uc = s_in.reshape(nrow, width)
    nblk = B_WIDTH // LANES
    const3 = lambda i: (0, 0, 0)
    const2 = lambda i: (0, 0)
    b_spec = pl.BlockSpec((nblk, LANES, sdim // nblk), const3)
    c_spec = pl.BlockSpec((nblk, sdim // nblk, LANES), const3)
    a_spec = pl.BlockSpec((1, sdim), const2)
    s_re, s_im = pl.pallas_call(
        _s5_local_body,
        grid=(nrow // tr,),
        in_specs=[pl.BlockSpec((tr, width), lambda i: (i, 0)), b_spec, b_spec, a_spec, a_spec],
        out_specs=[pl.BlockSpec((tr, sdim), lambda i: (i, 0))] * 2,
        out_shape=[jax.ShapeDtypeStruct((nrow, sdim), F32)] * 2,
        compiler_params=_cparams(("parallel",)),
        name="s5_local",
    )(uc, prm['b_re'], prm['b_im'], prm['ar'], prm['ai'])
    nsteps = prm['pows_r'].shape[0]
    x_re, x_im = pl.pallas_call(
        functools.partial(_s5_scan_body, nsteps),
        grid=(bn,),
        in_specs=[pl.BlockSpec((ncb, sdim), lambda b: (b, 0))] * 2
                 + [pl.BlockSpec((nsteps, sdim), lambda b: (0, 0))] * 2,
        out_specs=[pl.BlockSpec((ncb, sdim), lambda b: (b, 0))] * 2,
        out_shape=[jax.ShapeDtypeStruct((nrow, sdim), F32)] * 2,
        compiler_params=_cparams(("parallel",)),
        name="s5_scan",
    )(s_re, s_im, prm['pows_r'], prm['pows_i'])
    y = pl.pallas_call(
        _s5_out_body,
        grid=(nrow // tr,),
        in_specs=[pl.BlockSpec((tr, width), lambda i: (i, 0)),
                  pl.BlockSpec((tr, sdim), lambda i: (i, 0)),
                  pl.BlockSpec((tr, sdim), lambda i: (i, 0)),
                  b_spec, b_spec, c_spec, c_spec, a_spec, a_spec,
                  pl.BlockSpec((1, B_WIDTH), const2),
                  pl.BlockSpec((B_WIDTH, B_WIDTH), const2),
                  pl.BlockSpec((1, B_WIDTH), const2)],
        out_specs=pl.BlockSpec((tr, width), lambda i: (i, 0)),
        out_shape=jax.ShapeDtypeStruct((nrow, width), BF16),
        scratch_shapes=[pltpu.VMEM((tr, width), F32)],
        compiler_params=_cparams(("parallel",)),
        name="s5_out",
    )(uc, x_re, x_im, prm['b_re'], prm['b_im'], prm['c_re'], prm['c_im'], prm['ar'], prm['ai'],
      prm['d'], w_glu_bf, b_glu.reshape(1, B_WIDTH).astype(F32))
    return y


def _outproj_body(nparts, nflat, *refs):
    parts = refs[:nparts]
    flats = refs[nparts:nparts + nflat]
    w_ref, h_ref, gt_ref, o_ref = refs[nparts + nflat:nparts + nflat + 4]
    scr = refs[nparts + nflat + 4:]
    tm = h_ref.shape[0]
    acc = None
    off = 0
    for p in parts:
        k = p.shape[1]
        d = _dot(p[...].astype(BF16), w_ref[off:off + k, :])
        acc = d if acc is None else acc + d
        off += k
    for p, s in zip(flats, scr):
        nc = s.shape[0]
        k = nc * LANES
        for r in range(S5_CHUNK):
            for c in range(nc):
                s[c, pl.ds(r, tm // S5_CHUNK, stride=S5_CHUNK), :] = (
                    p[:, r * k + c * LANES:r * k + (c + 1) * LANES].astype(F32))
        rows = jnp.concatenate([s[c] for c in range(nc)], axis=1)
        d = _dot(rows.astype(BF16), w_ref[off:off + k, :])
        acc = d if acc is None else acc + d
        off += k
    o_ref[...] = h_ref[...] + gt_ref[0] * acc


def _outproj(parts, flat_parts, w_bf, h2, gate, seq, tm=512):
    t, d = h2.shape
    tpb = seq // tm
    in_specs = [pl.BlockSpec((tm, p.shape[1]), lambda i: (i, 0)) for p in parts]
    in_specs += [pl.BlockSpec((tm // S5_CHUNK, p.shape[1]), lambda i: (i, 0)) for p in flat_parts]
    in_specs += [pl.BlockSpec(w_bf.shape, lambda i: (0, 0)),
                 pl.BlockSpec((tm, d), lambda i: (i, 0)),
                 pl.BlockSpec((1, 1, d), lambda i: (i // tpb, 0, 0))]
    return pl.pallas_call(
        functools.partial(_outproj_body, len(parts), len(flat_parts)),
        grid=(t // tm,),
        in_specs=in_specs,
        out_specs=pl.BlockSpec((tm, d), lambda i: (i, 0)),
        out_shape=jax.ShapeDtypeStruct((t, d), F32),
        scratch_shapes=[pltpu.VMEM((p.shape[1] // S5_CHUNK // LANES, tm, LANES), F32) for p in flat_parts],
        compiler_params=_cparams(("parallel",)),
        name="outproj",
    )(*parts, *flat_parts, w_bf, h2, gate)


def _ffn_body(h_ref, g_ref, sh_ref, sc_ref, gt_ref, wg_ref, wu_ref, wd_ref, o_ref, u_scr, acc_scr):
    f = pl.program_id(1)

    @pl.when(f == 0)
    def _():
        u_scr[...] = _norm_mod(h_ref[...], g_ref[...], sh_ref[0], sc_ref[0]).astype(BF16)
        acc_scr[...] = jnp.zeros_like(acc_scr)

    u = u_scr[...]
    gp = _dot(u, wg_ref[0])
    up = _dot(u, wu_ref[0])
    a = (gp * jax.nn.sigmoid(gp) * up).astype(BF16)
    acc_scr[...] += _dot(a, wd_ref[...])

    @pl.when(f == pl.num_programs(1) - 1)
    def _():
        o_ref[...] = h_ref[...] + gt_ref[0] * acc_scr[...]


def _tile_cols(w, tf):
    *lead, d, f = w.shape
    w = w.reshape(*lead, d, f // tf, tf)
    return jnp.swapaxes(w, -3, -2)


def _ffn_dense(h2, g, shift, scale, gate, wg_t, wu_t, wd, seq, tm=1024):
    t, d = h2.shape
    nf, _, tf = wg_t.shape
    tpb = seq // tm
    mod_spec = pl.BlockSpec((1, 1, d), lambda i, f: (i // tpb, 0, 0))
    return pl.pallas_call(
        _ffn_body,
        grid=(t // tm, nf),
        in_specs=[pl.BlockSpec((tm, d), lambda i, f: (i, 0)),
                  pl.BlockSpec((1, d), lambda i, f: (0, 0)),
                  mod_spec, mod_spec, mod_spec,
                  pl.BlockSpec((1, d, tf), lambda i, f: (f, 0, 0)),
                  pl.BlockSpec((1, d, tf), lambda i, f: (f, 0, 0)),
                  pl.BlockSpec((tf, d), lambda i, f: (f, 0))],
        out_specs=pl.BlockSpec((tm, d), lambda i, f: (i, 0)),
        out_shape=jax.ShapeDtypeStruct((t, d), F32),
        scratch_shapes=[pltpu.VMEM((tm, d), BF16), pltpu.VMEM((tm, d), F32)],
        compiler_params=_cparams(("parallel", "arbitrary")),
        name="ffn_dense",
    )(h2, g, shift, scale, gate, wg_t, wu_t, wd)


P1_QR = 0
P1_QN = 256
P1_CKV = 1280
P1_KR = 1536
P1_QI = 1792
P1_KI = 2304
P1_WI = 2432
P1_COLS = 2560


def _proj1_layout(w_in):
    d = w_in.shape[0]
    c0 = 0
    w_qr = w_in[:, c0:c0 + C_HEADS * ROPE_DIM]; c0 += C_HEADS * ROPE_DIM
    w_qn = w_in[:, c0:c0 + C_HEADS * C_NOPE]; c0 += C_HEADS * C_NOPE
    w_ckv = w_in[:, c0:c0 + C_LATENT]; c0 += C_LATENT
    w_kr = w_in[:, c0:c0 + ROPE_DIM]; c0 += ROPE_DIM
    w_qi = w_in[:, c0:c0 + IDX_HEADS * IDX_DIM]; c0 += IDX_HEADS * IDX_DIM
    w_ki = w_in[:, c0:c0 + IDX_DIM]; c0 += IDX_DIM
    w_wi = w_in[:, c0:c0 + IDX_HEADS]
    w_qn = jnp.pad(w_qn.reshape(d, C_HEADS, C_NOPE), ((0, 0), (0, 0), (0, HEAD_DIM - C_NOPE)))
    w_qn = w_qn.reshape(d, C_HEADS * HEAD_DIM)
    w_kr = jnp.tile(w_kr, (1, C_HEADS))
    w_ki = jnp.tile(w_ki, (1, 2))
    w_wi = jnp.pad(w_wi, ((0, 0), (0, LANES - IDX_HEADS)))
    return jnp.concatenate([w_qr, w_qn, w_ckv, w_kr, w_qi, w_ki, w_wi], axis=1)


def _proj1_body(x_ref, g_ref, sh_ref, sc_ref, w_ref, gkv_ref,
                ct16_ref, sa16_ref, sb16_ref, ct64_ref, sa64_ref, sb64_ref,
                qa_ref, kp_ref, qi_ref, ki_ref, wi_ref):
    u = _norm_mod(x_ref[...], g_ref[...], sh_ref[0], sc_ref[0]).astype(BF16)
    t16 = (ct16_ref[...], sa16_ref[...], sb16_ref[...])
    t64 = (ct64_ref[...], sa64_ref[...], sb64_ref[...])
    qscale = HEAD_DIM ** -0.5

    def cols(lo, hi):
        return _dot(u, w_ref[:, lo:hi])

    a = cols(P1_QR, P1_QN)
    for c in range(2):
        blk = _rope128(a[:, c * LANES:(c + 1) * LANES], *t16) * qscale
        qa_ref[:, c * LANES:(c + 1) * LANES] = blk.astype(BF16)
    qa_ref[:, P1_QN:P1_CKV] = (cols(P1_QN, P1_CKV) * qscale).astype(BF16)
    ckv = cols(P1_CKV, P1_KR)
    ms = jnp.mean(ckv * ckv, axis=-1, keepdims=True)
    kp_ref[:, 0:C_LATENT] = (ckv * lax.rsqrt(ms + NORM_EPS) * gkv_ref[...]).astype(BF16)
    a = cols(P1_KR, P1_QI)
    for c in range(2):
        blk = _rope128(a[:, c * LANES:(c + 1) * LANES], *t16)
        kp_ref[:, C_LATENT + c * LANES:C_LATENT + (c + 1) * LANES] = blk.astype(BF16)
    a = cols(P1_QI, P1_KI)
    for c in range(4):
        qi_ref[:, c * LANES:(c + 1) * LANES] = _rope128(a[:, c * LANES:(c + 1) * LANES], *t64).astype(BF16)
    ki_ref[...] = _rope128(cols(P1_KI, P1_WI), *t64).astype(BF16)
    wi_ref[...] = cols(P1_WI, P1_COLS) * (IDX_HEADS ** -0.5) * (IDX_DIM ** -0.5)


def _proj1(x2, g, shift, scale, w_bf, gkv, t16, t64, seq, tm=512):
    t, d = x2.shape
    tpb = seq // tm
    tab_spec = pl.BlockSpec((tm, LANES), lambda i: (i % tpb, 0))
    mod_spec = pl.BlockSpec((1, 1, d), lambda i: (i // tpb, 0, 0))
    widths = (P1_CKV, 2 * C_LATENT, IDX_HEADS * IDX_DIM, LANES, LANES)
    dtypes = (BF16, BF16, BF16, BF16, F32)
    return pl.pallas_call(
        _proj1_body,
        grid=(t // tm,),
        in_specs=[pl.BlockSpec((tm, d), lambda i: (i, 0)),
                  pl.BlockSpec((1, d), lambda i: (0, 0)),
                  mod_spec, mod_spec,
                  pl.BlockSpec((d, P1_COLS), lambda i: (0, 0)),
                  pl.BlockSpec((1, C_LATENT), lambda i: (0, 0))] + [tab_spec] * 6,
        out_specs=[pl.BlockSpec((tm, w), lambda i: (i, 0)) for w in widths],
        out_shape=[jax.ShapeDtypeStruct((t, w), dt) for w, dt in zip(widths, dtypes)],
        compiler_params=_cparams(("parallel",)),
        name="proj1",
    )(x2, g, shift, scale, w_bf, gkv, *t16, *t64)


IDX_KT = 512


def _idx_body(topk, qi_ref, ki_ref, wi_ref, mask_ref, sc_scr):
    i = pl.program_id(1)
    seq = ki_ref.shape[1]
    nkt = seq // IDX_KT
    nlive = (i * Q_BLOCK) // IDX_KT + 1
    q = qi_ref[0]
    wt = wi_ref[0].T
    lane = lax.broadcasted_iota(I32, (1, LANES), 1)
    qpos = i * Q_BLOCK + lane
    krow = lax.broadcasted_iota(I32, (IDX_KT, 1), 0)
    qpair = [jnp.concatenate([q[:, (2 * g) * LANES:(2 * g + 1) * LANES],
                              q[:, (2 * g + 1) * LANES:(2 * g + 2) * LANES]], axis=0) for g in range(2)]

    def score_tile(kt, c):
        kk = ki_ref[0, pl.ds(pl.multiple_of(kt * IDX_KT, IDX_KT), IDX_KT), :]
        zero = jnp.zeros_like(kk)
        kpart = [jnp.where(lane < IDX_DIM, kk, zero), jnp.where(lane >= IDX_DIM, kk, zero)]
        acc = jnp.zeros((IDX_KT, Q_BLOCK), F32)
        for g in range(2):
            for part in range(2):
                res = jnp.maximum(_dot_nt(kpart[part], qpair[g]), 0.0)
                for j in range(2):
                    h = 2 * (2 * g + j) + part
                    acc = acc + wt[h:h + 1, :] * res[:, j * Q_BLOCK:(j + 1) * Q_BLOCK]
        acc = acc + 0.0
        sc_scr[kt] = jnp.where(kt * IDX_KT + krow <= qpos, acc, -jnp.inf)
        return c

    lax.fori_loop(0, nlive, score_tile, 0)
    kq = jnp.minimum(qpos + 1, topk).astype(F32)

    def count(pred_fn):
        def body(kt, acc):
            ones = jnp.where(pred_fn(sc_scr[kt], kt), 1.0, 0.0)
            part = jnp.sum(ones.reshape(8, IDX_KT // 64, 8, Q_BLOCK), axis=1)
            return acc + jnp.sum(part, axis=0)
        part = lax.fori_loop(0, nlive, body, jnp.zeros((8, Q_BLOCK), F32))
        return jnp.sum(part, axis=0, keepdims=True)

    def key_to_f32(key):
        bits = jnp.where(key < 0, key ^ jnp.int32(0x7FFFFFFF), key)
        return lax.bitcast_convert_type(bits, F32)

    def bit_step(it, ans):
        cand = ans | lax.shift_left(jnp.int32(1), 31 - it)
        thr = key_to_f32(cand ^ jnp.int32(INT_MIN))
        cnt = count(lambda sc, kt: sc >= thr)
        return jnp.where(cnt >= kq, cand, ans)

    ans = lax.fori_loop(0, 32, bit_step, jnp.zeros((1, Q_BLOCK), I32))
    thr = key_to_f32(ans ^ jnp.int32(INT_MIN))
    n_ge = count(lambda sc, kt: sc >= thr)
    nbits = int(math.log2(seq))

    def tie_cut(_):
        need = kq - count(lambda sc, kt: sc > thr)

        def tie_step(it, ans2):
            cand = ans2 | lax.shift_left(jnp.int32(1), nbits - 1 - it)
            cnt = count(lambda sc, kt: (sc == thr) & (kt * IDX_KT + krow < cand))
            return jnp.where(cnt < need, cand, ans2)

        return lax.fori_loop(0, nbits, tie_step, jnp.zeros((1, Q_BLOCK), I32))

    excess = jnp.max(n_ge - kq) > 0.0
    jcut = lax.cond(excess, tie_cut, lambda _: jnp.full((1, Q_BLOCK), seq, I32), 0)

    def emit(kt, c):
        sc = sc_scr[kt]
        sel = (sc > thr) | ((sc == thr) & (kt * IDX_KT + krow <= jcut))
        mask_ref[0, 0, kt] = jnp.where(sel, 1.0, 0.0).T.astype(BF16)
        return c

    lax.fori_loop(0, nlive, emit, 0)

    def emit_dead(kt, c):
        mask_ref[0, 0, kt] = jnp.zeros((Q_BLOCK, IDX_KT), BF16)
        return c

    lax.fori_loop(nlive, nkt, emit_dead, 0)


def _dsa_select(qi, ki, wi, bn, seq):
    topk = min(TOPK_MAX, seq // 4)
    nqb = seq // Q_BLOCK
    nkt = seq // IDX_KT
    return pl.pallas_call(
        functools.partial(_idx_body, topk),
        grid=(bn, nqb),
        in_specs=[pl.BlockSpec((1, Q_BLOCK, IDX_HEADS * IDX_DIM), lambda b, i: (b, i, 0)),
                  pl.BlockSpec((1, seq, LANES), lambda b, i: (b, 0, 0)),
                  pl.BlockSpec((1, Q_BLOCK, LANES), lambda b, i: (b, i, 0))],
        out_specs=pl.BlockSpec((1, 1, nkt, Q_BLOCK, IDX_KT), lambda b, i: (b, i, 0, 0, 0)),
        out_shape=jax.ShapeDtypeStruct((bn, nqb, nkt, Q_BLOCK, IDX_KT), BF16),
        scratch_shapes=[pltpu.VMEM((nkt, IDX_KT, Q_BLOCK), F32)],
        compiler_params=_cparams(("parallel", "arbitrary")),
        name="dsa_select",
    )(qi.reshape(bn, seq, -1), ki.reshape(bn, seq, -1), wi.reshape(bn, seq, -1))


ATT_RB = 256


def _dsa_attn_body(qa_ref, kp_ref, mask_ref, wuk_ref, wuv_ref, o_ref, qp_scr, m_scr, l_scr, acc_scr,
                   s_scr):
    i = pl.program_id(1)
    rows = C_HEADS * Q_BLOCK
    qr = qa_ref[0, :, 0:P1_QN]
    lane = lax.broadcasted_iota(I32, (1, C_HEADS * ROPE_DIM), 1)
    for h in range(C_HEADS):
        p = h // 2
        qn = qa_ref[0, :, P1_QN + p * LANES:P1_QN + (p + 1) * LANES]
        qlat = _dot(qn, wuk_ref[h])
        qp_scr[h * Q_BLOCK:(h + 1) * Q_BLOCK, 0:C_LATENT] = qlat.astype(BF16)
        qp_scr[h * Q_BLOCK:(h + 1) * Q_BLOCK, C_LATENT:2 * C_LATENT] = jnp.where(
            (lane // ROPE_DIM) == h, qr, jnp.zeros_like(qr))
    m_scr[...] = jnp.full(m_scr.shape, NEG_BIG, F32)
    l_scr[...] = jnp.zeros_like(l_scr)
    acc_scr[...] = jnp.zeros_like(acc_scr)
    nblk = rows // ATT_RB

    def ktile(kt):
        return kp_ref[0, pl.ds(pl.multiple_of(kt * IDX_KT, IDX_KT), IDX_KT), :]

    def softmax_pv(b, s, kt, vv):
        rs = slice(b * ATT_RB, (b + 1) * ATT_RB)
        bias = (mask_ref[0, 0, kt].astype(F32) - 1.0) * (-NEG_BIG)
        s = s + jnp.concatenate([bias] * (ATT_RB // Q_BLOCK), axis=0)
        m_prev = m_scr[rs, :]
        m_next = jnp.maximum(m_prev, jnp.max(s, axis=1, keepdims=True))
        alpha = jnp.exp(m_prev - m_next)
        pr = jnp.exp(s - jnp.concatenate([m_next] * (IDX_KT // LANES), axis=1))
        l_scr[rs, :] = alpha * l_scr[rs, :] + jnp.sum(pr, axis=1, keepdims=True)
        m_scr[rs, :] = m_next
        acc_scr[rs, :] = (jnp.concatenate([alpha] * (C_LATENT // LANES), axis=1) * acc_scr[rs, :]
                          + _dot(pr.astype(BF16), vv))

    kk0 = ktile(0)
    for b in range(nblk):
        s_scr[b * ATT_RB:(b + 1) * ATT_RB, :] = _dot_nt(qp_scr[b * ATT_RB:(b + 1) * ATT_RB, :], kk0)

    def step(kt, carry):
        kk = ktile(kt)
        kk_next = ktile(kt + 1)
        vv = kk[:, 0:C_LATENT]
        for b in range(nblk):
            rs = slice(b * ATT_RB, (b + 1) * ATT_RB)
            s = s_scr[rs, :]
            s_scr[rs, :] = _dot_nt(qp_scr[rs, :], kk_next)
            softmax_pv(b, s, kt, vv)
        return carry

    last = (i * Q_BLOCK) // IDX_KT
    lax.fori_loop(0, last, step, 0)
    vv = ktile(last)[:, 0:C_LATENT]
    for b in range(nblk):
        softmax_pv(b, s_scr[b * ATT_RB:(b + 1) * ATT_RB, :], last, vv)
    olat = (acc_scr[...] / jnp.concatenate([l_scr[...]] * (C_LATENT // LANES), axis=1)).astype(BF16)
    for p in range(C_HEADS // 2):
        o = (_dot(olat[(2 * p) * Q_BLOCK:(2 * p + 1) * Q_BLOCK], wuv_ref[2 * p])
             + _dot(olat[(2 * p + 1) * Q_BLOCK:(2 * p + 2) * Q_BLOCK], wuv_ref[2 * p + 1]))
        o_ref[0, :, p * LANES:(p + 1) * LANES] = o.astype(o_ref.dtype)


def _dsa_attention(qa, kp, mask, wuk_x, wuv_x, bn, seq):
    nqb = seq // Q_BLOCK
    nkt = seq // IDX_KT
    rows = C_HEADS * Q_BLOCK
    return pl.pallas_call(
        _dsa_attn_body,
        grid=(bn, nqb),
        in_specs=[pl.BlockSpec((1, Q_BLOCK, P1_CKV), lambda b, i: (b, i, 0)),
                  pl.BlockSpec((1, seq, 2 * C_LATENT), lambda b, i: (b, 0, 0)),
                  pl.BlockSpec((1, 1, nkt, Q_BLOCK, IDX_KT), lambda b, i: (b, i, 0, 0, 0)),
                  pl.BlockSpec((C_HEADS, LANES, C_LATENT), lambda b, i: (0, 0, 0)),
                  pl.BlockSpec((C_HEADS, C_LATENT, LANES), lambda b, i: (0, 0, 0))],
        out_specs=pl.BlockSpec((1, Q_BLOCK, C_HEADS * HEAD_DIM), lambda b, i: (b, i, 0)),
        out_shape=jax.ShapeDtypeStruct((bn, seq, C_HEADS * HEAD_DIM), BF16),
        scratch_shapes=[pltpu.VMEM((rows, 2 * C_LATENT), BF16),
                        pltpu.VMEM((rows, LANES), F32), pltpu.VMEM((rows, LANES), F32),
                        pltpu.VMEM((rows, C_LATENT), F32),
                        pltpu.VMEM((rows, IDX_KT), F32)],
        compiler_params=_cparams(("parallel", "arbitrary")),
        name="dsa_attn",
    )(qa.reshape(bn, seq, -1), kp.reshape(bn, seq, -1), mask, wuk_x, wuv_x)


def _mla_weights(w_uk, w_uv):
    wuk = jnp.transpose(w_uk, (1, 2, 0))
    wuk_x = jnp.zeros((C_HEADS, LANES, C_LATENT), F32)
    wuv = jnp.transpose(w_uv, (1, 0, 2))
    wuv_x = jnp.zeros((C_HEADS, C_LATENT, LANES), F32)
    for h in range(C_HEADS):
        o = (h % 2) * HEAD_DIM
        wuk_x = wuk_x.at[h, o:o + C_NOPE, :].set(wuk[h])
        wuv_x = wuv_x.at[h, :, o:o + HEAD_DIM].set(wuv[h])
    return wuk_x.astype(BF16), wuv_x.astype(BF16)


def _router_body(h_ref, g_ref, sh_ref, sc_ref, wr_ref, br_ref, route_ref, up_ref, cnt_ref, carry_scr):
    i = pl.program_id(0)
    tm = h_ref.shape[0]

    @pl.when(i == 0)
    def _():
        carry_scr[...] = jnp.zeros_like(carry_scr)

    u = _norm_mod(h_ref[...], g_ref[...], sh_ref[0], sc_ref[0])
    logits = jnp.dot(u, wr_ref[...], preferred_element_type=F32, precision=HIGHEST) + br_ref[...]
    lane = lax.broadcasted_iota(I32, (tm, LANES), 1).astype(F32)
    m1 = jnp.max(logits, axis=1, keepdims=True)
    e1 = jnp.min(jnp.where(logits == m1, lane, float(LANES)), axis=1, keepdims=True)
    rest = jnp.where(lane == e1, NEG_BIG * 2, logits)
    m2 = jnp.max(rest, axis=1, keepdims=True)
    e2 = jnp.min(jnp.where(rest == m2, lane, float(LANES)), axis=1, keepdims=True)
    ex = jnp.exp(m2 - m1)
    g1 = 1.0 / (1.0 + ex)
    g2 = ex / (1.0 + ex)
    onehot = ((lane == e1) | (lane == e2))
    oh_bf = jnp.where(onehot, 1.0, 0.0).astype(BF16)
    r = lax.broadcasted_iota(I32, (tm, tm), 0)
    c = lax.broadcasted_iota(I32, (tm, tm), 1)
    tri = jnp.where(c < r, 1.0, 0.0).astype(BF16)
    prefix = _dot(tri, oh_bf) + carry_scr[...]
    rank1 = jnp.sum(jnp.where(lane == e1, prefix, 0.0), axis=1, keepdims=True)
    rank2 = jnp.sum(jnp.where(lane == e2, prefix, 0.0), axis=1, keepdims=True)
    carry_scr[...] = carry_scr[...] + jnp.sum(jnp.where(onehot, 1.0, 0.0), axis=0, keepdims=True)
    cnt_ref[...] = carry_scr[...]
    vals = [e1, e2, g1, g2, rank1, rank2]
    route = jnp.zeros((tm, LANES), F32)
    for k, v in enumerate(vals):
        route = jnp.where(lane == float(k), v, route)
    route_ref[...] = route
    up_ref[...] = u


def _router(h2, g, shift, scale, w_router, b_router, seq, tm=512):
    t, d = h2.shape
    tpb = seq // tm
    wr = jnp.pad(w_router.astype(F32), ((0, 0), (0, LANES - N_EXPERTS)))
    br = jnp.pad(b_router.astype(F32), (0, LANES - N_EXPERTS), constant_values=NEG_BIG).reshape(1, LANES)
    mod_spec = pl.BlockSpec((1, 1, d), lambda i: (i // tpb, 0, 0))
    return pl.pallas_call(
        _router_body,
        grid=(t // tm,),
        in_specs=[pl.BlockSpec((tm, d), lambda i: (i, 0)),
                  pl.BlockSpec((1, d), lambda i: (0, 0)),
                  mod_spec, mod_spec,
                  pl.BlockSpec((d, LANES), lambda i: (0, 0)),
                  pl.BlockSpec((1, LANES), lambda i: (0, 0))],
        out_specs=[pl.BlockSpec((tm, LANES), lambda i: (i, 0)),
                   pl.BlockSpec((tm, d), lambda i: (i, 0)),
                   pl.BlockSpec((1, LANES), lambda i: (0, 0))],
        out_shape=[jax.ShapeDtypeStruct((t, LANES), F32),
                   jax.ShapeDtypeStruct((t, d), F32),
                   jax.ShapeDtypeStruct((1, LANES), F32)],
        scratch_shapes=[pltpu.VMEM((1, LANES), F32)],
        compiler_params=_cparams(("arbitrary",)),
        name="moe_router",
    )(h2, g, shift, scale, wr, br)


def _dispatch_body(p1_ref, p2_ref, up_ref, xs_in_ref, xs_ref, sem):
    del xs_in_ref
    tm = up_ref.shape[0]

    def copy(r, dst):
        return pltpu.make_async_copy(up_ref.at[pl.ds(r, 1)], xs_ref.at[pl.ds(dst, 1)], sem)

    def start(r, c):
        copy(r, p1_ref[r]).start(priority=0)
        copy(r, p2_ref[r]).start(priority=1)
        return c

    lax.fori_loop(0, tm, start, 0)

    def wait(r, c):
        copy(r, p1_ref[r]).wait()
        copy(r, p2_ref[r]).wait()
        return c

    lax.fori_loop(0, tm, wait, 0)


def _dispatch(pos1, pos2, up, nrows, tm=256):
    t, w = up.shape
    xs0 = jnp.zeros((nrows, w), F32)
    smem_spec = pl.BlockSpec((tm,), lambda i: (i,), memory_space=pltpu.SMEM)
    return pl.pallas_call(
        _dispatch_body,
        grid=(t // tm,),
        in_specs=[smem_spec, smem_spec,
                  pl.BlockSpec((tm, w), lambda i: (i, 0)),
                  pl.BlockSpec(memory_space=pl.ANY)],
        out_specs=pl.BlockSpec(memory_space=pl.ANY),
        out_shape=jax.ShapeDtypeStruct((nrows, w), F32),
        scratch_shapes=[pltpu.SemaphoreType.DMA(())],
        input_output_aliases={3: 0},
        compiler_params=_cparams(("arbitrary",)),
        name="moe_dispatch",
    )(pos1, pos2, up, xs0)


def _expert_body(te_ref, nv_ref, xs_ref, wg_ref, wu_ref, wd_ref, y_ref, x_scr, acc_scr):
    j = pl.program_id(0)
    f = pl.program_id(1)
    live = j < nv_ref[0]

    @pl.when(f == 0)
    def _():
        x_scr[...] = xs_ref[...].astype(BF16)
        acc_scr[...] = jnp.zeros_like(acc_scr)

    @pl.when(live)
    def _():
        x = x_scr[...]
        gp = _dot(x, wg_ref[0, 0])
        up = _dot(x, wu_ref[0, 0])
        a = (gp * jax.nn.sigmoid(gp) * up).astype(BF16)
        acc_scr[...] += _dot(a, wd_ref[0])

    @pl.when(f == pl.num_programs(1) - 1)
    def _():
        y_ref[...] = acc_scr[...]


def _experts(tile_expert, n_live, xs, wg_t, wu_t, wd, tm):
    nrows, d = xs.shape
    _, nf, _, tf = wg_t.shape
    nt = nrows // tm

    def f_eff(j, f, nv):
        return jnp.where(j < nv[0], f, nf - 1)

    grid_spec = pltpu.PrefetchScalarGridSpec(
        num_scalar_prefetch=2,
        grid=(nt, nf),
        in_specs=[pl.BlockSpec((tm, d), lambda j, f, te, nv: (j, 0)),
                  pl.BlockSpec((1, 1, d, tf), lambda j, f, te, nv: (te[j], f_eff(j, f, nv), 0, 0)),
                  pl.BlockSpec((1, 1, d, tf), lambda j, f, te, nv: (te[j], f_eff(j, f, nv), 0, 0)),
                  pl.BlockSpec((1, tf, d), lambda j, f, te, nv: (te[j], f_eff(j, f, nv), 0))],
        out_specs=pl.BlockSpec((tm, d), lambda j, f, te, nv: (j, 0)),
        scratch_shapes=[pltpu.VMEM((tm, d), BF16), pltpu.VMEM((tm, d), F32)])
    return pl.pallas_call(
        _expert_body,
        grid_spec=grid_spec,
        out_shape=jax.ShapeDtypeStruct((nrows, d), F32),
        compiler_params=_cparams(("arbitrary", "arbitrary")),
        name="moe_experts",
    )(tile_expert, n_live, xs, wg_t, wu_t, wd)


def _combine_body(p1_ref, p2_ref, route_ref, h_ref, gt_ref, gf_ref, y_ref, o_ref, y1_scr, y2_scr, sem):
    tm = h_ref.shape[0]

    def copy(src, dst_scr, r):
        return pltpu.make_async_copy(y_ref.at[pl.ds(src, 1)], dst_scr.at[pl.ds(r, 1)], sem)

    def start(r, c):
        copy(p1_ref[r], y1_scr, r).start(priority=0)
        copy(p2_ref[r], y2_scr, r).start(priority=1)
        return c

    lax.fori_loop(0, tm, start, 0)

    def wait(r, c):
        copy(p1_ref[r], y1_scr, r).wait()
        copy(p2_ref[r], y2_scr, r).wait()
        return c

    lax.fori_loop(0, tm, wait, 0)
    route = route_ref[...]
    g1, g2 = route[:, 2:3], route[:, 3:4]
    y = g1 * y1_scr[...] + g2 * y2_scr[...]
    hn = h_ref[...] + gt_ref[0] * y
    ms = jnp.mean(hn * hn, axis=-1, keepdims=True)
    o_ref[...] = hn * lax.rsqrt(ms + NORM_EPS) * gf_ref[...]


def _combine(pos1, pos2, route, h2, gate, g_final, y_sorted, seq, tm=256):
    t, d = h2.shape
    tpb = seq // tm
    smem_spec = pl.BlockSpec((tm,), lambda i: (i,), memory_space=pltpu.SMEM)
    return pl.pallas_call(
        _combine_body,
        grid=(t // tm,),
        in_specs=[smem_spec, smem_spec,
                  pl.BlockSpec((tm, LANES), lambda i: (i, 0)),
                  pl.BlockSpec((tm, d), lambda i: (i, 0)),
                  pl.BlockSpec((1, 1, d), lambda i: (i // tpb, 0, 0)),
                  pl.BlockSpec((1, d), lambda i: (0, 0)),
                  pl.BlockSpec(memory_space=pl.ANY)],
        out_specs=pl.BlockSpec((tm, d), lambda i: (i, 0)),
        out_shape=jax.ShapeDtypeStruct((t, d), F32),
        scratch_shapes=[pltpu.VMEM((tm, d), F32), pltpu.VMEM((tm, d), F32),
                        pltpu.SemaphoreType.DMA(())],
        compiler_params=_cparams(("arbitrary",)),
        name="moe_combine",
    )(pos1, pos2, route, h2, gate, g_final, y_sorted)


MOE_TILE = 512
MOE_TF = 512
FFN_TF = 256


def _moe_layout(route, counts):
    e1 = route[:, 0].astype(I32)
    e2 = route[:, 1].astype(I32)
    r1 = route[:, 4].astype(I32)
    r2 = route[:, 5].astype(I32)
    cnt = counts[0, :N_EXPERTS].astype(I32)
    tiles = (cnt + MOE_TILE - 1) // MOE_TILE
    tile_end = jnp.cumsum(tiles)
    start = (tile_end - tiles) * MOE_TILE
    pos1 = start[e1] + r1
    pos2 = start[e2] + r2
    nt = route.shape[0] * 2 // MOE_TILE + N_EXPERTS
    n_live = tile_end[-1]
    tid = jnp.minimum(jnp.arange(nt, dtype=I32), n_live - 1)
    tile_expert = jnp.sum((tid[:, None] >= tile_end[None, :]).astype(I32), axis=1)
    return pos1, pos2, tile_expert.astype(I32), n_live.reshape(1).astype(I32), nt * MOE_TILE


def kernel(x, c, w_ada, b_ada, g_mix, g_ffn, g_final, e_w_in, e_w_out, s5_a_re, s5_a_im, s5_log_dt,
           s5_b_re, s5_b_im, s5_c_re, s5_c_im, s5_d, s5_w_glu, s5_b_glu, ff_w_gate, ff_w_up,
           ff_w_down, o_w_in, o_w_out, mla_g_kv, mla_w_uk, mla_w_uv, moe_w_router, moe_b_router,
           moe_w_gate, moe_w_up, moe_w_down):
    bn, seq, d = x.shape
    t = bn * seq
    mod = _adaln(c, w_ada, b_ada)

    def mod_vec(layer, k):
        return mod[layer, :, k * d:(k + 1) * d].reshape(bn, 1, d)

    tabs64 = _rope_tables(seq, HEAD_DIM)
    tabs16 = _rope_tables(seq, ROPE_DIM)
    h = x.reshape(t, d)

    *qkv_views, s_in = _proj0(h, g_mix[0].reshape(1, d), mod_vec(0, 0), mod_vec(0, 1),
                              e_w_in[0].astype(BF16), tabs64, seq)
    y_a = _dilated_attention(qkv_views, bn, seq)
    prm = _s5_prepare(s5_a_re[0], s5_a_im[0], s5_log_dt[0], s5_b_re[0], s5_b_im[0],
                      s5_c_re[0], s5_c_im[0], s5_d[0], seq // S5_CHUNK)
    y_b = _s5_mixer(s_in, prm, s5_w_glu[0].astype(BF16), s5_b_glu[0], bn, seq)
    h = _outproj([y_a], [y_b], e_w_out[0].astype(BF16), h, mod_vec(0, 2), seq)
    h = _ffn_dense(h, g_ffn[0].reshape(1, d), mod_vec(0, 3), mod_vec(0, 4), mod_vec(0, 5),
                   _tile_cols(ff_w_gate[0].astype(BF16), FFN_TF), _tile_cols(ff_w_up[0].astype(BF16), FFN_TF),
                   ff_w_down[0].astype(BF16), seq)

    w1 = _proj1_layout(o_w_in[0]).astype(BF16)
    qa, kp, qi, ki, wi = _proj1(h, g_mix[1].reshape(1, d), mod_vec(1, 0), mod_vec(1, 1), w1,
                                mla_g_kv[0].reshape(1, C_LATENT).astype(F32), tabs16, tabs64, seq)
    mask = _dsa_select(qi, ki, wi, bn, seq)
    wuk_x, wuv_x = _mla_weights(mla_w_uk[0], mla_w_uv[0])
    o_attn = _dsa_attention(qa, kp, mask, wuk_x, wuv_x, bn, seq)
    h = _outproj([o_attn.reshape(t, C_HEADS * HEAD_DIM)], [], o_w_out[0].astype(BF16), h, mod_vec(1, 2), seq)

    route, up, counts = _router(h, g_ffn[1].reshape(1, d), mod_vec(1, 3), mod_vec(1, 4),
                                moe_w_router[0], moe_b_router[0], seq)
    pos1, pos2, tile_expert, n_live, nrows = _moe_layout(route, counts)
    xs = _dispatch(pos1, pos2, up, nrows)
    y_sorted = _experts(tile_expert, n_live, xs, _tile_cols(moe_w_gate[0].astype(BF16), MOE_TF),
                        _tile_cols(moe_w_up[0].astype(BF16), MOE_TF), moe_w_down[0].astype(BF16), MOE_TILE)
    out = _combine(pos1, pos2, route, h, mod_vec(1, 5), g_final.reshape(1, d), y_sorted, seq)
    return out.reshape(bn, seq, d)
```

```python
import functools
import math

import jax
import jax.numpy as jnp
import numpy as np
from jax import lax
from jax.experimental import pallas as pl
from jax.experimental.pallas import tpu as pltpu

F32 = jnp.float32
BF16 = jnp.bfloat16
I32 = jnp.int32
HIGHEST = lax.Precision.HIGHEST

D_MODEL = 1024
HEAD_DIM = 64
ROPE_DIM = 16
ROPE_THETA = 500000.0
NORM_EPS = 1e-6
A_HEADS = 8
A_WIDTH = 512
A_PATTERNS = ((128, 1), (512, 4), (2048, 16))
B_WIDTH = 512
B_GROUP = 16
B_GROUPS = 32
B_STATE = 64
C_HEADS = 16
C_NOPE = 48
C_LATENT = 256
IDX_HEADS = 8
IDX_DIM = 64
TOPK_MAX = 256
FF_DENSE = 2816
N_EXPERTS = 8
FF_EXPERT = 3584

LANES = 128
Q_BLOCK = 128
S5_CHUNK = 16
MOE_TF = 512
FFN_TF = 256
DMA_UNROLL = 8
VMEM_LIMIT = 56 * 1024 * 1024
NEG_BIG = -1e30
INT_MIN = -(2 ** 31)


def _cparams(sem, vmem=VMEM_LIMIT):
    return pltpu.CompilerParams(dimension_semantics=sem, vmem_limit_bytes=vmem)


def _dot(a, b):
    return jnp.dot(a, b, preferred_element_type=F32)


def _dot_nt(a, b):
    return lax.dot_general(a, b, (((1,), (1,)), ((), ())), preferred_element_type=F32)


def _norm_mod(x, g, shift, scale):
    ms = jnp.mean(x * x, axis=-1, keepdims=True)
    y = x * lax.rsqrt(ms + NORM_EPS) * g
    return y * (1.0 + scale) + shift


def _rope128(x, ct, sa, sb):
    return x * ct + pltpu.roll(x, LANES - ROPE_DIM // 2, 1) * sa + pltpu.roll(x, ROPE_DIM // 2, 1) * sb


def _adaln_body(c_ref, w_ref, b_ref, o_ref):
    c = c_ref[...]
    ca = c * jax.nn.sigmoid(c)
    o_ref[0] = jnp.dot(ca, w_ref[0], preferred_element_type=F32, precision=HIGHEST) + b_ref[0]


def _adaln(c, w_ada, b_ada):
    depth, d, d6 = w_ada.shape
    bn = c.shape[0]
    rows = 8
    cp = jnp.zeros((rows, d), F32).at[:bn].set(c)
    tn = 1536
    out = pl.pallas_call(
        _adaln_body,
        grid=(depth, d6 // tn),
        in_specs=[pl.BlockSpec((rows, d), lambda l, j: (0, 0)),
                  pl.BlockSpec((1, d, tn), lambda l, j: (l, 0, j)),
                  pl.BlockSpec((1, 1, tn), lambda l, j: (l, 0, j))],
        out_specs=pl.BlockSpec((1, rows, tn), lambda l, j: (l, 0, j)),
        out_shape=jax.ShapeDtypeStruct((depth, rows, d6), F32),
        compiler_params=_cparams(("arbitrary", "arbitrary")),
        name="adaln",
    )(cp, w_ada, b_ada.reshape(depth, 1, d6))
    return out[:, :bn]


def _rope_tables(seq, period):
    half = ROPE_DIM // 2
    pos = jnp.arange(seq, dtype=F32)
    inv = ROPE_THETA ** (-jnp.arange(0, ROPE_DIM, 2, dtype=F32) / ROPE_DIM)
    ang = pos[:, None] * inv[None, :]
    cos, sin = jnp.cos(ang), jnp.sin(ang)
    lane = np.arange(LANES) % period
    first = lane < half
    second = (lane >= half) & (lane < ROPE_DIM)
    idx = np.where(first, lane, np.where(second, lane - half, 0))
    cos_l, sin_l = cos[:, idx], sin[:, idx]
    ct = jnp.where(first | second, cos_l, 1.0)
    sa = jnp.where(first, -sin_l, 0.0)
    sb = jnp.where(second, sin_l, 0.0)
    return ct, sa, sb


def _proj0_body(x_ref, g_ref, sh_ref, sc_ref, w_ref, ct_ref, sa_ref, sb_ref, *rest):
    qkv_refs, s_ref, acc_scr = rest[:-2], rest[-2], rest[-1]
    tm = x_ref.shape[0]
    qw = 3 * A_WIDTH
    u = _norm_mod(x_ref[...], g_ref[...], sh_ref[0], sc_ref[0]).astype(BF16)
    ct, sa, sb = ct_ref[...], sa_ref[...], sb_ref[...]
    for j in range(3):
        acc = _dot(u, w_ref[:, j * A_WIDTH:(j + 1) * A_WIDTH])
        for c in range(A_WIDTH // LANES):
            a = acc[:, c * LANES:(c + 1) * LANES]
            if j < 2:
                a = _rope128(a, ct, sa, sb)
            if j == 0:
                a = a * (HEAD_DIM ** -0.5)
            acc_scr[j * (A_WIDTH // LANES) + c] = a
    acc = _dot(u, w_ref[:, qw:])
    nq = qw // LANES
    for c in range(B_WIDTH // LANES):
        acc_scr[nq + c] = acc[:, c * LANES:(c + 1) * LANES]
    for (_, dil), ref in zip(A_PATTERNS, qkv_refs):
        for r in range(dil):
            for c in range(nq):
                ref[:, r * qw + c * LANES:r * qw + (c + 1) * LANES] = (
                    acc_scr[c, pl.ds(r, tm // dil, stride=dil), :].astype(BF16))
    for r in range(S5_CHUNK):
        for c in range(B_WIDTH // LANES):
            s_ref[:, r * B_WIDTH + c * LANES:r * B_WIDTH + (c + 1) * LANES] = (
                acc_scr[nq + c, pl.ds(r, tm // S5_CHUNK, stride=S5_CHUNK), :])


def _proj0(x2, g, shift, scale, w_bf, tabs, seq, tm=512):
    t, d = x2.shape
    n = w_bf.shape[1]
    tpb = seq // tm
    ct, sa, sb = tabs
    tab_spec = pl.BlockSpec((tm, LANES), lambda i: (i % tpb, 0))
    mod_spec = pl.BlockSpec((1, 1, d), lambda i: (i // tpb, 0, 0))
    qw = 3 * A_WIDTH
    dils = [dil for _, dil in A_PATTERNS]
    out_specs = [pl.BlockSpec((tm // dil, dil * qw), lambda i: (i, 0)) for dil in dils]
    out_shape = [jax.ShapeDtypeStruct((t // dil, dil * qw), BF16) for dil in dils]
    out_specs.append(pl.BlockSpec((tm // S5_CHUNK, S5_CHUNK * B_WIDTH), lambda i: (i, 0)))
    out_shape.append(jax.ShapeDtypeStruct((t // S5_CHUNK, S5_CHUNK * B_WIDTH), F32))
    return pl.pallas_call(
        _proj0_body,
        grid=(t // tm,),
        in_specs=[pl.BlockSpec((tm, d), lambda i: (i, 0)),
                  pl.BlockSpec((1, d), lambda i: (0, 0)),
                  mod_spec, mod_spec,
                  pl.BlockSpec((d, n), lambda i: (0, 0)),
                  tab_spec, tab_spec, tab_spec],
        out_specs=out_specs,
        out_shape=out_shape,
        scratch_shapes=[pltpu.VMEM((n // LANES, tm, LANES), F32)],
        compiler_params=_cparams(("parallel",)),
        name="proj0",
    )(x2, g, shift, scale, w_bf, ct, sa, sb)


def _dil_body(q_ref, kp_ref, kc_ref, vp_ref, vc_ref, lse_ref, o_ref):
    i = pl.program_id(2)
    q = q_ref[0]
    kcat = jnp.concatenate([kp_ref[0], kc_ref[0]], axis=0)
    vcat = jnp.concatenate([vp_ref[0], vc_ref[0]], axis=0)
    row = lax.broadcasted_iota(I32, (Q_BLOCK, 2 * Q_BLOCK), 0)
    col = lax.broadcasted_iota(I32, (Q_BLOCK, 2 * Q_BLOCK), 1)
    rel = row + Q_BLOCK - col
    valid = (rel >= 0) & (rel <= Q_BLOCK) & ((col >= Q_BLOCK) | (i > 0))
    lane = lax.broadcasted_iota(I32, (1, LANES), 1)
    npair = A_WIDTH // LANES
    hpp = LANES // HEAD_DIM
    scores = []
    for p in range(npair):
        sl = slice(p * LANES, (p + 1) * LANES)
        for hh in range(hpp):
            hm = (lane // HEAD_DIM) == hh
            qh = jnp.where(hm, q[:, sl], jnp.zeros_like(q[:, sl]))
            scores.append(jnp.where(valid, _dot_nt(qh, kcat[:, sl]), -jnp.inf))
    s = jnp.concatenate(scores, axis=0)
    m = jnp.max(s, axis=1, keepdims=True)
    pr = jnp.exp(s - m)
    l = jnp.sum(pr, axis=1, keepdims=True)
    prb = pr.astype(BF16)
    for p in range(npair):
        sl = slice(p * LANES, (p + 1) * LANES)
        m_pair = l_pair = o_pair = None
        for hh in range(hpp):
            h = p * hpp + hh
            rs = slice(h * Q_BLOCK, (h + 1) * Q_BLOCK)
            hm = (lane // HEAD_DIM) == hh
            o = _dot(prb[rs], vcat[:, sl])
            if hh == 0:
                m_pair = jnp.broadcast_to(m[rs], (Q_BLOCK, LANES))
                l_pair = jnp.broadcast_to(l[rs], (Q_BLOCK, LANES))
                o_pair = o
            else:
                m_pair = jnp.where(hm, m[rs], m_pair)
                l_pair = jnp.where(hm, l[rs], l_pair)
                o_pair = jnp.where(hm, o, o_pair)
        lse_ref[0, :, sl] = m_pair + jnp.log(l_pair)
        o_ref[0, :, sl] = (o_pair / l_pair).astype(o_ref.dtype)


def _dilated_branch(qkv_d, dil, bn, seq):
    sd = seq // dil
    nb = sd // Q_BLOCK
    blk = (1, Q_BLOCK, A_WIDTH)
    q_spec = pl.BlockSpec(blk, lambda b, r, i: (b, i, r * 3))
    kp_spec = pl.BlockSpec(blk, lambda b, r, i: (b, jnp.maximum(i - 1, 0), r * 3 + 1))
    kc_spec = pl.BlockSpec(blk, lambda b, r, i: (b, i, r * 3 + 1))
    vp_spec = pl.BlockSpec(blk, lambda b, r, i: (b, jnp.maximum(i - 1, 0), r * 3 + 2))
    vc_spec = pl.BlockSpec(blk, lambda b, r, i: (b, i, r * 3 + 2))
    st_spec = pl.BlockSpec(blk, lambda b, r, i: (b, i, r))
    lse, out = pl.pallas_call(
        _dil_body,
        grid=(bn, dil, nb),
        in_specs=[q_spec, kp_spec, kc_spec, vp_spec, vc_spec],
        out_specs=[st_spec] * 2,
        out_shape=[jax.ShapeDtypeStruct((bn, sd, dil * A_WIDTH), F32),
                   jax.ShapeDtypeStruct((bn, sd, dil * A_WIDTH), BF16)],
        compiler_params=_cparams(("parallel", "parallel", "arbitrary")),
        name=f"dilated_d{dil}",
    )(*([qkv_d.reshape(bn, sd, dil * 3 * A_WIDTH)] * 5))
    return [lse.reshape(bn * sd, dil * A_WIDTH), out.reshape(bn * sd, dil * A_WIDTH)]


def _dil_merge_body(*refs):
    nbr = len(A_PATTERNS)
    stats = refs[:2 * nbr]
    y_ref = refs[2 * nbr]
    scr = refs[2 * nbr + 1:]
    tm = y_ref.shape[0]
    nc = A_WIDTH // LANES
    lses, outs = [], []
    si = 0
    for gi, (_, dil) in enumerate(A_PATTERNS):
        pair = []
        for ref in stats[2 * gi:2 * gi + 2]:
            if dil == 1:
                pair.append(ref[...].astype(F32))
            else:
                for r in range(dil):
                    for c in range(nc):
                        scr[si][c, pl.ds(r, tm // dil, stride=dil), :] = (
                            ref[:, r * A_WIDTH + c * LANES:r * A_WIDTH + (c + 1) * LANES].astype(F32))
                pair.append(jnp.concatenate([scr[si][c] for c in range(nc)], axis=1))
                si += 1
        lses.append(pair[0])
        outs.append(pair[1])
    top = functools.reduce(jnp.maximum, lses)
    den = jnp.zeros_like(top)
    num = jnp.zeros_like(top)
    for lse, o in zip(lses, outs):
        w = jnp.exp(lse - top)
        den = den + w
        num = num + w * o
    y_ref[...] = (num / den).astype(y_ref.dtype)


def _dilated_attention(qkv_views, bn, seq, tm=512):
    t = bn * seq
    stats, in_specs = [], []
    nscr = 0
    for (_, dil), qkv_d in zip(A_PATTERNS, qkv_views):
        stats += _dilated_branch(qkv_d, dil, bn, seq)
        in_specs += [pl.BlockSpec((tm // dil, dil * A_WIDTH), lambda i: (i, 0))] * 2
        nscr += 2 if dil > 1 else 0
    return pl.pallas_call(
        _dil_merge_body,
        grid=(t // tm,),
        in_specs=in_specs,
        out_specs=pl.BlockSpec((tm, A_WIDTH), lambda i: (i, 0)),
        out_shape=jax.ShapeDtypeStruct((t, A_WIDTH), BF16),
        scratch_shapes=[pltpu.VMEM((A_WIDTH // LANES, tm, LANES), F32)] * nscr,
        compiler_params=_cparams(("parallel",)),
        name="dilated_merge",
    )(*stats)


def _s5_prepare(a_re, a_im, log_dt, b_re, b_im, c_re, c_im, d_skip, nchunk):
    dt = jnp.exp(log_dt.astype(F32))[:, None]
    lr, li = a_re.astype(F32), a_im.astype(F32)
    mag = jnp.exp(lr * dt)
    ar = mag * jnp.cos(li * dt)
    ai = mag * jnp.sin(li * dt)
    den = lr * lr + li * li
    nr = ar - 1.0
    fr = (nr * lr + ai * li) / den
    fi = (ai * lr - nr * li) / den
    br, bi = b_re.astype(F32), b_im.astype(F32)
    bbr = fr[..., None] * br - fi[..., None] * bi
    bbi = fr[..., None] * bi + fi[..., None] * br
    nblk = B_WIDTH // LANES
    gpb = B_GROUPS // nblk
    eye = jnp.eye(gpb, dtype=F32)

    def bdiag_in(m):
        m = jnp.swapaxes(m.reshape(nblk, gpb, B_STATE, B_GROUP), 2, 3)
        m = m[:, :, :, None, :] * eye[None, :, None, :, None]
        return m.reshape(nblk, gpb * B_GROUP, gpb * B_STATE)

    def bdiag_out(m):
        m = jnp.swapaxes(m.reshape(nblk, gpb, B_GROUP, B_STATE), 2, 3)
        m = m[:, :, :, None, :] * eye[None, :, None, :, None]
        return m.reshape(nblk, gpb * B_STATE, gpb * B_GROUP)

    pr, pi = ar, ai
    for _ in range(int(math.log2(S5_CHUNK))):
        pr, pi = pr * pr - pi * pi, 2.0 * pr * pi
    pows_r, pows_i = [], []
    for _ in range(max(1, int(math.ceil(math.log2(nchunk))))):
        pows_r.append(pr.reshape(1, -1))
        pows_i.append(pi.reshape(1, -1))
        pr, pi = pr * pr - pi * pi, 2.0 * pr * pi
    return dict(
        ar=ar.reshape(1, -1), ai=ai.reshape(1, -1),
        b_re=bdiag_in(bbr).astype(BF16), b_im=bdiag_in(bbi).astype(BF16),
        c_re=bdiag_out(c_re.astype(F32)).astype(BF16), c_im=bdiag_out(-c_im.astype(F32)).astype(BF16),
        d=d_skip.astype(F32).reshape(1, B_WIDTH),
        pows_r=jnp.concatenate(pows_r, axis=0), pows_i=jnp.concatenate(pows_i, axis=0))


def _s5_local_body(u_ref, bre_ref, bim_ref, ar_ref, ai_ref, sre_ref, sim_ref):
    tr = u_ref.shape[0]
    sw = bre_ref.shape[2]
    for blk in range(B_WIDTH // LANES):
        arb = ar_ref[:, blk * sw:(blk + 1) * sw]
        aib = ai_ref[:, blk * sw:(blk + 1) * sw]
        sr = jnp.zeros((tr, sw), F32)
        si = jnp.zeros((tr, sw), F32)
        for t in range(S5_CHUNK):
            ub = u_ref[:, t * B_WIDTH + blk * LANES:t * B_WIDTH + (blk + 1) * LANES].astype(BF16)
            sr, si = (arb * sr - aib * si + _dot(ub, bre_ref[blk]),
                      arb * si + aib * sr + _dot(ub, bim_ref[blk]))
        sre_ref[:, blk * sw:(blk + 1) * sw] = sr
        sim_ref[:, blk * sw:(blk + 1) * sw] = si


def _s5_scan_body(nsteps, sre_ref, sim_ref, pr_ref, pi_ref, xre_ref, xim_ref):
    nc = sre_ref.shape[0]
    sw = 512
    row = lax.broadcasted_iota(I32, (nc, 1), 0)
    for cb in range(sre_ref.shape[1] // sw):
        sl = slice(cb * sw, (cb + 1) * sw)
        xr, xi = sre_ref[:, sl], sim_ref[:, sl]
        for s in range(nsteps):
            sh = 1 << s
            ok = row >= sh
            rr = jnp.where(ok, pltpu.roll(xr, sh, 0), 0.0)
            ri = jnp.where(ok, pltpu.roll(xi, sh, 0), 0.0)
            pr, pi = pr_ref[s:s + 1, sl], pi_ref[s:s + 1, sl]
            xr, xi = xr + pr * rr - pi * ri, xi + pr * ri + pi * rr
        ok = row >= 1
        xre_ref[:, sl] = jnp.where(ok, pltpu.roll(xr, 1, 0), 0.0)
        xim_ref[:, sl] = jnp.where(ok, pltpu.roll(xi, 1, 0), 0.0)


def _s5_out_body(u_ref, xre_ref, xim_ref, bre_ref, bim_ref, cre_ref, cim_ref, ar_ref, ai_ref,
                 d_ref, wglu_ref, bglu_ref, out_ref, y_scr):
    sw = bre_ref.shape[2]
    for blk in range(B_WIDTH // LANES):
        arb = ar_ref[:, blk * sw:(blk + 1) * sw]
        aib = ai_ref[:, blk * sw:(blk + 1) * sw]
        xr = xre_ref[:, blk * sw:(blk + 1) * sw]
        xi = xim_ref[:, blk * sw:(blk + 1) * sw]
        db = d_ref[:, blk * LANES:(blk + 1) * LANES]
        for t in range(S5_CHUNK):
            cs = slice(t * B_WIDTH + blk * LANES, t * B_WIDTH + (blk + 1) * LANES)
            uf = u_ref[:, cs]
            ub = uf.astype(BF16)
            xr, xi = (arb * xr - aib * xi + _dot(ub, bre_ref[blk]),
                      arb * xi + aib * xr + _dot(ub, bim_ref[blk]))
            y = _dot(xr.astype(BF16), cre_ref[blk]) + _dot(xi.astype(BF16), cim_ref[blk]) + db * uf
            y_scr[:, cs] = jax.nn.gelu(y, approximate=True)
    for t in range(S5_CHUNK):
        cs = slice(t * B_WIDTH, (t + 1) * B_WIDTH)
        y = y_scr[:, cs]
        z = _dot(y.astype(BF16), wglu_ref[...]) + bglu_ref[...]
        out_ref[:, cs] = (y * jax.nn.sigmoid(z)).astype(out_ref.dtype)


def _s5_mixer(s_in, prm, w_glu_bf, b_glu, bn, seq, tr=128):
    t = bn * seq
    nrow = t // S5_CHUNK
    ncb = seq // S5_CHUNK
    width = S5_CHUNK * B_WIDTH
    sdim = B_GROUPS * B_STATE
    uc = s_in.reshape(nrow, width)
    nblk = B_WIDTH // LANES
    const3 = lambda i: (0, 0, 0)
    const2 = lambda i: (0, 0)
    b_spec = pl.BlockSpec((nblk, LANES, sdim // nblk), const3)
    c_spec = pl.BlockSpec((nblk, sdim // nblk, LANES), const3)
    a_spec = pl.BlockSpec((1, sdim), const2)
    s_re, s_im = pl.pallas_call(
        _s5_local_body,
        grid=(nrow // tr,),
        in_specs=[pl.BlockSpec((tr, width), lambda i: (i, 0)), b_spec, b_spec, a_spec, a_spec],
        out_specs=[pl.BlockSpec((tr, sdim), lambda i: (i, 0))] * 2,
        out_shape=[jax.ShapeDtypeStruct((nrow, sdim), F32)] * 2,
        compiler_params=_cparams(("parallel",)),
        name="s5_local",
    )(uc, prm['b_re'], prm['b_im'], prm['ar'], prm['ai'])
    nsteps = prm['pows_r'].shape[0]
    x_re, x_im = pl.pallas_call(
        functools.partial(_s5_scan_body, nsteps),
        grid=(bn,),
        in_specs=[pl.BlockSpec((ncb, sdim), lambda b: (b, 0))] * 2
                 + [pl.BlockSpec((nsteps, sdim), lambda b: (0, 0))] * 2,
        out_specs=[pl.BlockSpec((ncb, sdim), lambda b: (b, 0))] * 2,
        out_shape=[jax.ShapeDtypeStruct((nrow, sdim), F32)] * 2,
        compiler_params=_cparams(("parallel",)),
        name="s5_scan",
    )(s_re, s_im, prm['pows_r'], prm['pows_i'])
    y = pl.pallas_call(
        _s5_out_body,
        grid=(nrow // tr,),
        in_specs=[pl.BlockSpec((tr, width), lambda i: (i, 0)),
                  pl.BlockSpec((tr, sdim), lambda i: (i, 0)),
                  pl.BlockSpec((tr, sdim), lambda i: (i, 0)),
                  b_spec, b_spec, c_spec, c_spec, a_spec, a_spec,
                  pl.BlockSpec((1, B_WIDTH), const2),
                  pl.BlockSpec((B_WIDTH, B_WIDTH), const2),
                  pl.BlockSpec((1, B_WIDTH), const2)],
        out_specs=pl.BlockSpec((tr, width), lambda i: (i, 0)),
        out_shape=jax.ShapeDtypeStruct((nrow, width), BF16),
        scratch_shapes=[pltpu.VMEM((tr, width), F32)],
        compiler_params=_cparams(("parallel",)),
        name="s5_out",
    )(uc, x_re, x_im, prm['b_re'], prm['b_im'], prm['c_re'], prm['c_im'], prm['ar'], prm['ai'],
      prm['d'], w_glu_bf, b_glu.reshape(1, B_WIDTH).astype(F32))
    return y


def _outproj_body(nparts, nflat, *refs):
    parts = refs[:nparts]
    flats = refs[nparts:nparts + nflat]
    w_ref, h_ref, gt_ref, o_ref = refs[nparts + nflat:nparts + nflat + 4]
    scr = refs[nparts + nflat + 4:]
    tm = h_ref.shape[0]
    acc = None
    off = 0
    for p in parts:
        k = p.shape[1]
        d = _dot(p[...].astype(BF16), w_ref[off:off + k, :])
        acc = d if acc is None else acc + d
        off += k
    for p, s in zip(flats, scr):
        nc = s.shape[0]
        k = nc * LANES
        for r in range(S5_CHUNK):
            for c in range(nc):
                s[c, pl.ds(r, tm // S5_CHUNK, stride=S5_CHUNK), :] = (
                    p[:, r * k + c * LANES:r * k + (c + 1) * LANES].astype(F32))
        rows = jnp.concatenate([s[c] for c in range(nc)], axis=1)
        d = _dot(rows.astype(BF16), w_ref[off:off + k, :])
        acc = d if acc is None else acc + d
        off += k
    o_ref[...] = h_ref[...] + gt_ref[0] * acc


def _outproj(parts, flat_parts, w_bf, h2, gate, seq, tm=512):
    t, d = h2.shape
    tpb = seq // tm
    in_specs = [pl.BlockSpec((tm, p.shape[1]), lambda i: (i, 0)) for p in parts]
    in_specs += [pl.BlockSpec((tm // S5_CHUNK, p.shape[1]), lambda i: (i, 0)) for p in flat_parts]
    in_specs += [pl.BlockSpec(w_bf.shape, lambda i: (0, 0)),
                 pl.BlockSpec((tm, d), lambda i: (i, 0)),
                 pl.BlockSpec((1, 1, d), lambda i: (i // tpb, 0, 0))]
    return pl.pallas_call(
        functools.partial(_outproj_body, len(parts), len(flat_parts)),
        grid=(t // tm,),
        in_specs=in_specs,
        out_specs=pl.BlockSpec((tm, d), lambda i: (i, 0)),
        out_shape=jax.ShapeDtypeStruct((t, d), F32),
        scratch_shapes=[pltpu.VMEM((p.shape[1] // S5_CHUNK // LANES, tm, LANES), F32) for p in flat_parts],
        compiler_params=_cparams(("parallel",)),
        name="outproj",
    )(*parts, *flat_parts, w_bf, h2, gate)


def _ffn_body(h_ref, g_ref, sh_ref, sc_ref, gt_ref, wg_ref, wu_ref, wd_ref, o_ref, u_scr, acc_scr):
    f = pl.program_id(1)

    @pl.when(f == 0)
    def _():
        u_scr[...] = _norm_mod(h_ref[...], g_ref[...], sh_ref[0], sc_ref[0]).astype(BF16)
        acc_scr[...] = jnp.zeros_like(acc_scr)

    u = u_scr[...]
    gp = _dot(u, wg_ref[...])
    up = _dot(u, wu_ref[...])
    a = (gp * jax.nn.sigmoid(gp) * up).astype(BF16)
    acc_scr[...] += _dot(a, wd_ref[...])

    @pl.when(f == pl.num_programs(1) - 1)
    def _():
        o_ref[...] = h_ref[...] + gt_ref[0] * acc_scr[...]


def _ffn_dense(h2, g, shift, scale, gate, wg, wu, wd, seq, tm=1024, tf=FFN_TF):
    t, d = h2.shape
    ff = wg.shape[1]
    tpb = seq // tm
    mod_spec = pl.BlockSpec((1, 1, d), lambda i, f: (i // tpb, 0, 0))
    return pl.pallas_call(
        _ffn_body,
        grid=(t // tm, ff // tf),
        in_specs=[pl.BlockSpec((tm, d), lambda i, f: (i, 0)),
                  pl.BlockSpec((1, d), lambda i, f: (0, 0)),
                  mod_spec, mod_spec, mod_spec,
                  pl.BlockSpec((d, tf), lambda i, f: (0, f)),
                  pl.BlockSpec((d, tf), lambda i, f: (0, f)),
                  pl.BlockSpec((tf, d), lambda i, f: (f, 0))],
        out_specs=pl.BlockSpec((tm, d), lambda i, f: (i, 0)),
        out_shape=jax.ShapeDtypeStruct((t, d), F32),
        scratch_shapes=[pltpu.VMEM((tm, d), BF16), pltpu.VMEM((tm, d), F32)],
        compiler_params=_cparams(("parallel", "arbitrary")),
        name="ffn_dense",
    )(h2, g, shift, scale, gate, wg, wu, wd)


P1_QR = 0
P1_QN = 256
P1_CKV = 1280
P1_KR = 1536
P1_QI = 1792
P1_KI = 2304
P1_WI = 2432
P1_COLS = 2560


def _proj1_layout(w_in):
    d = w_in.shape[0]
    c0 = 0
    w_qr = w_in[:, c0:c0 + C_HEADS * ROPE_DIM]; c0 += C_HEADS * ROPE_DIM
    w_qn = w_in[:, c0:c0 + C_HEADS * C_NOPE]; c0 += C_HEADS * C_NOPE
    w_ckv = w_in[:, c0:c0 + C_LATENT]; c0 += C_LATENT
    w_kr = w_in[:, c0:c0 + ROPE_DIM]; c0 += ROPE_DIM
    w_qi = w_in[:, c0:c0 + IDX_HEADS * IDX_DIM]; c0 += IDX_HEADS * IDX_DIM
    w_ki = w_in[:, c0:c0 + IDX_DIM]; c0 += IDX_DIM
    w_wi = w_in[:, c0:c0 + IDX_HEADS]
    w_qn = jnp.pad(w_qn.reshape(d, C_HEADS, C_NOPE), ((0, 0), (0, 0), (0, HEAD_DIM - C_NOPE)))
    w_qn = w_qn.reshape(d, C_HEADS * HEAD_DIM)
    w_kr = jnp.tile(w_kr, (1, C_HEADS))
    w_ki = jnp.tile(w_ki, (1, 2))
    w_wi = jnp.pad(w_wi, ((0, 0), (0, LANES - IDX_HEADS)))
    return jnp.concatenate([w_qr, w_qn, w_ckv, w_kr, w_qi, w_ki, w_wi], axis=1)


def _proj1_body(x_ref, g_ref, sh_ref, sc_ref, w_ref, gkv_ref,
                ct16_ref, sa16_ref, sb16_ref, ct64_ref, sa64_ref, sb64_ref,
                qa_ref, kp_ref, qi_ref, ki_ref, wi_ref):
    u = _norm_mod(x_ref[...], g_ref[...], sh_ref[0], sc_ref[0]).astype(BF16)
    t16 = (ct16_ref[...], sa16_ref[...], sb16_ref[...])
    t64 = (ct64_ref[...], sa64_ref[...], sb64_ref[...])
    qscale = HEAD_DIM ** -0.5

    def cols(lo, hi):
        return _dot(u, w_ref[:, lo:hi])

    a = cols(P1_QR, P1_QN)
    for c in range(2):
        blk = _rope128(a[:, c * LANES:(c + 1) * LANES], *t16) * qscale
        qa_ref[:, c * LANES:(c + 1) * LANES] = blk.astype(BF16)
    qa_ref[:, P1_QN:P1_CKV] = (cols(P1_QN, P1_CKV) * qscale).astype(BF16)
    ckv = cols(P1_CKV, P1_KR)
    ms = jnp.mean(ckv * ckv, axis=-1, keepdims=True)
    kp_ref[:, 0:C_LATENT] = (ckv * lax.rsqrt(ms + NORM_EPS) * gkv_ref[...]).astype(BF16)
    a = cols(P1_KR, P1_QI)
    for c in range(2):
        blk = _rope128(a[:, c * LANES:(c + 1) * LANES], *t16)
        kp_ref[:, C_LATENT + c * LANES:C_LATENT + (c + 1) * LANES] = blk.astype(BF16)
    a = cols(P1_QI, P1_KI)
    for c in range(4):
        qi_ref[:, c * LANES:(c + 1) * LANES] = _rope128(a[:, c * LANES:(c + 1) * LANES], *t64).astype(BF16)
    ki_ref[...] = _rope128(cols(P1_KI, P1_WI), *t64).astype(BF16)
    wi_ref[...] = cols(P1_WI, P1_COLS) * (IDX_HEADS ** -0.5) * (IDX_DIM ** -0.5)


def _proj1(x2, g, shift, scale, w_bf, gkv, t16, t64, seq, tm=512):
    t, d = x2.shape
    tpb = seq // tm
    tab_spec = pl.BlockSpec((tm, LANES), lambda i: (i % tpb, 0))
    mod_spec = pl.BlockSpec((1, 1, d), lambda i: (i // tpb, 0, 0))
    widths = (P1_CKV, 2 * C_LATENT, IDX_HEADS * IDX_DIM, LANES, LANES)
    dtypes = (BF16, BF16, BF16, BF16, F32)
    return pl.pallas_call(
        _proj1_body,
        grid=(t // tm,),
        in_specs=[pl.BlockSpec((tm, d), lambda i: (i, 0)),
                  pl.BlockSpec((1, d), lambda i: (0, 0)),
                  mod_spec, mod_spec,
                  pl.BlockSpec((d, P1_COLS), lambda i: (0, 0)),
                  pl.BlockSpec((1, C_LATENT), lambda i: (0, 0))] + [tab_spec] * 6,
        out_specs=[pl.BlockSpec((tm, w), lambda i: (i, 0)) for w in widths],
        out_shape=[jax.ShapeDtypeStruct((t, w), dt) for w, dt in zip(widths, dtypes)],
        compiler_params=_cparams(("parallel",)),
        name="proj1",
    )(x2, g, shift, scale, w_bf, gkv, *t16, *t64)


IDX_KT = 512


def _idx_body(topk, qi_ref, ki_ref, wi_ref, mask_ref, sc_scr):
    i = pl.program_id(1)
    seq = ki_ref.shape[1]
    nkt = seq // IDX_KT
    nlive = (i * Q_BLOCK) // IDX_KT + 1
    q = qi_ref[0]
    wt = wi_ref[0].T
    lane = lax.broadcasted_iota(I32, (1, LANES), 1)
    qpos = i * Q_BLOCK + lane
    krow = lax.broadcasted_iota(I32, (IDX_KT, 1), 0)
    qpair = [jnp.concatenate([q[:, (2 * g) * LANES:(2 * g + 1) * LANES],
                              q[:, (2 * g + 1) * LANES:(2 * g + 2) * LANES]], axis=0) for g in range(2)]

    def score_tile(kt, c):
        kk = ki_ref[0, pl.ds(pl.multiple_of(kt * IDX_KT, IDX_KT), IDX_KT), :]
        zero = jnp.zeros_like(kk)
        kpart = [jnp.where(lane < IDX_DIM, kk, zero), jnp.where(lane >= IDX_DIM, kk, zero)]
        acc = jnp.zeros((IDX_KT, Q_BLOCK), F32)
        for g in range(2):
            for part in range(2):
                res = jnp.maximum(_dot_nt(kpart[part], qpair[g]), 0.0)
                for j in range(2):
                    h = 2 * (2 * g + j) + part
                    acc = acc + wt[h:h + 1, :] * res[:, j * Q_BLOCK:(j + 1) * Q_BLOCK]
        acc = acc + 0.0
        sc_scr[kt] = jnp.where(kt * IDX_KT + krow <= qpos, acc, -jnp.inf)
        return c

    lax.fori_loop(0, nlive, score_tile, 0)
    kq = jnp.minimum(qpos + 1, topk).astype(F32)

    def count(pred_fn):
        def body(kt, acc):
            ones = jnp.where(pred_fn(sc_scr[kt], kt), 1.0, 0.0)
            part = jnp.sum(ones.reshape(8, IDX_KT // 64, 8, Q_BLOCK), axis=1)
            return acc + jnp.sum(part, axis=0)
        part = lax.fori_loop(0, nlive, body, jnp.zeros((8, Q_BLOCK), F32))
        return jnp.sum(part, axis=0, keepdims=True)

    def key_to_f32(key):
        bits = jnp.where(key < 0, key ^ jnp.int32(0x7FFFFFFF), key)
        return lax.bitcast_convert_type(bits, F32)

    def bit_step(it, ans):
        cand = ans | lax.shift_left(jnp.int32(1), 31 - it)
        thr = key_to_f32(cand ^ jnp.int32(INT_MIN))
        cnt = count(lambda sc, kt: sc >= thr)
        return jnp.where(cnt >= kq, cand, ans)

    ans = lax.fori_loop(0, 32, bit_step, jnp.zeros((1, Q_BLOCK), I32))
    thr = key_to_f32(ans ^ jnp.int32(INT_MIN))
    n_ge = count(lambda sc, kt: sc >= thr)
    nbits = int(math.log2(seq))

    def tie_cut(_):
        need = kq - count(lambda sc, kt: sc > thr)

        def tie_step(it, ans2):
            cand = ans2 | lax.shift_left(jnp.int32(1), nbits - 1 - it)
            cnt = count(lambda sc, kt: (sc == thr) & (kt * IDX_KT + krow < cand))
            return jnp.where(cnt < need, cand, ans2)

        return lax.fori_loop(0, nbits, tie_step, jnp.zeros((1, Q_BLOCK), I32))

    excess = jnp.max(n_ge - kq) > 0.0
    jcut = lax.cond(excess, tie_cut, lambda _: jnp.full((1, Q_BLOCK), seq, I32), 0)

    def emit(kt, c):
        sc = sc_scr[kt]
        sel = (sc > thr) | ((sc == thr) & (kt * IDX_KT + krow <= jcut))
        mask_ref[0, 0, kt] = jnp.where(sel, 1.0, 0.0).T.astype(BF16)
        return c

    lax.fori_loop(0, nlive, emit, 0)

    def emit_dead(kt, c):
        mask_ref[0, 0, kt] = jnp.zeros((Q_BLOCK, IDX_KT), BF16)
        return c

    lax.fori_loop(nlive, nkt, emit_dead, 0)


def _dsa_select(qi, ki, wi, bn, seq):
    topk = min(TOPK_MAX, seq // 4)
    nqb = seq // Q_BLOCK
    nkt = seq // IDX_KT
    return pl.pallas_call(
        functools.partial(_idx_body, topk),
        grid=(bn, nqb),
        in_specs=[pl.BlockSpec((1, Q_BLOCK, IDX_HEADS * IDX_DIM), lambda b, i: (b, i, 0)),
                  pl.BlockSpec((1, seq, LANES), lambda b, i: (b, 0, 0)),
                  pl.BlockSpec((1, Q_BLOCK, LANES), lambda b, i: (b, i, 0))],
        out_specs=pl.BlockSpec((1, 1, nkt, Q_BLOCK, IDX_KT), lambda b, i: (b, i, 0, 0, 0)),
        out_shape=jax.ShapeDtypeStruct((bn, nqb, nkt, Q_BLOCK, IDX_KT), BF16),
        scratch_shapes=[pltpu.VMEM((nkt, IDX_KT, Q_BLOCK), F32)],
        compiler_params=_cparams(("parallel", "arbitrary")),
        name="dsa_select",
    )(qi.reshape(bn, seq, -1), ki.reshape(bn, seq, -1), wi.reshape(bn, seq, -1))


ATT_RB = 512


def _dsa_attn_body(qa_ref, kp_ref, mask_ref, wuk_ref, wuv_ref, o_ref, qp_scr, m_scr, l_scr, acc_scr,
                   s_scr):
    i = pl.program_id(1)
    rows = C_HEADS * Q_BLOCK
    qr = qa_ref[0, :, 0:P1_QN]
    lane = lax.broadcasted_iota(I32, (1, C_HEADS * ROPE_DIM), 1)
    for h in range(C_HEADS):
        p = h // 2
        qn = qa_ref[0, :, P1_QN + p * LANES:P1_QN + (p + 1) * LANES]
        qlat = _dot(qn, wuk_ref[h])
        qp_scr[h * Q_BLOCK:(h + 1) * Q_BLOCK, 0:C_LATENT] = qlat.astype(BF16)
        qp_scr[h * Q_BLOCK:(h + 1) * Q_BLOCK, C_LATENT:2 * C_LATENT] = jnp.where(
            (lane // ROPE_DIM) == h, qr, jnp.zeros_like(qr))
    m_scr[...] = jnp.full(m_scr.shape, NEG_BIG, F32)
    l_scr[...] = jnp.zeros_like(l_scr)
    acc_scr[...] = jnp.zeros_like(acc_scr)
    nblk = rows // ATT_RB

    def ktile(kt):
        return kp_ref[0, pl.ds(pl.multiple_of(kt * IDX_KT, IDX_KT), IDX_KT), :]

    def softmax_pv(b, s, kt, vv):
        rs = slice(b * ATT_RB, (b + 1) * ATT_RB)
        bias = (mask_ref[0, 0, kt].astype(F32) - 1.0) * (-NEG_BIG)
        s = s + jnp.concatenate([bias] * (ATT_RB // Q_BLOCK), axis=0)
        m_prev = m_scr[rs, :]
        m_next = jnp.maximum(m_prev, jnp.max(s, axis=1, keepdims=True))
        alpha = jnp.exp(m_prev - m_next)
        pr = jnp.exp(s - jnp.concatenate([m_next] * (IDX_KT // LANES), axis=1))
        l_scr[rs, :] = alpha * l_scr[rs, :] + jnp.sum(pr, axis=1, keepdims=True)
        m_scr[rs, :] = m_next
        acc_scr[rs, :] = (jnp.concatenate([alpha] * (C_LATENT // LANES), axis=1) * acc_scr[rs, :]
                          + _dot(pr.astype(BF16), vv))

    kk0 = ktile(0)
    for b in range(nblk):
        s_scr[b * ATT_RB:(b + 1) * ATT_RB, :] = _dot_nt(qp_scr[b * ATT_RB:(b + 1) * ATT_RB, :], kk0)

    def step(kt, carry):
        kk = ktile(kt)
        kk_next = ktile(kt + 1)
        vv = kk[:, 0:C_LATENT]
        for b in range(nblk):
            rs = slice(b * ATT_RB, (b + 1) * ATT_RB)
            s = s_scr[rs, :]
            s_scr[rs, :] = _dot_nt(qp_scr[rs, :], kk_next)
            softmax_pv(b, s, kt, vv)
        return carry

    last = (i * Q_BLOCK) // IDX_KT
    lax.fori_loop(0, last, step, 0)
    vv = ktile(last)[:, 0:C_LATENT]
    for b in range(nblk):
        softmax_pv(b, s_scr[b * ATT_RB:(b + 1) * ATT_RB, :], last, vv)
    olat = (acc_scr[...] / jnp.concatenate([l_scr[...]] * (C_LATENT // LANES), axis=1)).astype(BF16)
    for p in range(C_HEADS // 2):
        o = (_dot(olat[(2 * p) * Q_BLOCK:(2 * p + 1) * Q_BLOCK], wuv_ref[2 * p])
             + _dot(olat[(2 * p + 1) * Q_BLOCK:(2 * p + 2) * Q_BLOCK], wuv_ref[2 * p + 1]))
        o_ref[0, :, p * LANES:(p + 1) * LANES] = o.astype(o_ref.dtype)


def _dsa_attention(qa, kp, mask, wuk_x, wuv_x, bn, seq):
    nqb = seq // Q_BLOCK
    nkt = seq // IDX_KT
    rows = C_HEADS * Q_BLOCK
    return pl.pallas_call(
        _dsa_attn_body,
        grid=(bn, nqb),
        in_specs=[pl.BlockSpec((1, Q_BLOCK, P1_CKV), lambda b, i: (b, i, 0)),
                  pl.BlockSpec((1, seq, 2 * C_LATENT), lambda b, i: (b, 0, 0)),
                  pl.BlockSpec((1, 1, nkt, Q_BLOCK, IDX_KT), lambda b, i: (b, i, 0, 0, 0)),
                  pl.BlockSpec((C_HEADS, LANES, C_LATENT), lambda b, i: (0, 0, 0)),
                  pl.BlockSpec((C_HEADS, C_LATENT, LANES), lambda b, i: (0, 0, 0))],
        out_specs=pl.BlockSpec((1, Q_BLOCK, C_HEADS * HEAD_DIM), lambda b, i: (b, i, 0)),
        out_shape=jax.ShapeDtypeStruct((bn, seq, C_HEADS * HEAD_DIM), BF16),
        scratch_shapes=[pltpu.VMEM((rows, 2 * C_LATENT), BF16),
                        pltpu.VMEM((rows, LANES), F32), pltpu.VMEM((rows, LANES), F32),
                        pltpu.VMEM((rows, C_LATENT), F32),
                        pltpu.VMEM((rows, IDX_KT), F32)],
        compiler_params=_cparams(("parallel", "arbitrary")),
        name="dsa_attn",
    )(qa.reshape(bn, seq, -1), kp.reshape(bn, seq, -1), mask, wuk_x, wuv_x)


def _mla_weights(w_uk, w_uv):
    wuk = jnp.transpose(w_uk, (1, 2, 0))
    wuk_x = jnp.zeros((C_HEADS, LANES, C_LATENT), F32)
    wuv = jnp.transpose(w_uv, (1, 0, 2))
    wuv_x = jnp.zeros((C_HEADS, C_LATENT, LANES), F32)
    for h in range(C_HEADS):
        o = (h % 2) * HEAD_DIM
        wuk_x = wuk_x.at[h, o:o + C_NOPE, :].set(wuk[h])
        wuv_x = wuv_x.at[h, :, o:o + HEAD_DIM].set(wuv[h])
    return wuk_x.astype(BF16), wuv_x.astype(BF16)


def _router_body(h_ref, g_ref, sh_ref, sc_ref, wr_ref, br_ref, route_ref, up_ref, cnt_ref, carry_scr):
    i = pl.program_id(0)
    tm = h_ref.shape[0]

    @pl.when(i == 0)
    def _():
        carry_scr[...] = jnp.zeros_like(carry_scr)

    u = _norm_mod(h_ref[...], g_ref[...], sh_ref[0], sc_ref[0])
    logits = jnp.dot(u, wr_ref[...], preferred_element_type=F32, precision=HIGHEST) + br_ref[...]
    lane = lax.broadcasted_iota(I32, (tm, LANES), 1).astype(F32)
    m1 = jnp.max(logits, axis=1, keepdims=True)
    e1 = jnp.min(jnp.where(logits == m1, lane, float(LANES)), axis=1, keepdims=True)
    rest = jnp.where(lane == e1, NEG_BIG * 2, logits)
    m2 = jnp.max(rest, axis=1, keepdims=True)
    e2 = jnp.min(jnp.where(rest == m2, lane, float(LANES)), axis=1, keepdims=True)
    ex = jnp.exp(m2 - m1)
    g1 = 1.0 / (1.0 + ex)
    g2 = ex / (1.0 + ex)
    onehot = ((lane == e1) | (lane == e2))
    oh_bf = jnp.where(onehot, 1.0, 0.0).astype(BF16)
    r = lax.broadcasted_iota(I32, (tm, tm), 0)
    c = lax.broadcasted_iota(I32, (tm, tm), 1)
    tri = jnp.where(c < r, 1.0, 0.0).astype(BF16)
    prefix = _dot(tri, oh_bf) + carry_scr[...]
    rank1 = jnp.sum(jnp.where(lane == e1, prefix, 0.0), axis=1, keepdims=True)
    rank2 = jnp.sum(jnp.where(lane == e2, prefix, 0.0), axis=1, keepdims=True)
    carry_scr[...] = carry_scr[...] + jnp.sum(jnp.where(onehot, 1.0, 0.0), axis=0, keepdims=True)
    cnt_ref[...] = carry_scr[...]
    vals = [e1, e2, g1, g2, rank1, rank2]
    route = jnp.zeros((tm, LANES), F32)
    for k, v in enumerate(vals):
        route = jnp.where(lane == float(k), v, route)
    route_ref[...] = route
    up_ref[...] = u


def _router(h2, g, shift, scale, w_router, b_router, seq, tm=512):
    t, d = h2.shape
    tpb = seq // tm
    wr = jnp.pad(w_router.astype(F32), ((0, 0), (0, LANES - N_EXPERTS)))
    br = jnp.pad(b_router.astype(F32), (0, LANES - N_EXPERTS), constant_values=NEG_BIG).reshape(1, LANES)
    mod_spec = pl.BlockSpec((1, 1, d), lambda i: (i // tpb, 0, 0))
    return pl.pallas_call(
        _router_body,
        grid=(t // tm,),
        in_specs=[pl.BlockSpec((tm, d), lambda i: (i, 0)),
                  pl.BlockSpec((1, d), lambda i: (0, 0)),
                  mod_spec, mod_spec,
                  pl.BlockSpec((d, LANES), lambda i: (0, 0)),
                  pl.BlockSpec((1, LANES), lambda i: (0, 0))],
        out_specs=[pl.BlockSpec((tm, LANES), lambda i: (i, 0)),
                   pl.BlockSpec((tm, d), lambda i: (i, 0)),
                   pl.BlockSpec((1, LANES), lambda i: (0, 0))],
        out_shape=[jax.ShapeDtypeStruct((t, LANES), F32),
                   jax.ShapeDtypeStruct((t, d), F32),
                   jax.ShapeDtypeStruct((1, LANES), F32)],
        scratch_shapes=[pltpu.VMEM((1, LANES), F32)],
        compiler_params=_cparams(("arbitrary",)),
        name="moe_router",
    )(h2, g, shift, scale, wr, br)


def _dispatch_body(p1_ref, p2_ref, up_ref, xs_in_ref, xs_ref, sem):
    del xs_in_ref
    tm = up_ref.shape[0]

    def copy(r, dst):
        return pltpu.make_async_copy(up_ref.at[pl.ds(r, 1)], xs_ref.at[pl.ds(dst, 1)], sem)

    def start(r, c):
        copy(r, p1_ref[r]).start(priority=0)
        copy(r, p2_ref[r]).start(priority=1)
        return c

    lax.fori_loop(0, tm, start, 0, unroll=DMA_UNROLL)

    def wait(r, c):
        copy(r, p1_ref[r]).wait()
        copy(r, p2_ref[r]).wait()
        return c

    lax.fori_loop(0, tm, wait, 0, unroll=DMA_UNROLL)


def _dispatch(pos1, pos2, up, nrows, tm=256):
    t, w = up.shape
    xs0 = jnp.zeros((nrows, w), F32)
    smem_spec = pl.BlockSpec((tm,), lambda i: (i,), memory_space=pltpu.SMEM)
    return pl.pallas_call(
        _dispatch_body,
        grid=(t // tm,),
        in_specs=[smem_spec, smem_spec,
                  pl.BlockSpec((tm, w), lambda i: (i, 0)),
                  pl.BlockSpec(memory_space=pl.ANY)],
        out_specs=pl.BlockSpec(memory_space=pl.ANY),
        out_shape=jax.ShapeDtypeStruct((nrows, w), F32),
        scratch_shapes=[pltpu.SemaphoreType.DMA(())],
        input_output_aliases={3: 0},
        compiler_params=_cparams(("arbitrary",)),
        name="moe_dispatch",
    )(pos1, pos2, up, xs0)


def _expert_body(te_ref, nv_ref, xs_ref, wg_ref, wu_ref, wd_ref, y_ref, x_scr, acc_scr):
    j = pl.program_id(0)
    f = pl.program_id(1)
    live = j < nv_ref[0]

    @pl.when(f == 0)
    def _():
        x_scr[...] = xs_ref[...].astype(BF16)
        acc_scr[...] = jnp.zeros_like(acc_scr)

    @pl.when(live)
    def _():
        x = x_scr[...]
        gp = _dot(x, wg_ref[0])
        up = _dot(x, wu_ref[0])
        a = (gp * jax.nn.sigmoid(gp) * up).astype(BF16)
        acc_scr[...] += _dot(a, wd_ref[0])

    @pl.when(f == pl.num_programs(1) - 1)
    def _():
        y_ref[...] = acc_scr[...]


def _experts(tile_expert, n_live, xs, wg, wu, wd, tm, tf=MOE_TF):
    nrows, d = xs.shape
    nf = wg.shape[2] // tf
    nt = nrows // tm

    def f_eff(j, f, nv):
        return jnp.where(j < nv[0], f, nf - 1)

    grid_spec = pltpu.PrefetchScalarGridSpec(
        num_scalar_prefetch=2,
        grid=(nt, nf),
        in_specs=[pl.BlockSpec((tm, d), lambda j, f, te, nv: (j, 0)),
                  pl.BlockSpec((1, d, tf), lambda j, f, te, nv: (te[j], 0, f_eff(j, f, nv))),
                  pl.BlockSpec((1, d, tf), lambda j, f, te, nv: (te[j], 0, f_eff(j, f, nv))),
                  pl.BlockSpec((1, tf, d), lambda j, f, te, nv: (te[j], f_eff(j, f, nv), 0))],
        out_specs=pl.BlockSpec((tm, d), lambda j, f, te, nv: (j, 0)),
        scratch_shapes=[pltpu.VMEM((tm, d), BF16), pltpu.VMEM((tm, d), F32)])
    return pl.pallas_call(
        _expert_body,
        grid_spec=grid_spec,
        out_shape=jax.ShapeDtypeStruct((nrows, d), F32),
        compiler_params=_cparams(("arbitrary", "arbitrary")),
        name="moe_experts",
    )(tile_expert, n_live, xs, wg, wu, wd)


def _combine_body(p1_ref, p2_ref, route_ref, h_ref, gt_ref, gf_ref, y_ref, o_ref, y1_scr, y2_scr, sem):
    tm = h_ref.shape[0]

    def copy(src, dst_scr, r):
        return pltpu.make_async_copy(y_ref.at[pl.ds(src, 1)], dst_scr.at[pl.ds(r, 1)], sem)

    def start(r, c):
        copy(p1_ref[r], y1_scr, r).start(priority=0)
        copy(p2_ref[r], y2_scr, r).start(priority=1)
        return c

    lax.fori_loop(0, tm, start, 0, unroll=DMA_UNROLL)

    def wait(r, c):
        copy(p1_ref[r], y1_scr, r).wait()
        copy(p2_ref[r], y2_scr, r).wait()
        return c

    lax.fori_loop(0, tm, wait, 0, unroll=DMA_UNROLL)
    route = route_ref[...]
    g1, g2 = route[:, 2:3], route[:, 3:4]
    y = g1 * y1_scr[...] + g2 * y2_scr[...]
    hn = h_ref[...] + gt_ref[0] * y
    ms = jnp.mean(hn * hn, axis=-1, keepdims=True)
    o_ref[...] = hn * lax.rsqrt(ms + NORM_EPS) * gf_ref[...]


def _combine(pos1, pos2, route, h2, gate, g_final, y_sorted, seq, tm=256):
    t, d = h2.shape
    tpb = seq // tm
    smem_spec = pl.BlockSpec((tm,), lambda i: (i,), memory_space=pltpu.SMEM)
    return pl.pallas_call(
        _combine_body,
        grid=(t // tm,),
        in_specs=[smem_spec, smem_spec,
                  pl.BlockSpec((tm, LANES), lambda i: (i, 0)),
                  pl.BlockSpec((tm, d), lambda i: (i, 0)),
                  pl.BlockSpec((1, 1, d), lambda i: (i // tpb, 0, 0)),
                  pl.BlockSpec((1, d), lambda i: (0, 0)),
                  pl.BlockSpec(memory_space=pl.ANY)],
        out_specs=pl.BlockSpec((tm, d), lambda i: (i, 0)),
        out_shape=jax.ShapeDtypeStruct((t, d), F32),
        scratch_shapes=[pltpu.VMEM((tm, d), F32), pltpu.VMEM((tm, d), F32),
                        pltpu.SemaphoreType.DMA(())],
        compiler_params=_cparams(("arbitrary",)),
        name="moe_combine",
    )(pos1, pos2, route, h2, gate, g_final, y_sorted)


MOE_TILE = 512


def _moe_layout(route, counts):
    e1 = route[:, 0].astype(I32)
    e2 = route[:, 1].astype(I32)
    r1 = route[:, 4].astype(I32)
    r2 = route[:, 5].astype(I32)
    cnt = counts[0, :N_EXPERTS].astype(I32)
    tiles = (cnt + MOE_TILE - 1) // MOE_TILE
    tile_end = jnp.cumsum(tiles)
    start = (tile_end - tiles) * MOE_TILE
    pos1 = start[e1] + r1
    pos2 = start[e2] + r2
    nt = route.shape[0] * 2 // MOE_TILE + N_EXPERTS
    n_live = tile_end[-1]
    tid = jnp.minimum(jnp.arange(nt, dtype=I32), n_live - 1)
    tile_expert = jnp.sum((tid[:, None] >= tile_end[None, :]).astype(I32), axis=1)
    return pos1, pos2, tile_expert.astype(I32), n_live.reshape(1).astype(I32), nt * MOE_TILE


def kernel(x, c, w_ada, b_ada, g_mix, g_ffn, g_final, e_w_in, e_w_out, s5_a_re, s5_a_im, s5_log_dt,
           s5_b_re, s5_b_im, s5_c_re, s5_c_im, s5_d, s5_w_glu, s5_b_glu, ff_w_gate, ff_w_up,
           ff_w_down, o_w_in, o_w_out, mla_g_kv, mla_w_uk, mla_w_uv, moe_w_router, moe_b_router,
           moe_w_gate, moe_w_up, moe_w_down):
    bn, seq, d = x.shape
    t = bn * seq
    mod = _adaln(c, w_ada, b_ada)

    def mod_vec(layer, k):
        return mod[layer, :, k * d:(k + 1) * d].reshape(bn, 1, d)

    tabs64 = _rope_tables(seq, HEAD_DIM)
    tabs16 = _rope_tables(seq, ROPE_DIM)
    h = x.reshape(t, d)

    *qkv_views, s_in = _proj0(h, g_mix[0].reshape(1, d), mod_vec(0, 0), mod_vec(0, 1),
                              e_w_in[0].astype(BF16), tabs64, seq)
    y_a = _dilated_attention(qkv_views, bn, seq)
    prm = _s5_prepare(s5_a_re[0], s5_a_im[0], s5_log_dt[0], s5_b_re[0], s5_b_im[0],
                      s5_c_re[0], s5_c_im[0], s5_d[0], seq // S5_CHUNK)
    y_b = _s5_mixer(s_in, prm, s5_w_glu[0].astype(BF16), s5_b_glu[0], bn, seq)
    h = _outproj([y_a], [y_b], e_w_out[0].astype(BF16), h, mod_vec(0, 2), seq)
    h = _ffn_dense(h, g_ffn[0].reshape(1, d), mod_vec(0, 3), mod_vec(0, 4), mod_vec(0, 5),
                   ff_w_gate[0].astype(BF16), ff_w_up[0].astype(BF16), ff_w_down[0].astype(BF16), seq)

    w1 = _proj1_layout(o_w_in[0]).astype(BF16)
    qa, kp, qi, ki, wi = _proj1(h, g_mix[1].reshape(1, d), mod_vec(1, 0), mod_vec(1, 1), w1,
                                mla_g_kv[0].reshape(1, C_LATENT).astype(F32), tabs16, tabs64, seq)
    mask = _dsa_select(qi, ki, wi, bn, seq)
    wuk_x, wuv_x = _mla_weights(mla_w_uk[0], mla_w_uv[0])
    o_attn = _dsa_attention(qa, kp, mask, wuk_x, wuv_x, bn, seq)
    h = _outproj([o_attn.reshape(t, C_HEADS * HEAD_DIM)], [], o_w_out[0].astype(BF16), h, mod_vec(1, 2), seq)

    route, up, counts = _router(h, g_ffn[1].reshape(1, d), mod_vec(1, 3), mod_vec(1, 4),
                                moe_w_router[0], moe_b_router[0], seq)
    pos1, pos2, tile_expert, n_live, nrows = _moe_layout(route, counts)
    xs = _dispatch(pos1, pos2, up, nrows)
    y_sorted = _experts(tile_expert, n_live, xs, moe_w_gate[0].astype(BF16), moe_w_up[0].astype(BF16),
                        moe_w_down[0].astype(BF16), MOE_TILE)
    out = _combine(pos1, pos2, route, h, mod_vec(1, 5), g_final.reshape(1, d), y_sorted, seq)
    return out.reshape(bn, seq, d)
```

```python
import functools
import math

import jax
import jax.numpy as jnp
import numpy as np
from jax import lax
from jax.experimental import pallas as pl
from jax.experimental.pallas import tpu as pltpu

F32 = jnp.float32
BF16 = jnp.bfloat16
I32 = jnp.int32
HIGHEST = lax.Precision.HIGHEST

D_MODEL = 1024
HEAD_DIM = 64
ROPE_DIM = 16
ROPE_THETA = 500000.0
NORM_EPS = 1e-6
A_HEADS = 8
A_WIDTH = 512
A_PATTERNS = ((128, 1), (512, 4), (2048, 16))
B_WIDTH = 512
B_GROUP = 16
B_GROUPS = 32
B_STATE = 64
C_HEADS = 16
C_NOPE = 48
C_LATENT = 256
IDX_HEADS = 8
IDX_DIM = 64
TOPK_MAX = 256
FF_DENSE = 2816
N_EXPERTS = 8
FF_EXPERT = 3584

LANES = 128
Q_BLOCK = 128
S5_CHUNK = 16
MOE_TF = 512
FFN_TF = 256
DMA_UNROLL = 8
VMEM_LIMIT = 56 * 1024 * 1024
NEG_BIG = -1e30
INT_MIN = -(2 ** 31)


def _cparams(sem, vmem=VMEM_LIMIT):
    return pltpu.CompilerParams(dimension_semantics=sem, vmem_limit_bytes=vmem)


def _dot(a, b):
    return jnp.dot(a, b, preferred_element_type=F32)


def _dot_nt(a, b):
    return lax.dot_general(a, b, (((1,), (1,)), ((), ())), preferred_element_type=F32)


def _norm_mod(x, g, shift, scale):
    ms = jnp.mean(x * x, axis=-1, keepdims=True)
    y = x * lax.rsqrt(ms + NORM_EPS) * g
    return y * (1.0 + scale) + shift


def _rope128(x, ct, sa, sb):
    return x * ct + pltpu.roll(x, LANES - ROPE_DIM // 2, 1) * sa + pltpu.roll(x, ROPE_DIM // 2, 1) * sb


def _adaln_body(c_ref, w_ref, b_ref, o_ref):
    c = c_ref[...]
    ca = c * jax.nn.sigmoid(c)
    o_ref[0] = jnp.dot(ca, w_ref[0], preferred_element_type=F32, precision=HIGHEST) + b_ref[0]


def _adaln(c, w_ada, b_ada):
    depth, d, d6 = w_ada.shape
    bn = c.shape[0]
    rows = 8
    cp = jnp.zeros((rows, d), F32).at[:bn].set(c)
    tn = 1536
    out = pl.pallas_call(
        _adaln_body,
        grid=(depth, d6 // tn),
        in_specs=[pl.BlockSpec((rows, d), lambda l, j: (0, 0)),
                  pl.BlockSpec((1, d, tn), lambda l, j: (l, 0, j)),
                  pl.BlockSpec((1, 1, tn), lambda l, j: (l, 0, j))],
        out_specs=pl.BlockSpec((1, rows, tn), lambda l, j: (l, 0, j)),
        out_shape=jax.ShapeDtypeStruct((depth, rows, d6), F32),
        compiler_params=_cparams(("arbitrary", "arbitrary")),
        name="adaln",
    )(cp, w_ada, b_ada.reshape(depth, 1, d6))
    return out[:, :bn]


def _rope_tables(seq, period):
    half = ROPE_DIM // 2
    pos = jnp.arange(seq, dtype=F32)
    inv = ROPE_THETA ** (-jnp.arange(0, ROPE_DIM, 2, dtype=F32) / ROPE_DIM)
    ang = pos[:, None] * inv[None, :]
    cos, sin = jnp.cos(ang), jnp.sin(ang)
    lane = np.arange(LANES) % period
    first = lane < half
    second = (lane >= half) & (lane < ROPE_DIM)
    idx = np.where(first, lane, np.where(second, lane - half, 0))
    cos_l, sin_l = cos[:, idx], sin[:, idx]
    ct = jnp.where(first | second, cos_l, 1.0)
    sa = jnp.where(first, -sin_l, 0.0)
    sb = jnp.where(second, sin_l, 0.0)
    return ct, sa, sb


def _proj0_body(x_ref, g_ref, sh_ref, sc_ref, w_ref, ct_ref, sa_ref, sb_ref, *rest):
    qkv_refs, s_ref, acc_scr = rest[:-2], rest[-2], rest[-1]
    tm = x_ref.shape[0]
    qw = 3 * A_WIDTH
    u = _norm_mod(x_ref[...], g_ref[...], sh_ref[0], sc_ref[0]).astype(BF16)
    ct, sa, sb = ct_ref[...], sa_ref[...], sb_ref[...]
    for j in range(3):
        acc = _dot(u, w_ref[:, j * A_WIDTH:(j + 1) * A_WIDTH])
        for c in range(A_WIDTH // LANES):
            a = acc[:, c * LANES:(c + 1) * LANES]
            if j < 2:
                a = _rope128(a, ct, sa, sb)
            if j == 0:
                a = a * (HEAD_DIM ** -0.5)
            acc_scr[j * (A_WIDTH // LANES) + c] = a
    acc = _dot(u, w_ref[:, qw:])
    nq = qw // LANES
    for c in range(B_WIDTH // LANES):
        acc_scr[nq + c] = acc[:, c * LANES:(c + 1) * LANES]
    for (_, dil), ref in zip(A_PATTERNS, qkv_refs):
        for r in range(dil):
            for c in range(nq):
                ref[:, r * qw + c * LANES:r * qw + (c + 1) * LANES] = (
                    acc_scr[c, pl.ds(r, tm // dil, stride=dil), :].astype(BF16))
    for r in range(S5_CHUNK):
        for c in range(B_WIDTH // LANES):
            s_ref[:, r * B_WIDTH + c * LANES:r * B_WIDTH + (c + 1) * LANES] = (
                acc_scr[nq + c, pl.ds(r, tm // S5_CHUNK, stride=S5_CHUNK), :])


def _proj0(x2, g, shift, scale, w_bf, tabs, seq, tm=512):
    t, d = x2.shape
    n = w_bf.shape[1]
    tpb = seq // tm
    ct, sa, sb = tabs
    tab_spec = pl.BlockSpec((tm, LANES), lambda i: (i % tpb, 0))
    mod_spec = pl.BlockSpec((1, 1, d), lambda i: (i // tpb, 0, 0))
    qw = 3 * A_WIDTH
    dils = [dil for _, dil in A_PATTERNS]
    out_specs = [pl.BlockSpec((tm // dil, dil * qw), lambda i: (i, 0)) for dil in dils]
    out_shape = [jax.ShapeDtypeStruct((t // dil, dil * qw), BF16) for dil in dils]
    out_specs.append(pl.BlockSpec((tm // S5_CHUNK, S5_CHUNK * B_WIDTH), lambda i: (i, 0)))
    out_shape.append(jax.ShapeDtypeStruct((t // S5_CHUNK, S5_CHUNK * B_WIDTH), F32))
    return pl.pallas_call(
        _proj0_body,
        grid=(t // tm,),
        in_specs=[pl.BlockSpec((tm, d), lambda i: (i, 0)),
                  pl.BlockSpec((1, d), lambda i: (0, 0)),
                  mod_spec, mod_spec,
                  pl.BlockSpec((d, n), lambda i: (0, 0)),
                  tab_spec, tab_spec, tab_spec],
        out_specs=out_specs,
        out_shape=out_shape,
        scratch_shapes=[pltpu.VMEM((n // LANES, tm, LANES), F32)],
        compiler_params=_cparams(("parallel",)),
        name="proj0",
    )(x2, g, shift, scale, w_bf, ct, sa, sb)


DIL_QB_MAX = 4


def _dil_body(q_ref, kp_ref, kc_ref, vp_ref, vc_ref, lse_ref, o_ref):
    i = pl.program_id(2)
    nqb = q_ref.shape[1] // Q_BLOCK
    kall = jnp.concatenate([kp_ref[0], kc_ref[0]], axis=0)
    vall = jnp.concatenate([vp_ref[0], vc_ref[0]], axis=0)
    row = lax.broadcasted_iota(I32, (Q_BLOCK, 2 * Q_BLOCK), 0)
    col = lax.broadcasted_iota(I32, (Q_BLOCK, 2 * Q_BLOCK), 1)
    rel = row + Q_BLOCK - col
    band = (rel >= 0) & (rel <= Q_BLOCK)
    lane = lax.broadcasted_iota(I32, (1, LANES), 1)
    npair = A_WIDTH // LANES
    hpp = LANES // HEAD_DIM
    scores = []
    for j in range(nqb):
        valid = band if j > 0 else band & ((col >= Q_BLOCK) | (i > 0))
        qj = q_ref[0, j * Q_BLOCK:(j + 1) * Q_BLOCK, :]
        kj = kall[j * Q_BLOCK:(j + 2) * Q_BLOCK]
        for p in range(npair):
            sl = slice(p * LANES, (p + 1) * LANES)
            for hh in range(hpp):
                hm = (lane // HEAD_DIM) == hh
                qh = jnp.where(hm, qj[:, sl], jnp.zeros_like(qj[:, sl]))
                scores.append(jnp.where(valid, _dot_nt(qh, kj[:, sl]), -jnp.inf))
    s = jnp.concatenate(scores, axis=0)
    m = jnp.max(s, axis=1, keepdims=True)
    pr = jnp.exp(s - m)
    l = jnp.sum(pr, axis=1, keepdims=True)
    prb = pr.astype(BF16)
    for j in range(nqb):
        vj = vall[j * Q_BLOCK:(j + 2) * Q_BLOCK]
        qs = slice(j * Q_BLOCK, (j + 1) * Q_BLOCK)
        for p in range(npair):
            sl = slice(p * LANES, (p + 1) * LANES)
            m_pair = l_pair = o_pair = None
            for hh in range(hpp):
                h = (j * npair + p) * hpp + hh
                rs = slice(h * Q_BLOCK, (h + 1) * Q_BLOCK)
                hm = (lane // HEAD_DIM) == hh
                o = _dot(prb[rs], vj[:, sl])
                if hh == 0:
                    m_pair = jnp.broadcast_to(m[rs], (Q_BLOCK, LANES))
                    l_pair = jnp.broadcast_to(l[rs], (Q_BLOCK, LANES))
                    o_pair = o
                else:
                    m_pair = jnp.where(hm, m[rs], m_pair)
                    l_pair = jnp.where(hm, l[rs], l_pair)
                    o_pair = jnp.where(hm, o, o_pair)
            lse_ref[0, qs, sl] = m_pair + jnp.log(l_pair)
            o_ref[0, qs, sl] = (o_pair / l_pair).astype(o_ref.dtype)


def _dilated_branch(qkv_d, dil, bn, seq):
    sd = seq // dil
    nqb = min(DIL_QB_MAX, sd // Q_BLOCK)
    nb = sd // (nqb * Q_BLOCK)
    cur = (1, nqb * Q_BLOCK, A_WIDTH)
    prev = (1, Q_BLOCK, A_WIDTH)
    q_spec = pl.BlockSpec(cur, lambda b, r, i: (b, i, r * 3))
    kp_spec = pl.BlockSpec(prev, lambda b, r, i: (b, jnp.maximum(nqb * i - 1, 0), r * 3 + 1))
    kc_spec = pl.BlockSpec(cur, lambda b, r, i: (b, i, r * 3 + 1))
    vp_spec = pl.BlockSpec(prev, lambda b, r, i: (b, jnp.maximum(nqb * i - 1, 0), r * 3 + 2))
    vc_spec = pl.BlockSpec(cur, lambda b, r, i: (b, i, r * 3 + 2))
    st_spec = pl.BlockSpec(cur, lambda b, r, i: (b, i, r))
    lse, out = pl.pallas_call(
        _dil_body,
        grid=(bn, dil, nb),
        in_specs=[q_spec, kp_spec, kc_spec, vp_spec, vc_spec],
        out_specs=[st_spec] * 2,
        out_shape=[jax.ShapeDtypeStruct((bn, sd, dil * A_WIDTH), F32),
                   jax.ShapeDtypeStruct((bn, sd, dil * A_WIDTH), BF16)],
        compiler_params=_cparams(("parallel", "parallel", "arbitrary")),
        name=f"dilated_d{dil}",
    )(*([qkv_d.reshape(bn, sd, dil * 3 * A_WIDTH)] * 5))
    return [lse.reshape(bn * sd, dil * A_WIDTH), out.reshape(bn * sd, dil * A_WIDTH)]


def _dil_merge_body(*refs):
    nbr = len(A_PATTERNS)
    stats = refs[:2 * nbr]
    y_ref = refs[2 * nbr]
    scr = refs[2 * nbr + 1:]
    tm = y_ref.shape[0]
    nc = A_WIDTH // LANES
    lses, outs = [], []
    si = 0
    for gi, (_, dil) in enumerate(A_PATTERNS):
        pair = []
        for ref in stats[2 * gi:2 * gi + 2]:
            if dil == 1:
                pair.append(ref[...].astype(F32))
            else:
                for r in range(dil):
                    for c in range(nc):
                        scr[si][c, pl.ds(r, tm // dil, stride=dil), :] = (
                            ref[:, r * A_WIDTH + c * LANES:r * A_WIDTH + (c + 1) * LANES].astype(F32))
                pair.append(jnp.concatenate([scr[si][c] for c in range(nc)], axis=1))
                si += 1
        lses.append(pair[0])
        outs.append(pair[1])
    top = functools.reduce(jnp.maximum, lses)
    den = jnp.zeros_like(top)
    num = jnp.zeros_like(top)
    for lse, o in zip(lses, outs):
        w = jnp.exp(lse - top)
        den = den + w
        num = num + w * o
    y_ref[...] = (num / den).astype(y_ref.dtype)


def _dilated_attention(qkv_views, bn, seq, tm=512):
    t = bn * seq
    stats, in_specs = [], []
    nscr = 0
    for (_, dil), qkv_d in zip(A_PATTERNS, qkv_views):
        stats += _dilated_branch(qkv_d, dil, bn, seq)
        in_specs += [pl.BlockSpec((tm // dil, dil * A_WIDTH), lambda i: (i, 0))] * 2
        nscr += 2 if dil > 1 else 0
    return pl.pallas_call(
        _dil_merge_body,
        grid=(t // tm,),
        in_specs=in_specs,
        out_specs=pl.BlockSpec((tm, A_WIDTH), lambda i: (i, 0)),
        out_shape=jax.ShapeDtypeStruct((t, A_WIDTH), BF16),
        scratch_shapes=[pltpu.VMEM((A_WIDTH // LANES, tm, LANES), F32)] * nscr,
        compiler_params=_cparams(("parallel",)),
        name="dilated_merge",
    )(*stats)


def _s5_prepare(a_re, a_im, log_dt, b_re, b_im, c_re, c_im, d_skip, nchunk):
    dt = jnp.exp(log_dt.astype(F32))[:, None]
    lr, li = a_re.astype(F32), a_im.astype(F32)
    mag = jnp.exp(lr * dt)
    ar = mag * jnp.cos(li * dt)
    ai = mag * jnp.sin(li * dt)
    den = lr * lr + li * li
    nr = ar - 1.0
    fr = (nr * lr + ai * li) / den
    fi = (ai * lr - nr * li) / den
    br, bi = b_re.astype(F32), b_im.astype(F32)
    bbr = fr[..., None] * br - fi[..., None] * bi
    bbi = fr[..., None] * bi + fi[..., None] * br
    nblk = B_WIDTH // LANES
    gpb = B_GROUPS // nblk
    eye = jnp.eye(gpb, dtype=F32)

    def bdiag_in(m):
        m = jnp.swapaxes(m.reshape(nblk, gpb, B_STATE, B_GROUP), 2, 3)
        m = m[:, :, :, None, :] * eye[None, :, None, :, None]
        return m.reshape(nblk, gpb * B_GROUP, gpb * B_STATE)

    def bdiag_out(m):
        m = jnp.swapaxes(m.reshape(nblk, gpb, B_GROUP, B_STATE), 2, 3)
        m = m[:, :, :, None, :] * eye[None, :, None, :, None]
        return m.reshape(nblk, gpb * B_STATE, gpb * B_GROUP)

    pr, pi = ar, ai
    for _ in range(int(math.log2(S5_CHUNK))):
        pr, pi = pr * pr - pi * pi, 2.0 * pr * pi
    pows_r, pows_i = [], []
    for _ in range(max(1, int(math.ceil(math.log2(nchunk))))):
        pows_r.append(pr.reshape(1, -1))
        pows_i.append(pi.reshape(1, -1))
        pr, pi = pr * pr - pi * pi, 2.0 * pr * pi
    return dict(
        ar=ar.reshape(1, -1), ai=ai.reshape(1, -1),
        b_re=bdiag_in(bbr).astype(BF16), b_im=bdiag_in(bbi).astype(BF16),
        c_re=bdiag_out(c_re.astype(F32)).astype(BF16), c_im=bdiag_out(-c_im.astype(F32)).astype(BF16),
        d=d_skip.astype(F32).reshape(1, B_WIDTH),
        pows_r=jnp.concatenate(pows_r, axis=0), pows_i=jnp.concatenate(pows_i, axis=0))


def _s5_local_body(u_ref, bre_ref, bim_ref, ar_ref, ai_ref, sre_ref, sim_ref):
    tr = u_ref.shape[0]
    sw = bre_ref.shape[2]
    for blk in range(B_WIDTH // LANES):
        arb = ar_ref[:, blk * sw:(blk + 1) * sw]
        aib = ai_ref[:, blk * sw:(blk + 1) * sw]
        sr = jnp.zeros((tr, sw), F32)
        si = jnp.zeros((tr, sw), F32)
        for t in range(S5_CHUNK):
            ub = u_ref[:, t * B_WIDTH + blk * LANES:t * B_WIDTH + (blk + 1) * LANES].astype(BF16)
            sr, si = (arb * sr - aib * si + _dot(ub, bre_ref[blk]),
                      arb * si + aib * sr + _dot(ub, bim_ref[blk]))
        sre_ref[:, blk * sw:(blk + 1) * sw] = sr
        sim_ref[:, blk * sw:(blk + 1) * sw] = si


def _s5_scan_body(nsteps, sre_ref, sim_ref, pr_ref, pi_ref, xre_ref, xim_ref):
    nc = sre_ref.shape[0]
    sw = 512
    row = lax.broadcasted_iota(I32, (nc, 1), 0)
    for cb in range(sre_ref.shape[1] // sw):
        sl = slice(cb * sw, (cb + 1) * sw)
        xr, xi = sre_ref[:, sl], sim_ref[:, sl]
        for s in range(nsteps):
            sh = 1 << s
            ok = row >= sh
            rr = jnp.where(ok, pltpu.roll(xr, sh, 0), 0.0)
            ri = jnp.where(ok, pltpu.roll(xi, sh, 0), 0.0)
            pr, pi = pr_ref[s:s + 1, sl], pi_ref[s:s + 1, sl]
            xr, xi = xr + pr * rr - pi * ri, xi + pr * ri + pi * rr
        ok = row >= 1
        xre_ref[:, sl] = jnp.where(ok, pltpu.roll(xr, 1, 0), 0.0)
        xim_ref[:, sl] = jnp.where(ok, pltpu.roll(xi, 1, 0), 0.0)


def _s5_out_body(u_ref, xre_ref, xim_ref, bre_ref, bim_ref, cre_ref, cim_ref, ar_ref, ai_ref,
                 d_ref, wglu_ref, bglu_ref, out_ref, y_scr):
    sw = bre_ref.shape[2]
    for blk in range(B_WIDTH // LANES):
        arb = ar_ref[:, blk * sw:(blk + 1) * sw]
        aib = ai_ref[:, blk * sw:(blk + 1) * sw]
        xr = xre_ref[:, blk * sw:(blk + 1) * sw]
        xi = xim_ref[:, blk * sw:(blk + 1) * sw]
        db = d_ref[:, blk * LANES:(blk + 1) * LANES]
        for t in range(S5_CHUNK):
            cs = slice(t * B_WIDTH + blk * LANES, t * B_WIDTH + (blk + 1) * LANES)
            uf = u_ref[:, cs]
            ub = uf.astype(BF16)
            xr, xi = (arb * xr - aib * xi + _dot(ub, bre_ref[blk]),
                      arb * xi + aib * xr + _dot(ub, bim_ref[blk]))
            y = _dot(xr.astype(BF16), cre_ref[blk]) + _dot(xi.astype(BF16), cim_ref[blk]) + db * uf
            y_scr[:, cs] = jax.nn.gelu(y, approximate=True)
    for t in range(S5_CHUNK):
        cs = slice(t * B_WIDTH, (t + 1) * B_WIDTH)
        y = y_scr[:, cs]
        z = _dot(y.astype(BF16), wglu_ref[...]) + bglu_ref[...]
        out_ref[:, cs] = (y * jax.nn.sigmoid(z)).astype(out_ref.dtype)


def _s5_mixer(s_in, prm, w_glu_bf, b_glu, bn, seq, tr=128):
    t = bn * seq
    nrow = t // S5_CHUNK
    ncb = seq // S5_CHUNK
    width = S5_CHUNK * B_WIDTH
    sdim = B_GROUPS * B_STATE
    uc = s_in.reshape(nrow, width)
    nblk = B_WIDTH // LANES
    const3 = lambda i: (0, 0, 0)
    const2 = lambda i: (0, 0)
    b_spec = pl.BlockSpec((nblk, LANES, sdim // nblk), const3)
    c_spec = pl.BlockSpec((nblk, sdim // nblk, LANES), const3)
    a_spec = pl.BlockSpec((1, sdim), const2)
    s_re, s_im = pl.pallas_call(
        _s5_local_body,
        grid=(nrow // tr,),
        in_specs=[pl.BlockSpec((tr, width), lambda i: (i, 0)), b_spec, b_spec, a_spec, a_spec],
        out_specs=[pl.BlockSpec((tr, sdim), lambda i: (i, 0))] * 2,
        out_shape=[jax.ShapeDtypeStruct((nrow, sdim), F32)] * 2,
        compiler_params=_cparams(("parallel",)),
        name="s5_local",
    )(uc, prm['b_re'], prm['b_im'], prm['ar'], prm['ai'])
    nsteps = prm['pows_r'].shape[0]
    x_re, x_im = pl.pallas_call(
        functools.partial(_s5_scan_body, nsteps),
        grid=(bn,),
        in_specs=[pl.BlockSpec((ncb, sdim), lambda b: (b, 0))] * 2
                 + [pl.BlockSpec((nsteps, sdim), lambda b: (0, 0))] * 2,
        out_specs=[pl.BlockSpec((ncb, sdim), lambda b: (b, 0))] * 2,
        out_shape=[jax.ShapeDtypeStruct((nrow, sdim), F32)] * 2,
        compiler_params=_cparams(("parallel",)),
        name="s5_scan",
    )(s_re, s_im, prm['pows_r'], prm['pows_i'])
    y = pl.pallas_call(
        _s5_out_body,
        grid=(nrow // tr,),
        in_specs=[pl.BlockSpec((tr, width), lambda i: (i, 0)),
                  pl.BlockSpec((tr, sdim), lambda i: (i, 0)),
                  pl.BlockSpec((tr, sdim), lambda i: (i, 0)),
                  b_spec, b_spec, c_spec, c_spec, a_spec, a_spec,
                  pl.BlockSpec((1, B_WIDTH), const2),
                  pl.BlockSpec((B_WIDTH, B_WIDTH), const2),
                  pl.BlockSpec((1, B_WIDTH), const2)],
        out_specs=pl.BlockSpec((tr, width), lambda i: (i, 0)),
        out_shape=jax.ShapeDtypeStruct((nrow, width), BF16),
        scratch_shapes=[pltpu.VMEM((tr, width), F32)],
        compiler_params=_cparams(("parallel",)),
        name="s5_out",
    )(uc, x_re, x_im, prm['b_re'], prm['b_im'], prm['c_re'], prm['c_im'], prm['ar'], prm['ai'],
      prm['d'], w_glu_bf, b_glu.reshape(1, B_WIDTH).astype(F32))
    return y


def _outproj_body(nparts, nflat, *refs):
    parts = refs[:nparts]
    flats = refs[nparts:nparts + nflat]
    w_ref, h_ref, gt_ref, o_ref = refs[nparts + nflat:nparts + nflat + 4]
    scr = refs[nparts + nflat + 4:]
    tm = h_ref.shape[0]
    acc = None
    off = 0
    for p in parts:
        k = p.shape[1]
        d = _dot(p[...].astype(BF16), w_ref[off:off + k, :])
        acc = d if acc is None else acc + d
        off += k
    for p, s in zip(flats, scr):
        nc = s.shape[0]
        k = nc * LANES
        for r in range(S5_CHUNK):
            for c in range(nc):
                s[c, pl.ds(r, tm // S5_CHUNK, stride=S5_CHUNK), :] = (
                    p[:, r * k + c * LANES:r * k + (c + 1) * LANES].astype(F32))
        rows = jnp.concatenate([s[c] for c in range(nc)], axis=1)
        d = _dot(rows.astype(BF16), w_ref[off:off + k, :])
        acc = d if acc is None else acc + d
        off += k
    o_ref[...] = h_ref[...] + gt_ref[0] * acc


def _outproj(parts, flat_parts, w_bf, h2, gate, seq, tm=512):
    t, d = h2.shape
    tpb = seq // tm
    in_specs = [pl.BlockSpec((tm, p.shape[1]), lambda i: (i, 0)) for p in parts]
    in_specs += [pl.BlockSpec((tm // S5_CHUNK, p.shape[1]), lambda i: (i, 0)) for p in flat_parts]
    in_specs += [pl.BlockSpec(w_bf.shape, lambda i: (0, 0)),
                 pl.BlockSpec((tm, d), lambda i: (i, 0)),
                 pl.BlockSpec((1, 1, d), lambda i: (i // tpb, 0, 0))]
    return pl.pallas_call(
        functools.partial(_outproj_body, len(parts), len(flat_parts)),
        grid=(t // tm,),
        in_specs=in_specs,
        out_specs=pl.BlockSpec((tm, d), lambda i: (i, 0)),
        out_shape=jax.ShapeDtypeStruct((t, d), F32),
        scratch_shapes=[pltpu.VMEM((p.shape[1] // S5_CHUNK // LANES, tm, LANES), F32) for p in flat_parts],
        compiler_params=_cparams(("parallel",)),
        name="outproj",
    )(*parts, *flat_parts, w_bf, h2, gate)


def _ffn_body(h_ref, g_ref, sh_ref, sc_ref, gt_ref, wg_ref, wu_ref, wd_ref, o_ref, u_scr, acc_scr):
    f = pl.program_id(1)

    @pl.when(f == 0)
    def _():
        u_scr[...] = _norm_mod(h_ref[...], g_ref[...], sh_ref[0], sc_ref[0]).astype(BF16)
        acc_scr[...] = jnp.zeros_like(acc_scr)

    u = u_scr[...]
    gp = _dot(u, wg_ref[...])
    up = _dot(u, wu_ref[...])
    a = (gp * jax.nn.sigmoid(gp) * up).astype(BF16)
    acc_scr[...] += _dot(a, wd_ref[...])

    @pl.when(f == pl.num_programs(1) - 1)
    def _():
        o_ref[...] = h_ref[...] + gt_ref[0] * acc_scr[...]


def _ffn_dense(h2, g, shift, scale, gate, wg, wu, wd, seq, tm=1024, tf=FFN_TF):
    t, d = h2.shape
    ff = wg.shape[1]
    tpb = seq // tm
    mod_spec = pl.BlockSpec((1, 1, d), lambda i, f: (i // tpb, 0, 0))
    return pl.pallas_call(
        _ffn_body,
        grid=(t // tm, ff // tf),
        in_specs=[pl.BlockSpec((tm, d), lambda i, f: (i, 0)),
                  pl.BlockSpec((1, d), lambda i, f: (0, 0)),
                  mod_spec, mod_spec, mod_spec,
                  pl.BlockSpec((d, tf), lambda i, f: (0, f)),
                  pl.BlockSpec((d, tf), lambda i, f: (0, f)),
                  pl.BlockSpec((tf, d), lambda i, f: (f, 0))],
        out_specs=pl.BlockSpec((tm, d), lambda i, f: (i, 0)),
        out_shape=jax.ShapeDtypeStruct((t, d), F32),
        scratch_shapes=[pltpu.VMEM((tm, d), BF16), pltpu.VMEM((tm, d), F32)],
        compiler_params=_cparams(("parallel", "arbitrary")),
        name="ffn_dense",
    )(h2, g, shift, scale, gate, wg, wu, wd)


P1_QR = 0
P1_QN = 256
P1_CKV = 1280
P1_KR = 1536
P1_QI = 1792
P1_KI = 2304
P1_WI = 2432
P1_COLS = 2560


def _proj1_layout(w_in):
    d = w_in.shape[0]
    c0 = 0
    w_qr = w_in[:, c0:c0 + C_HEADS * ROPE_DIM]; c0 += C_HEADS * ROPE_DIM
    w_qn = w_in[:, c0:c0 + C_HEADS * C_NOPE]; c0 += C_HEADS * C_NOPE
    w_ckv = w_in[:, c0:c0 + C_LATENT]; c0 += C_LATENT
    w_kr = w_in[:, c0:c0 + ROPE_DIM]; c0 += ROPE_DIM
    w_qi = w_in[:, c0:c0 + IDX_HEADS * IDX_DIM]; c0 += IDX_HEADS * IDX_DIM
    w_ki = w_in[:, c0:c0 + IDX_DIM]; c0 += IDX_DIM
    w_wi = w_in[:, c0:c0 + IDX_HEADS]
    w_qn = jnp.pad(w_qn.reshape(d, C_HEADS, C_NOPE), ((0, 0), (0, 0), (0, HEAD_DIM - C_NOPE)))
    w_qn = w_qn.reshape(d, C_HEADS * HEAD_DIM)
    w_kr = jnp.tile(w_kr, (1, C_HEADS))
    w_ki = jnp.tile(w_ki, (1, 2))
    w_wi = jnp.pad(w_wi, ((0, 0), (0, LANES - IDX_HEADS)))
    return jnp.concatenate([w_qr, w_qn, w_ckv, w_kr, w_qi, w_ki, w_wi], axis=1)


def _proj1_body(x_ref, g_ref, sh_ref, sc_ref, w_ref, gkv_ref,
                ct16_ref, sa16_ref, sb16_ref, ct64_ref, sa64_ref, sb64_ref,
                qa_ref, kp_ref, qi_ref, ki_ref, wi_ref):
    u = _norm_mod(x_ref[...], g_ref[...], sh_ref[0], sc_ref[0]).astype(BF16)
    t16 = (ct16_ref[...], sa16_ref[...], sb16_ref[...])
    t64 = (ct64_ref[...], sa64_ref[...], sb64_ref[...])
    qscale = HEAD_DIM ** -0.5 * math.log2(math.e)

    def cols(lo, hi):
        return _dot(u, w_ref[:, lo:hi])

    a = cols(P1_QR, P1_QN)
    for c in range(2):
        blk = _rope128(a[:, c * LANES:(c + 1) * LANES], *t16) * qscale
        qa_ref[:, c * LANES:(c + 1) * LANES] = blk.astype(BF16)
    qa_ref[:, P1_QN:P1_CKV] = (cols(P1_QN, P1_CKV) * qscale).astype(BF16)
    ckv = cols(P1_CKV, P1_KR)
    ms = jnp.mean(ckv * ckv, axis=-1, keepdims=True)
    kp_ref[:, 0:C_LATENT] = (ckv * lax.rsqrt(ms + NORM_EPS) * gkv_ref[...]).astype(BF16)
    a = cols(P1_KR, P1_QI)
    for c in range(2):
        blk = _rope128(a[:, c * LANES:(c + 1) * LANES], *t16)
        kp_ref[:, C_LATENT + c * LANES:C_LATENT + (c + 1) * LANES] = blk.astype(BF16)
    a = cols(P1_QI, P1_KI)
    for c in range(4):
        qi_ref[:, c * LANES:(c + 1) * LANES] = _rope128(a[:, c * LANES:(c + 1) * LANES], *t64).astype(BF16)
    ki_ref[...] = _rope128(cols(P1_KI, P1_WI), *t64).astype(BF16)
    wi_ref[...] = cols(P1_WI, P1_COLS) * (IDX_HEADS ** -0.5) * (IDX_DIM ** -0.5)


def _proj1(x2, g, shift, scale, w_bf, gkv, t16, t64, seq, tm=512):
    t, d = x2.shape
    tpb = seq // tm
    tab_spec = pl.BlockSpec((tm, LANES), lambda i: (i % tpb, 0))
    mod_spec = pl.BlockSpec((1, 1, d), lambda i: (i // tpb, 0, 0))
    widths = (P1_CKV, 2 * C_LATENT, IDX_HEADS * IDX_DIM, LANES, LANES)
    dtypes = (BF16, BF16, BF16, BF16, F32)
    return pl.pallas_call(
        _proj1_body,
        grid=(t // tm,),
        in_specs=[pl.BlockSpec((tm, d), lambda i: (i, 0)),
                  pl.BlockSpec((1, d), lambda i: (0, 0)),
                  mod_spec, mod_spec,
                  pl.BlockSpec((d, P1_COLS), lambda i: (0, 0)),
                  pl.BlockSpec((1, C_LATENT), lambda i: (0, 0))] + [tab_spec] * 6,
        out_specs=[pl.BlockSpec((tm, w), lambda i: (i, 0)) for w in widths],
        out_shape=[jax.ShapeDtypeStruct((t, w), dt) for w, dt in zip(widths, dtypes)],
        compiler_params=_cparams(("parallel",)),
        name="proj1",
    )(x2, g, shift, scale, w_bf, gkv, *t16, *t64)


IDX_KT = 512


def _idx_body(topk, qi_ref, ki_ref, wi_ref, mask_ref, sc_scr):
    i = pl.program_id(1)
    seq = ki_ref.shape[1]
    nkt = seq // IDX_KT
    nlive = (i * Q_BLOCK) // IDX_KT + 1
    q = qi_ref[0]
    wt = wi_ref[0].T
    lane = lax.broadcasted_iota(I32, (1, LANES), 1)
    qpos = i * Q_BLOCK + lane
    krow = lax.broadcasted_iota(I32, (IDX_KT, 1), 0)
    qpair = [jnp.concatenate([q[:, (2 * g) * LANES:(2 * g + 1) * LANES],
                              q[:, (2 * g + 1) * LANES:(2 * g + 2) * LANES]], axis=0) for g in range(2)]

    def score_tile(kt, c):
        kk = ki_ref[0, pl.ds(pl.multiple_of(kt * IDX_KT, IDX_KT), IDX_KT), :]
        zero = jnp.zeros_like(kk)
        kpart = [jnp.where(lane < IDX_DIM, kk, zero), jnp.where(lane >= IDX_DIM, kk, zero)]
        acc = jnp.zeros((IDX_KT, Q_BLOCK), F32)
        for g in range(2):
            for part in range(2):
                res = jnp.maximum(_dot_nt(kpart[part], qpair[g]), 0.0)
                for j in range(2):
                    h = 2 * (2 * g + j) + part
                    acc = acc + wt[h:h + 1, :] * res[:, j * Q_BLOCK:(j + 1) * Q_BLOCK]
        acc = acc + 0.0
        sc_scr[kt] = jnp.where(kt * IDX_KT + krow <= qpos, acc, -jnp.inf)
        return c

    lax.fori_loop(0, nlive, score_tile, 0)
    kq = jnp.minimum(qpos + 1, topk).astype(F32)

    def count(pred_fn):
        def body(kt, acc):
            ones = jnp.where(pred_fn(sc_scr[kt], kt), 1.0, 0.0)
            part = jnp.sum(ones.reshape(8, IDX_KT // 64, 8, Q_BLOCK), axis=1)
            return acc + jnp.sum(part, axis=0)
        part = lax.fori_loop(0, nlive, body, jnp.zeros((8, Q_BLOCK), F32))
        return jnp.sum(part, axis=0, keepdims=True)

    def key_to_f32(key):
        bits = jnp.where(key < 0, key ^ jnp.int32(0x7FFFFFFF), key)
        return lax.bitcast_convert_type(bits, F32)

    def bit_step(it, ans):
        cand = ans | lax.shift_left(jnp.int32(1), 31 - it)
        thr = key_to_f32(cand ^ jnp.int32(INT_MIN))
        cnt = count(lambda sc, kt: sc >= thr)
        return jnp.where(cnt >= kq, cand, ans)

    ans = lax.fori_loop(0, 32, bit_step, jnp.zeros((1, Q_BLOCK), I32))
    thr = key_to_f32(ans ^ jnp.int32(INT_MIN))
    n_ge = count(lambda sc, kt: sc >= thr)
    nbits = int(math.log2(seq))

    def tie_cut(_):
        need = kq - count(lambda sc, kt: sc > thr)

        def tie_step(it, ans2):
            cand = ans2 | lax.shift_left(jnp.int32(1), nbits - 1 - it)
            cnt = count(lambda sc, kt: (sc == thr) & (kt * IDX_KT + krow < cand))
            return jnp.where(cnt < need, cand, ans2)

        return lax.fori_loop(0, nbits, tie_step, jnp.zeros((1, Q_BLOCK), I32))

    excess = jnp.max(n_ge - kq) > 0.0
    jcut = lax.cond(excess, tie_cut, lambda _: jnp.full((1, Q_BLOCK), seq, I32), 0)

    def emit(kt, c):
        sc = sc_scr[kt]
        sel = (sc > thr) | ((sc == thr) & (kt * IDX_KT + krow <= jcut))
        mask_ref[0, 0, kt] = jnp.where(sel, 1.0, 0.0).T.astype(BF16)
        return c

    lax.fori_loop(0, nlive, emit, 0)

    def emit_dead(kt, c):
        mask_ref[0, 0, kt] = jnp.zeros((Q_BLOCK, IDX_KT), BF16)
        return c

    lax.fori_loop(nlive, nkt, emit_dead, 0)


def _dsa_select(qi, ki, wi, bn, seq):
    topk = min(TOPK_MAX, seq // 4)
    nqb = seq // Q_BLOCK
    nkt = seq // IDX_KT
    return pl.pallas_call(
        functools.partial(_idx_body, topk),
        grid=(bn, nqb),
        in_specs=[pl.BlockSpec((1, Q_BLOCK, IDX_HEADS * IDX_DIM), lambda b, i: (b, i, 0)),
                  pl.BlockSpec((1, seq, LANES), lambda b, i: (b, 0, 0)),
                  pl.BlockSpec((1, Q_BLOCK, LANES), lambda b, i: (b, i, 0))],
        out_specs=pl.BlockSpec((1, 1, nkt, Q_BLOCK, IDX_KT), lambda b, i: (b, i, 0, 0, 0)),
        out_shape=jax.ShapeDtypeStruct((bn, nqb, nkt, Q_BLOCK, IDX_KT), BF16),
        scratch_shapes=[pltpu.VMEM((nkt, IDX_KT, Q_BLOCK), F32)],
        compiler_params=_cparams(("parallel", "arbitrary")),
        name="dsa_select",
    )(qi.reshape(bn, seq, -1), ki.reshape(bn, seq, -1), wi.reshape(bn, seq, -1))


ATT_RB = 512


def _dsa_attn_body(qa_ref, kp_ref, mask_ref, wuk_ref, wuv_ref, o_ref, qp_scr, m_scr, l_scr, acc_scr,
                   s_scr):
    i = pl.program_id(1)
    rows = C_HEADS * Q_BLOCK
    qr = qa_ref[0, :, 0:P1_QN]
    lane = lax.broadcasted_iota(I32, (1, C_HEADS * ROPE_DIM), 1)
    for h in range(C_HEADS):
        p = h // 2
        qn = qa_ref[0, :, P1_QN + p * LANES:P1_QN + (p + 1) * LANES]
        qlat = _dot(qn, wuk_ref[h])
        qp_scr[h * Q_BLOCK:(h + 1) * Q_BLOCK, 0:C_LATENT] = qlat.astype(BF16)
        qp_scr[h * Q_BLOCK:(h + 1) * Q_BLOCK, C_LATENT:2 * C_LATENT] = jnp.where(
            (lane // ROPE_DIM) == h, qr, jnp.zeros_like(qr))
    m_scr[...] = jnp.full(m_scr.shape, NEG_BIG, F32)
    l_scr[...] = jnp.zeros_like(l_scr)
    acc_scr[...] = jnp.zeros_like(acc_scr)
    nblk = rows // ATT_RB

    def ktile(kt):
        return kp_ref[0, pl.ds(pl.multiple_of(kt * IDX_KT, IDX_KT), IDX_KT), :]

    def tile_bias(kt):
        bias = (mask_ref[0, 0, kt].astype(F32) - 1.0) * (-NEG_BIG)
        return jnp.concatenate([bias] * (ATT_RB // Q_BLOCK), axis=0)

    def softmax_pv(b, s, bias, vv):
        rs = slice(b * ATT_RB, (b + 1) * ATT_RB)
        s = s + bias
        m_prev = m_scr[rs, :]
        m_next = jnp.maximum(m_prev, jnp.max(s, axis=1, keepdims=True))
        alpha = jnp.exp2(m_prev - m_next)
        pr = jnp.exp2(s - jnp.concatenate([m_next] * (IDX_KT // LANES), axis=1))
        l_scr[rs, :] = alpha * l_scr[rs, :] + jnp.sum(pr, axis=1, keepdims=True)
        m_scr[rs, :] = m_next
        acc_scr[rs, :] = (jnp.concatenate([alpha] * (C_LATENT // LANES), axis=1) * acc_scr[rs, :]
                          + _dot(pr.astype(BF16), vv))

    kk0 = ktile(0)
    for b in range(nblk):
        s_scr[b * ATT_RB:(b + 1) * ATT_RB, :] = _dot_nt(qp_scr[b * ATT_RB:(b + 1) * ATT_RB, :], kk0)

    def step(kt, carry):
        kk = ktile(kt)
        kk_next = ktile(kt + 1)
        vv = kk[:, 0:C_LATENT]
        bias = tile_bias(kt)
        for b in range(nblk):
            rs = slice(b * ATT_RB, (b + 1) * ATT_RB)
            s = s_scr[rs, :]
            s_scr[rs, :] = _dot_nt(qp_scr[rs, :], kk_next)
            softmax_pv(b, s, bias, vv)
        return carry

    last = (i * Q_BLOCK) // IDX_KT
    lax.fori_loop(0, last, step, 0)
    vv = ktile(last)[:, 0:C_LATENT]
    bias = tile_bias(last)
    for b in range(nblk):
        softmax_pv(b, s_scr[b * ATT_RB:(b + 1) * ATT_RB, :], bias, vv)
    olat = (acc_scr[...] / jnp.concatenate([l_scr[...]] * (C_LATENT // LANES), axis=1)).astype(BF16)
    for p in range(C_HEADS // 2):
        o = (_dot(olat[(2 * p) * Q_BLOCK:(2 * p + 1) * Q_BLOCK], wuv_ref[2 * p])
             + _dot(olat[(2 * p + 1) * Q_BLOCK:(2 * p + 2) * Q_BLOCK], wuv_ref[2 * p + 1]))
        o_ref[0, :, p * LANES:(p + 1) * LANES] = o.astype(o_ref.dtype)


def _dsa_attention(qa, kp, mask, wuk_x, wuv_x, bn, seq):
    nqb = seq // Q_BLOCK
    nkt = seq // IDX_KT
    rows = C_HEADS * Q_BLOCK
    return pl.pallas_call(
        _dsa_attn_body,
        grid=(bn, nqb),
        in_specs=[pl.BlockSpec((1, Q_BLOCK, P1_CKV), lambda b, i: (b, i, 0)),
                  pl.BlockSpec((1, seq, 2 * C_LATENT), lambda b, i: (b, 0, 0)),
                  pl.BlockSpec((1, 1, nkt, Q_BLOCK, IDX_KT), lambda b, i: (b, i, 0, 0, 0)),
                  pl.BlockSpec((C_HEADS, LANES, C_LATENT), lambda b, i: (0, 0, 0)),
                  pl.BlockSpec((C_HEADS, C_LATENT, LANES), lambda b, i: (0, 0, 0))],
        out_specs=pl.BlockSpec((1, Q_BLOCK, C_HEADS * HEAD_DIM), lambda b, i: (b, i, 0)),
        out_shape=jax.ShapeDtypeStruct((bn, seq, C_HEADS * HEAD_DIM), BF16),
        scratch_shapes=[pltpu.VMEM((rows, 2 * C_LATENT), BF16),
                        pltpu.VMEM((rows, LANES), F32), pltpu.VMEM((rows, LANES), F32),
                        pltpu.VMEM((rows, C_LATENT), F32),
                        pltpu.VMEM((rows, IDX_KT), F32)],
        compiler_params=_cparams(("parallel", "arbitrary")),
        name="dsa_attn",
    )(qa.reshape(bn, seq, -1), kp.reshape(bn, seq, -1), mask, wuk_x, wuv_x)


def _mla_weights(w_uk, w_uv):
    wuk = jnp.transpose(w_uk, (1, 2, 0))
    wuk_x = jnp.zeros((C_HEADS, LANES, C_LATENT), F32)
    wuv = jnp.transpose(w_uv, (1, 0, 2))
    wuv_x = jnp.zeros((C_HEADS, C_LATENT, LANES), F32)
    for h in range(C_HEADS):
        o = (h % 2) * HEAD_DIM
        wuk_x = wuk_x.at[h, o:o + C_NOPE, :].set(wuk[h])
        wuv_x = wuv_x.at[h, :, o:o + HEAD_DIM].set(wuv[h])
    return wuk_x.astype(BF16), wuv_x.astype(BF16)


def _router_body(h_ref, g_ref, sh_ref, sc_ref, wr_ref, br_ref, route_ref, up_ref, cnt_ref, carry_scr):
    i = pl.program_id(0)
    tm = h_ref.shape[0]

    @pl.when(i == 0)
    def _():
        carry_scr[...] = jnp.zeros_like(carry_scr)

    u = _norm_mod(h_ref[...], g_ref[...], sh_ref[0], sc_ref[0])
    logits = jnp.dot(u, wr_ref[...], preferred_element_type=F32, precision=HIGHEST) + br_ref[...]
    lane = lax.broadcasted_iota(I32, (tm, LANES), 1).astype(F32)
    m1 = jnp.max(logits, axis=1, keepdims=True)
    e1 = jnp.min(jnp.where(logits == m1, lane, float(LANES)), axis=1, keepdims=True)
    rest = jnp.where(lane == e1, NEG_BIG * 2, logits)
    m2 = jnp.max(rest, axis=1, keepdims=True)
    e2 = jnp.min(jnp.where(rest == m2, lane, float(LANES)), axis=1, keepdims=True)
    ex = jnp.exp(m2 - m1)
    g1 = 1.0 / (1.0 + ex)
    g2 = ex / (1.0 + ex)
    onehot = ((lane == e1) | (lane == e2))
    oh_bf = jnp.where(onehot, 1.0, 0.0).astype(BF16)
    r = lax.broadcasted_iota(I32, (tm, tm), 0)
    c = lax.broadcasted_iota(I32, (tm, tm), 1)
    tri = jnp.where(c < r, 1.0, 0.0).astype(BF16)
    prefix = _dot(tri, oh_bf) + carry_scr[...]
    rank1 = jnp.sum(jnp.where(lane == e1, prefix, 0.0), axis=1, keepdims=True)
    rank2 = jnp.sum(jnp.where(lane == e2, prefix, 0.0), axis=1, keepdims=True)
    carry_scr[...] = carry_scr[...] + jnp.sum(jnp.where(onehot, 1.0, 0.0), axis=0, keepdims=True)
    cnt_ref[...] = carry_scr[...]
    vals = [e1, e2, g1, g2, rank1, rank2]
    route = jnp.zeros((tm, LANES), F32)
    for k, v in enumerate(vals):
        route = jnp.where(lane == float(k), v, route)
    route_ref[...] = route
    up_ref[...] = u


def _router(h2, g, shift, scale, w_router, b_router, seq, tm=512):
    t, d = h2.shape
    tpb = seq // tm
    wr = jnp.pad(w_router.astype(F32), ((0, 0), (0, LANES - N_EXPERTS)))
    br = jnp.pad(b_router.astype(F32), (0, LANES - N_EXPERTS), constant_values=NEG_BIG).reshape(1, LANES)
    mod_spec = pl.BlockSpec((1, 1, d), lambda i: (i // tpb, 0, 0))
    return pl.pallas_call(
        _router_body,
        grid=(t // tm,),
        in_specs=[pl.BlockSpec((tm, d), lambda i: (i, 0)),
                  pl.BlockSpec((1, d), lambda i: (0, 0)),
                  mod_spec, mod_spec,
                  pl.BlockSpec((d, LANES), lambda i: (0, 0)),
                  pl.BlockSpec((1, LANES), lambda i: (0, 0))],
        out_specs=[pl.BlockSpec((tm, LANES), lambda i: (i, 0)),
                   pl.BlockSpec((tm, d), lambda i: (i, 0)),
                   pl.BlockSpec((1, LANES), lambda i: (0, 0))],
        out_shape=[jax.ShapeDtypeStruct((t, LANES), F32),
                   jax.ShapeDtypeStruct((t, d), F32),
                   jax.ShapeDtypeStruct((1, LANES), F32)],
        scratch_shapes=[pltpu.VMEM((1, LANES), F32)],
        compiler_params=_cparams(("arbitrary",)),
        name="moe_router",
    )(h2, g, shift, scale, wr, br)


def _dispatch_body(p1_ref, p2_ref, up_ref, xs_in_ref, xs_ref, sem):
    del xs_in_ref
    tm = up_ref.shape[0]

    def copy(r, dst):
        return pltpu.make_async_copy(up_ref.at[pl.ds(r, 1)], xs_ref.at[pl.ds(dst, 1)], sem)

    def start(r, c):
        copy(r, p1_ref[r]).start(priority=0)
        copy(r, p2_ref[r]).start(priority=1)
        return c

    lax.fori_loop(0, tm, start, 0, unroll=DMA_UNROLL)

    def wait(r, c):
        copy(r, p1_ref[r]).wait()
        copy(r, p2_ref[r]).wait()
        return c

    lax.fori_loop(0, tm, wait, 0, unroll=DMA_UNROLL)


def _dispatch(pos1, pos2, up, nrows, tm=256):
    t, w = up.shape
    xs0 = jnp.zeros((nrows, w), F32)
    smem_spec = pl.BlockSpec((tm,), lambda i: (i,), memory_space=pltpu.SMEM)
    return pl.pallas_call(
        _dispatch_body,
        grid=(t // tm,),
        in_specs=[smem_spec, smem_spec,
                  pl.BlockSpec((tm, w), lambda i: (i, 0)),
                  pl.BlockSpec(memory_space=pl.ANY)],
        out_specs=pl.BlockSpec(memory_space=pl.ANY),
        out_shape=jax.ShapeDtypeStruct((nrows, w), F32),
        scratch_shapes=[pltpu.SemaphoreType.DMA(())],
        input_output_aliases={3: 0},
        compiler_params=_cparams(("arbitrary",)),
        name="moe_dispatch",
    )(pos1, pos2, up, xs0)


def _expert_body(te_ref, nv_ref, xs_ref, wg_ref, wu_ref, wd_ref, y_ref, x_scr, acc_scr):
    j = pl.program_id(0)
    f = pl.program_id(1)
    live = j < nv_ref[0]

    @pl.when(f == 0)
    def _():
        x_scr[...] = xs_ref[...].astype(BF16)
        acc_scr[...] = jnp.zeros_like(acc_scr)

    @pl.when(live)
    def _():
        x = x_scr[...]
        gp = _dot(x, wg_ref[0])
        up = _dot(x, wu_ref[0])
        a = (gp * jax.nn.sigmoid(gp) * up).astype(BF16)
        acc_scr[...] += _dot(a, wd_ref[0])

    @pl.when(f == pl.num_programs(1) - 1)
    def _():
        y_ref[...] = acc_scr[...]


def _experts(tile_expert, n_live, xs, wg, wu, wd, tm, tf=MOE_TF):
    nrows, d = xs.shape
    nf = wg.shape[2] // tf
    nt = nrows // tm

    def f_eff(j, f, nv):
        return jnp.where(j < nv[0], f, nf - 1)

    grid_spec = pltpu.PrefetchScalarGridSpec(
        num_scalar_prefetch=2,
        grid=(nt, nf),
        in_specs=[pl.BlockSpec((tm, d), lambda j, f, te, nv: (j, 0)),
                  pl.BlockSpec((1, d, tf), lambda j, f, te, nv: (te[j], 0, f_eff(j, f, nv))),
                  pl.BlockSpec((1, d, tf), lambda j, f, te, nv: (te[j], 0, f_eff(j, f, nv))),
                  pl.BlockSpec((1, tf, d), lambda j, f, te, nv: (te[j], f_eff(j, f, nv), 0))],
        out_specs=pl.BlockSpec((tm, d), lambda j, f, te, nv: (j, 0)),
        scratch_shapes=[pltpu.VMEM((tm, d), BF16), pltpu.VMEM((tm, d), F32)])
    return pl.pallas_call(
        _expert_body,
        grid_spec=grid_spec,
        out_shape=jax.ShapeDtypeStruct((nrows, d), F32),
        compiler_params=_cparams(("arbitrary", "arbitrary")),
        name="moe_experts",
    )(tile_expert, n_live, xs, wg, wu, wd)


def _combine_body(p1_ref, p2_ref, route_ref, h_ref, gt_ref, gf_ref, y_ref, o_ref, y1_scr, y2_scr, sem):
    tm = h_ref.shape[0]

    def copy(src, dst_scr, r):
        return pltpu.make_async_copy(y_ref.at[pl.ds(src, 1)], dst_scr.at[pl.ds(r, 1)], sem)

    def start(r, c):
        copy(p1_ref[r], y1_scr, r).start(priority=0)
        copy(p2_ref[r], y2_scr, r).start(priority=1)
        return c

    lax.fori_loop(0, tm, start, 0, unroll=DMA_UNROLL)

    def wait(r, c):
        copy(p1_ref[r], y1_scr, r).wait()
        copy(p2_ref[r], y2_scr, r).wait()
        return c

    lax.fori_loop(0, tm, wait, 0, unroll=DMA_UNROLL)
    route = route_ref[...]
    g1, g2 = route[:, 2:3], route[:, 3:4]
    y = g1 * y1_scr[...] + g2 * y2_scr[...]
    hn = h_ref[...] + gt_ref[0] * y
    ms = jnp.mean(hn * hn, axis=-1, keepdims=True)
    o_ref[...] = hn * lax.rsqrt(ms + NORM_EPS) * gf_ref[...]


def _combine(pos1, pos2, route, h2, gate, g_final, y_sorted, seq, tm=256):
    t, d = h2.shape
    tpb = seq // tm
    smem_spec = pl.BlockSpec((tm,), lambda i: (i,), memory_space=pltpu.SMEM)
    return pl.pallas_call(
        _combine_body,
        grid=(t // tm,),
        in_specs=[smem_spec, smem_spec,
                  pl.BlockSpec((tm, LANES), lambda i: (i, 0)),
                  pl.BlockSpec((tm, d), lambda i: (i, 0)),
                  pl.BlockSpec((1, 1, d), lambda i: (i // tpb, 0, 0)),
                  pl.BlockSpec((1, d), lambda i: (0, 0)),
                  pl.BlockSpec(memory_space=pl.ANY)],
        out_specs=pl.BlockSpec((tm, d), lambda i: (i, 0)),
        out_shape=jax.ShapeDtypeStruct((t, d), F32),
        scratch_shapes=[pltpu.VMEM((tm, d), F32), pltpu.VMEM((tm, d), F32),
                        pltpu.SemaphoreType.DMA(())],
        compiler_params=_cparams(("arbitrary",)),
        name="moe_combine",
    )(pos1, pos2, route, h2, gate, g_final, y_sorted)


MOE_TILE = 512


def _moe_layout(route, counts):
    e1 = route[:, 0].astype(I32)
    e2 = route[:, 1].astype(I32)
    r1 = route[:, 4].astype(I32)
    r2 = route[:, 5].astype(I32)
    cnt = counts[0, :N_EXPERTS].astype(I32)
    tiles = (cnt + MOE_TILE - 1) // MOE_TILE
    tile_end = jnp.cumsum(tiles)
    start = (tile_end - tiles) * MOE_TILE
    pos1 = start[e1] + r1
    pos2 = start[e2] + r2
    nt = route.shape[0] * 2 // MOE_TILE + N_EXPERTS
    n_live = tile_end[-1]
    tid = jnp.minimum(jnp.arange(nt, dtype=I32), n_live - 1)
    tile_expert = jnp.sum((tid[:, None] >= tile_end[None, :]).astype(I32), axis=1)
    return pos1, pos2, tile_expert.astype(I32), n_live.reshape(1).astype(I32), nt * MOE_TILE


def kernel(x, c, w_ada, b_ada, g_mix, g_ffn, g_final, e_w_in, e_w_out, s5_a_re, s5_a_im, s5_log_dt,
           s5_b_re, s5_b_im, s5_c_re, s5_c_im, s5_d, s5_w_glu, s5_b_glu, ff_w_gate, ff_w_up,
           ff_w_down, o_w_in, o_w_out, mla_g_kv, mla_w_uk, mla_w_uv, moe_w_router, moe_b_router,
           moe_w_gate, moe_w_up, moe_w_down):
    bn, seq, d = x.shape
    t = bn * seq
    mod = _adaln(c, w_ada, b_ada)

    def mod_vec(layer, k):
        return mod[layer, :, k * d:(k + 1) * d].reshape(bn, 1, d)

    tabs64 = _rope_tables(seq, HEAD_DIM)
    tabs16 = _rope_tables(seq, ROPE_DIM)
    h = x.reshape(t, d)

    *qkv_views, s_in = _proj0(h, g_mix[0].reshape(1, d), mod_vec(0, 0), mod_vec(0, 1),
                              e_w_in[0].astype(BF16), tabs64, seq)
    y_a = _dilated_attention(qkv_views, bn, seq)
    prm = _s5_prepare(s5_a_re[0], s5_a_im[0], s5_log_dt[0], s5_b_re[0], s5_b_im[0],
                      s5_c_re[0], s5_c_im[0], s5_d[0], seq // S5_CHUNK)
    y_b = _s5_mixer(s_in, prm, s5_w_glu[0].astype(BF16), s5_b_glu[0], bn, seq)
    h = _outproj([y_a], [y_b], e_w_out[0].astype(BF16), h, mod_vec(0, 2), seq)
    h = _ffn_dense(h, g_ffn[0].reshape(1, d), mod_vec(0, 3), mod_vec(0, 4), mod_vec(0, 5),
                   ff_w_gate[0].astype(BF16), ff_w_up[0].astype(BF16), ff_w_down[0].astype(BF16), seq)

    w1 = _proj1_layout(o_w_in[0]).astype(BF16)
    qa, kp, qi, ki, wi = _proj1(h, g_mix[1].reshape(1, d), mod_vec(1, 0), mod_vec(1, 1), w1,
                                mla_g_kv[0].reshape(1, C_LATENT).astype(F32), tabs16, tabs64, seq)
    mask = _dsa_select(qi, ki, wi, bn, seq)
    wuk_x, wuv_x = _mla_weights(mla_w_uk[0], mla_w_uv[0])
    o_attn = _dsa_attention(qa, kp, mask, wuk_x, wuv_x, bn, seq)
    h = _outproj([o_attn.reshape(t, C_HEADS * HEAD_DIM)], [], o_w_out[0].astype(BF16), h, mod_vec(1, 2), seq)

    route, up, counts = _router(h, g_ffn[1].reshape(1, d), mod_vec(1, 3), mod_vec(1, 4),
                                moe_w_router[0], moe_b_router[0], seq)
    pos1, pos2, tile_expert, n_live, nrows = _moe_layout(route, counts)
    xs = _dispatch(pos1, pos2, up, nrows)
    y_sorted = _experts(tile_expert, n_live, xs, moe_w_gate[0].astype(BF16), moe_w_up[0].astype(BF16),
                        moe_w_down[0].astype(BF16), MOE_TILE)
    out = _combine(pos1, pos2, route, h, mod_vec(1, 5), g_final.reshape(1, d), y_sorted, seq)
    return out.reshape(bn, seq, d)
```

```python
import functools
import math

import jax
import jax.numpy as jnp
import numpy as np
from jax import lax
from jax.experimental import pallas as pl
from jax.experimental.pallas import tpu as pltpu

F32 = jnp.float32
BF16 = jnp.bfloat16
I32 = jnp.int32
HIGHEST = lax.Precision.HIGHEST

D_MODEL = 1024
HEAD_DIM = 64
ROPE_DIM = 16
ROPE_THETA = 500000.0
NORM_EPS = 1e-6
A_HEADS = 8
A_WIDTH = 512
A_PATTERNS = ((128, 1), (512, 4), (2048, 16))
B_WIDTH = 512
B_GROUP = 16
B_GROUPS = 32
B_STATE = 64
C_HEADS = 16
C_NOPE = 48
C_LATENT = 256
IDX_HEADS = 8
IDX_DIM = 64
TOPK_MAX = 256
FF_DENSE = 2816
N_EXPERTS = 8
FF_EXPERT = 3584

LANES = 128
Q_BLOCK = 128
S5_CHUNK = 16
MOE_TF = 512
FFN_TF = 256
DMA_UNROLL = 8
VMEM_LIMIT = 56 * 1024 * 1024
NEG_BIG = -1e30
INT_MIN = -(2 ** 31)


def _cparams(sem, vmem=VMEM_LIMIT):
    return pltpu.CompilerParams(dimension_semantics=sem, vmem_limit_bytes=vmem)


def _dot(a, b):
    return jnp.dot(a, b, preferred_element_type=F32)


def _dot_nt(a, b):
    return lax.dot_general(a, b, (((1,), (1,)), ((), ())), preferred_element_type=F32)


def _norm_mod(x, g, shift, scale):
    ms = jnp.mean(x * x, axis=-1, keepdims=True)
    y = x * lax.rsqrt(ms + NORM_EPS) * g
    return y * (1.0 + scale) + shift


def _rope128(x, ct, sa, sb):
    return x * ct + pltpu.roll(x, LANES - ROPE_DIM // 2, 1) * sa + pltpu.roll(x, ROPE_DIM // 2, 1) * sb


def _adaln_body(c_ref, w_ref, b_ref, o_ref):
    c = c_ref[...]
    ca = c * jax.nn.sigmoid(c)
    o_ref[0] = jnp.dot(ca, w_ref[0], preferred_element_type=F32, precision=HIGHEST) + b_ref[0]


def _adaln(c, w_ada, b_ada):
    depth, d, d6 = w_ada.shape
    bn = c.shape[0]
    rows = 8
    cp = jnp.zeros((rows, d), F32).at[:bn].set(c)
    tn = 1536
    out = pl.pallas_call(
        _adaln_body,
        grid=(depth, d6 // tn),
        in_specs=[pl.BlockSpec((rows, d), lambda l, j: (0, 0)),
                  pl.BlockSpec((1, d, tn), lambda l, j: (l, 0, j)),
                  pl.BlockSpec((1, 1, tn), lambda l, j: (l, 0, j))],
        out_specs=pl.BlockSpec((1, rows, tn), lambda l, j: (l, 0, j)),
        out_shape=jax.ShapeDtypeStruct((depth, rows, d6), F32),
        compiler_params=_cparams(("arbitrary", "arbitrary")),
        name="adaln",
    )(cp, w_ada, b_ada.reshape(depth, 1, d6))
    return out[:, :bn]


def _rope_tables(seq, period):
    half = ROPE_DIM // 2
    pos = jnp.arange(seq, dtype=F32)
    inv = ROPE_THETA ** (-jnp.arange(0, ROPE_DIM, 2, dtype=F32) / ROPE_DIM)
    ang = pos[:, None] * inv[None, :]
    cos, sin = jnp.cos(ang), jnp.sin(ang)
    lane = np.arange(LANES) % period
    first = lane < half
    second = (lane >= half) & (lane < ROPE_DIM)
    idx = np.where(first, lane, np.where(second, lane - half, 0))
    cos_l, sin_l = cos[:, idx], sin[:, idx]
    ct = jnp.where(first | second, cos_l, 1.0)
    sa = jnp.where(first, -sin_l, 0.0)
    sb = jnp.where(second, sin_l, 0.0)
    return ct, sa, sb


def _proj0_body(x_ref, g_ref, sh_ref, sc_ref, w_ref, ct_ref, sa_ref, sb_ref, *rest):
    qkv_refs, s_ref, acc_scr = rest[:-2], rest[-2], rest[-1]
    tm = x_ref.shape[0]
    qw = 3 * A_WIDTH
    u = _norm_mod(x_ref[...], g_ref[...], sh_ref[0], sc_ref[0]).astype(BF16)
    ct, sa, sb = ct_ref[...], sa_ref[...], sb_ref[...]
    for j in range(3):
        acc = _dot(u, w_ref[:, j * A_WIDTH:(j + 1) * A_WIDTH])
        for c in range(A_WIDTH // LANES):
            a = acc[:, c * LANES:(c + 1) * LANES]
            if j < 2:
                a = _rope128(a, ct, sa, sb)
            if j == 0:
                a = a * (HEAD_DIM ** -0.5)
            acc_scr[j * (A_WIDTH // LANES) + c] = a
    acc = _dot(u, w_ref[:, qw:])
    nq = qw // LANES
    for c in range(B_WIDTH // LANES):
        acc_scr[nq + c] = acc[:, c * LANES:(c + 1) * LANES]
    for (_, dil), ref in zip(A_PATTERNS, qkv_refs):
        for r in range(dil):
            for c in range(nq):
                ref[:, r * qw + c * LANES:r * qw + (c + 1) * LANES] = (
                    acc_scr[c, pl.ds(r, tm // dil, stride=dil), :].astype(BF16))
    for r in range(S5_CHUNK):
        for c in range(B_WIDTH // LANES):
            s_ref[:, r * B_WIDTH + c * LANES:r * B_WIDTH + (c + 1) * LANES] = (
                acc_scr[nq + c, pl.ds(r, tm // S5_CHUNK, stride=S5_CHUNK), :])


def _proj0(x2, g, shift, scale, w_bf, tabs, seq, tm=512):
    t, d = x2.shape
    n = w_bf.shape[1]
    tpb = seq // tm
    ct, sa, sb = tabs
    tab_spec = pl.BlockSpec((tm, LANES), lambda i: (i % tpb, 0))
    mod_spec = pl.BlockSpec((1, 1, d), lambda i: (i // tpb, 0, 0))
    qw = 3 * A_WIDTH
    dils = [dil for _, dil in A_PATTERNS]
    out_specs = [pl.BlockSpec((tm // dil, dil * qw), lambda i: (i, 0)) for dil in dils]
    out_shape = [jax.ShapeDtypeStruct((t // dil, dil * qw), BF16) for dil in dils]
    out_specs.append(pl.BlockSpec((tm // S5_CHUNK, S5_CHUNK * B_WIDTH), lambda i: (i, 0)))
    out_shape.append(jax.ShapeDtypeStruct((t // S5_CHUNK, S5_CHUNK * B_WIDTH), F32))
    return pl.pallas_call(
        _proj0_body,
        grid=(t // tm,),
        in_specs=[pl.BlockSpec((tm, d), lambda i: (i, 0)),
                  pl.BlockSpec((1, d), lambda i: (0, 0)),
                  mod_spec, mod_spec,
                  pl.BlockSpec((d, n), lambda i: (0, 0)),
                  tab_spec, tab_spec, tab_spec],
        out_specs=out_specs,
        out_shape=out_shape,
        scratch_shapes=[pltpu.VMEM((n // LANES, tm, LANES), F32)],
        compiler_params=_cparams(("parallel",)),
        name="proj0",
    )(x2, g, shift, scale, w_bf, ct, sa, sb)


DIL_QB_MAX = 4


def _dil_body(q_ref, kp_ref, kc_ref, vp_ref, vc_ref, lse_ref, o_ref):
    i = pl.program_id(2)
    nqb = q_ref.shape[1] // Q_BLOCK
    kall = jnp.concatenate([kp_ref[0], kc_ref[0]], axis=0)
    vall = jnp.concatenate([vp_ref[0], vc_ref[0]], axis=0)
    row = lax.broadcasted_iota(I32, (Q_BLOCK, 2 * Q_BLOCK), 0)
    col = lax.broadcasted_iota(I32, (Q_BLOCK, 2 * Q_BLOCK), 1)
    rel = row + Q_BLOCK - col
    band = (rel >= 0) & (rel <= Q_BLOCK)
    lane = lax.broadcasted_iota(I32, (1, LANES), 1)
    npair = A_WIDTH // LANES
    hpp = LANES // HEAD_DIM
    scores = []
    for j in range(nqb):
        valid = band if j > 0 else band & ((col >= Q_BLOCK) | (i > 0))
        qj = q_ref[0, j * Q_BLOCK:(j + 1) * Q_BLOCK, :]
        kj = kall[j * Q_BLOCK:(j + 2) * Q_BLOCK]
        for p in range(npair):
            sl = slice(p * LANES, (p + 1) * LANES)
            for hh in range(hpp):
                hm = (lane // HEAD_DIM) == hh
                qh = jnp.where(hm, qj[:, sl], jnp.zeros_like(qj[:, sl]))
                scores.append(jnp.where(valid, _dot_nt(qh, kj[:, sl]), -jnp.inf))
    s = jnp.concatenate(scores, axis=0)
    m = jnp.max(s, axis=1, keepdims=True)
    pr = jnp.exp(s - m)
    l = jnp.sum(pr, axis=1, keepdims=True)
    prb = pr.astype(BF16)
    for j in range(nqb):
        vj = vall[j * Q_BLOCK:(j + 2) * Q_BLOCK]
        qs = slice(j * Q_BLOCK, (j + 1) * Q_BLOCK)
        for p in range(npair):
            sl = slice(p * LANES, (p + 1) * LANES)
            m_pair = l_pair = o_pair = None
            for hh in range(hpp):
                h = (j * npair + p) * hpp + hh
                rs = slice(h * Q_BLOCK, (h + 1) * Q_BLOCK)
                hm = (lane // HEAD_DIM) == hh
                o = _dot(prb[rs], vj[:, sl])
                if hh == 0:
                    m_pair = jnp.broadcast_to(m[rs], (Q_BLOCK, LANES))
                    l_pair = jnp.broadcast_to(l[rs], (Q_BLOCK, LANES))
                    o_pair = o
                else:
                    m_pair = jnp.where(hm, m[rs], m_pair)
                    l_pair = jnp.where(hm, l[rs], l_pair)
                    o_pair = jnp.where(hm, o, o_pair)
            lse_ref[0, qs, sl] = m_pair + jnp.log(l_pair)
            o_ref[0, qs, sl] = (o_pair / l_pair).astype(o_ref.dtype)


def _dilated_branch(qkv_d, dil, bn, seq):
    sd = seq // dil
    nqb = min(DIL_QB_MAX, sd // Q_BLOCK)
    nb = sd // (nqb * Q_BLOCK)
    cur = (1, nqb * Q_BLOCK, A_WIDTH)
    prev = (1, Q_BLOCK, A_WIDTH)
    q_spec = pl.BlockSpec(cur, lambda b, r, i: (b, i, r * 3))
    kp_spec = pl.BlockSpec(prev, lambda b, r, i: (b, jnp.maximum(nqb * i - 1, 0), r * 3 + 1))
    kc_spec = pl.BlockSpec(cur, lambda b, r, i: (b, i, r * 3 + 1))
    vp_spec = pl.BlockSpec(prev, lambda b, r, i: (b, jnp.maximum(nqb * i - 1, 0), r * 3 + 2))
    vc_spec = pl.BlockSpec(cur, lambda b, r, i: (b, i, r * 3 + 2))
    st_spec = pl.BlockSpec(cur, lambda b, r, i: (b, i, r))
    lse, out = pl.pallas_call(
        _dil_body,
        grid=(bn, dil, nb),
        in_specs=[q_spec, kp_spec, kc_spec, vp_spec, vc_spec],
        out_specs=[st_spec] * 2,
        out_shape=[jax.ShapeDtypeStruct((bn, sd, dil * A_WIDTH), F32),
                   jax.ShapeDtypeStruct((bn, sd, dil * A_WIDTH), BF16)],
        compiler_params=_cparams(("parallel", "parallel", "arbitrary")),
        name=f"dilated_d{dil}",
    )(*([qkv_d.reshape(bn, sd, dil * 3 * A_WIDTH)] * 5))
    return [lse.reshape(bn * sd, dil * A_WIDTH), out.reshape(bn * sd, dil * A_WIDTH)]


def _dil_merge_body(*refs):
    nbr = len(A_PATTERNS)
    stats = refs[:2 * nbr]
    y_ref = refs[2 * nbr]
    scr = refs[2 * nbr + 1:]
    tm = y_ref.shape[0]
    nc = A_WIDTH // LANES
    lses, outs = [], []
    si = 0
    for gi, (_, dil) in enumerate(A_PATTERNS):
        pair = []
        for ref in stats[2 * gi:2 * gi + 2]:
            if dil == 1:
                pair.append(ref[...].astype(F32))
            else:
                for r in range(dil):
                    for c in range(nc):
                        scr[si][c, pl.ds(r, tm // dil, stride=dil), :] = (
                            ref[:, r * A_WIDTH + c * LANES:r * A_WIDTH + (c + 1) * LANES].astype(F32))
                pair.append(jnp.concatenate([scr[si][c] for c in range(nc)], axis=1))
                si += 1
        lses.append(pair[0])
        outs.append(pair[1])
    top = functools.reduce(jnp.maximum, lses)
    den = jnp.zeros_like(top)
    num = jnp.zeros_like(top)
    for lse, o in zip(lses, outs):
        w = jnp.exp(lse - top)
        den = den + w
        num = num + w * o
    y_ref[...] = (num / den).astype(y_ref.dtype)


def _dilated_attention(qkv_views, bn, seq, tm=512):
    t = bn * seq
    stats, in_specs = [], []
    nscr = 0
    for (_, dil), qkv_d in zip(A_PATTERNS, qkv_views):
        stats += _dilated_branch(qkv_d, dil, bn, seq)
        in_specs += [pl.BlockSpec((tm // dil, dil * A_WIDTH), lambda i: (i, 0))] * 2
        nscr += 2 if dil > 1 else 0
    return pl.pallas_call(
        _dil_merge_body,
        grid=(t // tm,),
        in_specs=in_specs,
        out_specs=pl.BlockSpec((tm, A_WIDTH), lambda i: (i, 0)),
        out_shape=jax.ShapeDtypeStruct((t, A_WIDTH), BF16),
        scratch_shapes=[pltpu.VMEM((A_WIDTH // LANES, tm, LANES), F32)] * nscr,
        compiler_params=_cparams(("parallel",)),
        name="dilated_merge",
    )(*stats)


def _s5_prepare(a_re, a_im, log_dt, b_re, b_im, c_re, c_im, d_skip, nchunk):
    dt = jnp.exp(log_dt.astype(F32))[:, None]
    lr, li = a_re.astype(F32), a_im.astype(F32)
    mag = jnp.exp(lr * dt)
    ar = mag * jnp.cos(li * dt)
    ai = mag * jnp.sin(li * dt)
    den = lr * lr + li * li
    nr = ar - 1.0
    fr = (nr * lr + ai * li) / den
    fi = (ai * lr - nr * li) / den
    br, bi = b_re.astype(F32), b_im.astype(F32)
    bbr = fr[..., None] * br - fi[..., None] * bi
    bbi = fr[..., None] * bi + fi[..., None] * br
    nblk = B_WIDTH // LANES
    gpb = B_GROUPS // nblk
    eye = jnp.eye(gpb, dtype=F32)

    def bdiag_in(m):
        m = jnp.swapaxes(m.reshape(nblk, gpb, B_STATE, B_GROUP), 2, 3)
        m = m[:, :, :, None, :] * eye[None, :, None, :, None]
        return m.reshape(nblk, gpb * B_GROUP, gpb * B_STATE)

    def bdiag_out(m):
        m = jnp.swapaxes(m.reshape(nblk, gpb, B_GROUP, B_STATE), 2, 3)
        m = m[:, :, :, None, :] * eye[None, :, None, :, None]
        return m.reshape(nblk, gpb * B_STATE, gpb * B_GROUP)

    pr, pi = ar, ai
    for _ in range(int(math.log2(S5_CHUNK))):
        pr, pi = pr * pr - pi * pi, 2.0 * pr * pi
    pows_r, pows_i = [], []
    for _ in range(max(1, int(math.ceil(math.log2(nchunk))))):
        pows_r.append(pr.reshape(1, -1))
        pows_i.append(pi.reshape(1, -1))
        pr, pi = pr * pr - pi * pi, 2.0 * pr * pi
    return dict(
        ar=ar.reshape(1, -1), ai=ai.reshape(1, -1),
        b_re=bdiag_in(bbr).astype(BF16), b_im=bdiag_in(bbi).astype(BF16),
        c_re=bdiag_out(c_re.astype(F32)).astype(BF16), c_im=bdiag_out(-c_im.astype(F32)).astype(BF16),
        d=d_skip.astype(F32).reshape(1, B_WIDTH),
        pows_r=jnp.concatenate(pows_r, axis=0), pows_i=jnp.concatenate(pows_i, axis=0))


def _s5_local_body(u_ref, bre_ref, bim_ref, ar_ref, ai_ref, sre_ref, sim_ref):
    tr = u_ref.shape[0]
    sw = bre_ref.shape[2]
    for blk in range(B_WIDTH // LANES):
        arb = ar_ref[:, blk * sw:(blk + 1) * sw]
        aib = ai_ref[:, blk * sw:(blk + 1) * sw]
        sr = jnp.zeros((tr, sw), F32)
        si = jnp.zeros((tr, sw), F32)
        for t in range(S5_CHUNK):
            ub = u_ref[:, t * B_WIDTH + blk * LANES:t * B_WIDTH + (blk + 1) * LANES].astype(BF16)
            sr, si = (arb * sr - aib * si + _dot(ub, bre_ref[blk]),
                      arb * si + aib * sr + _dot(ub, bim_ref[blk]))
        sre_ref[:, blk * sw:(blk + 1) * sw] = sr
        sim_ref[:, blk * sw:(blk + 1) * sw] = si


def _s5_scan_body(nsteps, sre_ref, sim_ref, pr_ref, pi_ref, xre_ref, xim_ref):
    nc = sre_ref.shape[0]
    sw = 512
    row = lax.broadcasted_iota(I32, (nc, 1), 0)
    for cb in range(sre_ref.shape[1] // sw):
        sl = slice(cb * sw, (cb + 1) * sw)
        xr, xi = sre_ref[:, sl], sim_ref[:, sl]
        for s in range(nsteps):
            sh = 1 << s
            ok = row >= sh
            rr = jnp.where(ok, pltpu.roll(xr, sh, 0), 0.0)
            ri = jnp.where(ok, pltpu.roll(xi, sh, 0), 0.0)
            pr, pi = pr_ref[s:s + 1, sl], pi_ref[s:s + 1, sl]
            xr, xi = xr + pr * rr - pi * ri, xi + pr * ri + pi * rr
        ok = row >= 1
        xre_ref[:, sl] = jnp.where(ok, pltpu.roll(xr, 1, 0), 0.0)
        xim_ref[:, sl] = jnp.where(ok, pltpu.roll(xi, 1, 0), 0.0)


def _s5_out_body(u_ref, xre_ref, xim_ref, bre_ref, bim_ref, cre_ref, cim_ref, ar_ref, ai_ref,
                 d_ref, wglu_ref, bglu_ref, out_ref, y_scr):
    sw = bre_ref.shape[2]
    for blk in range(B_WIDTH // LANES):
        arb = ar_ref[:, blk * sw:(blk + 1) * sw]
        aib = ai_ref[:, blk * sw:(blk + 1) * sw]
        xr = xre_ref[:, blk * sw:(blk + 1) * sw]
        xi = xim_ref[:, blk * sw:(blk + 1) * sw]
        db = d_ref[:, blk * LANES:(blk + 1) * LANES]
        for t in range(S5_CHUNK):
            cs = slice(t * B_WIDTH + blk * LANES, t * B_WIDTH + (blk + 1) * LANES)
            uf = u_ref[:, cs]
            ub = uf.astype(BF16)
            xr, xi = (arb * xr - aib * xi + _dot(ub, bre_ref[blk]),
                      arb * xi + aib * xr + _dot(ub, bim_ref[blk]))
            y = _dot(xr.astype(BF16), cre_ref[blk]) + _dot(xi.astype(BF16), cim_ref[blk]) + db * uf
            y_scr[:, cs] = jax.nn.gelu(y, approximate=True)
    for t in range(S5_CHUNK):
        cs = slice(t * B_WIDTH, (t + 1) * B_WIDTH)
        y = y_scr[:, cs]
        z = _dot(y.astype(BF16), wglu_ref[...]) + bglu_ref[...]
        out_ref[:, cs] = (y * jax.nn.sigmoid(z)).astype(out_ref.dtype)


def _s5_mixer(s_in, prm, w_glu_bf, b_glu, bn, seq, tr=128):
    t = bn * seq
    nrow = t // S5_CHUNK
    ncb = seq // S5_CHUNK
    width = S5_CHUNK * B_WIDTH
    sdim = B_GROUPS * B_STATE
    uc = s_in.reshape(nrow, width)
    nblk = B_WIDTH // LANES
    const3 = lambda i: (0, 0, 0)
    const2 = lambda i: (0, 0)
    b_spec = pl.BlockSpec((nblk, LANES, sdim // nblk), const3)
    c_spec = pl.BlockSpec((nblk, sdim // nblk, LANES), const3)
    a_spec = pl.BlockSpec((1, sdim), const2)
    s_re, s_im = pl.pallas_call(
        _s5_local_body,
        grid=(nrow // tr,),
        in_specs=[pl.BlockSpec((tr, width), lambda i: (i, 0)), b_spec, b_spec, a_spec, a_spec],
        out_specs=[pl.BlockSpec((tr, sdim), lambda i: (i, 0))] * 2,
        out_shape=[jax.ShapeDtypeStruct((nrow, sdim), F32)] * 2,
        compiler_params=_cparams(("parallel",)),
        name="s5_local",
    )(uc, prm['b_re'], prm['b_im'], prm['ar'], prm['ai'])
    nsteps = prm['pows_r'].shape[0]
    x_re, x_im = pl.pallas_call(
        functools.partial(_s5_scan_body, nsteps),
        grid=(bn,),
        in_specs=[pl.BlockSpec((ncb, sdim), lambda b: (b, 0))] * 2
                 + [pl.BlockSpec((nsteps, sdim), lambda b: (0, 0))] * 2,
        out_specs=[pl.BlockSpec((ncb, sdim), lambda b: (b, 0))] * 2,
        out_shape=[jax.ShapeDtypeStruct((nrow, sdim), F32)] * 2,
        compiler_params=_cparams(("parallel",)),
        name="s5_scan",
    )(s_re, s_im, prm['pows_r'], prm['pows_i'])
    y = pl.pallas_call(
        _s5_out_body,
        grid=(nrow // tr,),
        in_specs=[pl.BlockSpec((tr, width), lambda i: (i, 0)),
                  pl.BlockSpec((tr, sdim), lambda i: (i, 0)),
                  pl.BlockSpec((tr, sdim), lambda i: (i, 0)),
                  b_spec, b_spec, c_spec, c_spec, a_spec, a_spec,
                  pl.BlockSpec((1, B_WIDTH), const2),
                  pl.BlockSpec((B_WIDTH, B_WIDTH), const2),
                  pl.BlockSpec((1, B_WIDTH), const2)],
        out_specs=pl.BlockSpec((tr, width), lambda i: (i, 0)),
        out_shape=jax.ShapeDtypeStruct((nrow, width), BF16),
        scratch_shapes=[pltpu.VMEM((tr, width), F32)],
        compiler_params=_cparams(("parallel",)),
        name="s5_out",
    )(uc, x_re, x_im, prm['b_re'], prm['b_im'], prm['c_re'], prm['c_im'], prm['ar'], prm['ai'],
      prm['d'], w_glu_bf, b_glu.reshape(1, B_WIDTH).astype(F32))
    return y


def _outproj_body(nparts, nflat, *refs):
    parts = refs[:nparts]
    flats = refs[nparts:nparts + nflat]
    w_ref, h_ref, gt_ref, o_ref = refs[nparts + nflat:nparts + nflat + 4]
    scr = refs[nparts + nflat + 4:]
    tm = h_ref.shape[0]
    acc = None
    off = 0
    for p in parts:
        k = p.shape[1]
        d = _dot(p[...].astype(BF16), w_ref[off:off + k, :])
        acc = d if acc is None else acc + d
        off += k
    for p, s in zip(flats, scr):
        nc = s.shape[0]
        k = nc * LANES
        for r in range(S5_CHUNK):
            for c in range(nc):
                s[c, pl.ds(r, tm // S5_CHUNK, stride=S5_CHUNK), :] = (
                    p[:, r * k + c * LANES:r * k + (c + 1) * LANES].astype(F32))
        rows = jnp.concatenate([s[c] for c in range(nc)], axis=1)
        d = _dot(rows.astype(BF16), w_ref[off:off + k, :])
        acc = d if acc is None else acc + d
        off += k
    o_ref[...] = h_ref[...] + gt_ref[0] * acc


def _outproj(parts, flat_parts, w_bf, h2, gate, seq, tm=512):
    t, d = h2.shape
    tpb = seq // tm
    in_specs = [pl.BlockSpec((tm, p.shape[1]), lambda i: (i, 0)) for p in parts]
    in_specs += [pl.BlockSpec((tm // S5_CHUNK, p.shape[1]), lambda i: (i, 0)) for p in flat_parts]
    in_specs += [pl.BlockSpec(w_bf.shape, lambda i: (0, 0)),
                 pl.BlockSpec((tm, d), lambda i: (i, 0)),
                 pl.BlockSpec((1, 1, d), lambda i: (i // tpb, 0, 0))]
    return pl.pallas_call(
        functools.partial(_outproj_body, len(parts), len(flat_parts)),
        grid=(t // tm,),
        in_specs=in_specs,
        out_specs=pl.BlockSpec((tm, d), lambda i: (i, 0)),
        out_shape=jax.ShapeDtypeStruct((t, d), F32),
        scratch_shapes=[pltpu.VMEM((p.shape[1] // S5_CHUNK // LANES, tm, LANES), F32) for p in flat_parts],
        compiler_params=_cparams(("parallel",)),
        name="outproj",
    )(*parts, *flat_parts, w_bf, h2, gate)


def _ffn_body(h_ref, g_ref, sh_ref, sc_ref, gt_ref, wg_ref, wu_ref, wd_ref, o_ref, u_scr, acc_scr):
    f = pl.program_id(1)

    @pl.when(f == 0)
    def _():
        u_scr[...] = _norm_mod(h_ref[...], g_ref[...], sh_ref[0], sc_ref[0]).astype(BF16)
        acc_scr[...] = jnp.zeros_like(acc_scr)

    u = u_scr[...]
    gp = _dot(u, wg_ref[...])
    up = _dot(u, wu_ref[...])
    a = (gp * jax.nn.sigmoid(gp) * up).astype(BF16)
    acc_scr[...] += _dot(a, wd_ref[...])

    @pl.when(f == pl.num_programs(1) - 1)
    def _():
        o_ref[...] = h_ref[...] + gt_ref[0] * acc_scr[...]


def _ffn_dense(h2, g, shift, scale, gate, wg, wu, wd, seq, tm=1024, tf=FFN_TF):
    t, d = h2.shape
    ff = wg.shape[1]
    tpb = seq // tm
    mod_spec = pl.BlockSpec((1, 1, d), lambda i, f: (i // tpb, 0, 0))
    return pl.pallas_call(
        _ffn_body,
        grid=(t // tm, ff // tf),
        in_specs=[pl.BlockSpec((tm, d), lambda i, f: (i, 0)),
                  pl.BlockSpec((1, d), lambda i, f: (0, 0)),
                  mod_spec, mod_spec, mod_spec,
                  pl.BlockSpec((d, tf), lambda i, f: (0, f)),
                  pl.BlockSpec((d, tf), lambda i, f: (0, f)),
                  pl.BlockSpec((tf, d), lambda i, f: (f, 0))],
        out_specs=pl.BlockSpec((tm, d), lambda i, f: (i, 0)),
        out_shape=jax.ShapeDtypeStruct((t, d), F32),
        scratch_shapes=[pltpu.VMEM((tm, d), BF16), pltpu.VMEM((tm, d), F32)],
        compiler_params=_cparams(("parallel", "arbitrary")),
        name="ffn_dense",
    )(h2, g, shift, scale, gate, wg, wu, wd)


P1_QR = 0
P1_QN = 256
P1_CKV = 1280
P1_KR = 1536
P1_QI = 1792
P1_KI = 2304
P1_WI = 2432
P1_COLS = 2560


def _proj1_layout(w_in):
    d = w_in.shape[0]
    c0 = 0
    w_qr = w_in[:, c0:c0 + C_HEADS * ROPE_DIM]; c0 += C_HEADS * ROPE_DIM
    w_qn = w_in[:, c0:c0 + C_HEADS * C_NOPE]; c0 += C_HEADS * C_NOPE
    w_ckv = w_in[:, c0:c0 + C_LATENT]; c0 += C_LATENT
    w_kr = w_in[:, c0:c0 + ROPE_DIM]; c0 += ROPE_DIM
    w_qi = w_in[:, c0:c0 + IDX_HEADS * IDX_DIM]; c0 += IDX_HEADS * IDX_DIM
    w_ki = w_in[:, c0:c0 + IDX_DIM]; c0 += IDX_DIM
    w_wi = w_in[:, c0:c0 + IDX_HEADS]
    w_qn = jnp.pad(w_qn.reshape(d, C_HEADS, C_NOPE), ((0, 0), (0, 0), (0, HEAD_DIM - C_NOPE)))
    w_qn = w_qn.reshape(d, C_HEADS * HEAD_DIM)
    w_kr = jnp.tile(w_kr, (1, C_HEADS))
    w_ki = jnp.tile(w_ki, (1, 2))
    w_wi = jnp.pad(w_wi, ((0, 0), (0, LANES - IDX_HEADS)))
    return jnp.concatenate([w_qr, w_qn, w_ckv, w_kr, w_qi, w_ki, w_wi], axis=1)


def _proj1_body(x_ref, g_ref, sh_ref, sc_ref, w_ref, gkv_ref,
                ct16_ref, sa16_ref, sb16_ref, ct64_ref, sa64_ref, sb64_ref,
                qa_ref, kp_ref, qi_ref, ki_ref, wi_ref):
    u = _norm_mod(x_ref[...], g_ref[...], sh_ref[0], sc_ref[0]).astype(BF16)
    t16 = (ct16_ref[...], sa16_ref[...], sb16_ref[...])
    t64 = (ct64_ref[...], sa64_ref[...], sb64_ref[...])
    qscale = HEAD_DIM ** -0.5 * math.log2(math.e)

    def cols(lo, hi):
        return _dot(u, w_ref[:, lo:hi])

    a = cols(P1_QR, P1_QN)
    for c in range(2):
        blk = _rope128(a[:, c * LANES:(c + 1) * LANES], *t16) * qscale
        qa_ref[:, c * LANES:(c + 1) * LANES] = blk.astype(BF16)
    qa_ref[:, P1_QN:P1_CKV] = (cols(P1_QN, P1_CKV) * qscale).astype(BF16)
    ckv = cols(P1_CKV, P1_KR)
    ms = jnp.mean(ckv * ckv, axis=-1, keepdims=True)
    kp_ref[:, 0:C_LATENT] = (ckv * lax.rsqrt(ms + NORM_EPS) * gkv_ref[...]).astype(BF16)
    a = cols(P1_KR, P1_QI)
    for c in range(2):
        blk = _rope128(a[:, c * LANES:(c + 1) * LANES], *t16)
        kp_ref[:, C_LATENT + c * LANES:C_LATENT + (c + 1) * LANES] = blk.astype(BF16)
    a = cols(P1_QI, P1_KI)
    for c in range(4):
        qi_ref[:, c * LANES:(c + 1) * LANES] = _rope128(a[:, c * LANES:(c + 1) * LANES], *t64).astype(BF16)
    ki_ref[...] = _rope128(cols(P1_KI, P1_WI), *t64).astype(BF16)
    wi_ref[...] = cols(P1_WI, P1_COLS) * (IDX_HEADS ** -0.5) * (IDX_DIM ** -0.5)


def _proj1(x2, g, shift, scale, w_bf, gkv, t16, t64, seq, tm=512):
    t, d = x2.shape
    tpb = seq // tm
    tab_spec = pl.BlockSpec((tm, LANES), lambda i: (i % tpb, 0))
    mod_spec = pl.BlockSpec((1, 1, d), lambda i: (i // tpb, 0, 0))
    widths = (P1_CKV, 2 * C_LATENT, IDX_HEADS * IDX_DIM, LANES, LANES)
    dtypes = (BF16, BF16, BF16, BF16, F32)
    return pl.pallas_call(
        _proj1_body,
        grid=(t // tm,),
        in_specs=[pl.BlockSpec((tm, d), lambda i: (i, 0)),
                  pl.BlockSpec((1, d), lambda i: (0, 0)),
                  mod_spec, mod_spec,
                  pl.BlockSpec((d, P1_COLS), lambda i: (0, 0)),
                  pl.BlockSpec((1, C_LATENT), lambda i: (0, 0))] + [tab_spec] * 6,
        out_specs=[pl.BlockSpec((tm, w), lambda i: (i, 0)) for w in widths],
        out_shape=[jax.ShapeDtypeStruct((t, w), dt) for w, dt in zip(widths, dtypes)],
        compiler_params=_cparams(("parallel",)),
        name="proj1",
    )(x2, g, shift, scale, w_bf, gkv, *t16, *t64)


IDX_KT = 512


def _idx_body(topk, qi_ref, ki_ref, wi_ref, mask_ref, sc_scr):
    i = pl.program_id(1)
    seq = ki_ref.shape[1]
    nkt = seq // IDX_KT
    nlive = (i * Q_BLOCK) // IDX_KT + 1
    q = qi_ref[0]
    wt = wi_ref[0].T
    lane = lax.broadcasted_iota(I32, (1, LANES), 1)
    qpos = i * Q_BLOCK + lane
    krow = lax.broadcasted_iota(I32, (IDX_KT, 1), 0)
    qpair = [jnp.concatenate([q[:, (2 * g) * LANES:(2 * g + 1) * LANES],
                              q[:, (2 * g + 1) * LANES:(2 * g + 2) * LANES]], axis=0) for g in range(2)]

    def score_tile(kt, c):
        kk = ki_ref[0, pl.ds(pl.multiple_of(kt * IDX_KT, IDX_KT), IDX_KT), :]
        zero = jnp.zeros_like(kk)
        kpart = [jnp.where(lane < IDX_DIM, kk, zero), jnp.where(lane >= IDX_DIM, kk, zero)]
        acc = jnp.zeros((IDX_KT, Q_BLOCK), F32)
        for g in range(2):
            for part in range(2):
                res = jnp.maximum(_dot_nt(kpart[part], qpair[g]), 0.0)
                for j in range(2):
                    h = 2 * (2 * g + j) + part
                    acc = acc + wt[h:h + 1, :] * res[:, j * Q_BLOCK:(j + 1) * Q_BLOCK]
        acc = acc + 0.0
        sc_scr[kt] = jnp.where(kt * IDX_KT + krow <= qpos, acc, -jnp.inf)
        return c

    lax.fori_loop(0, nlive, score_tile, 0)
    kq = jnp.minimum(qpos + 1, topk).astype(F32)

    def count(pred_fn):
        def body(kt, acc):
            ones = jnp.where(pred_fn(sc_scr[kt], kt), 1.0, 0.0)
            part = jnp.sum(ones.reshape(8, IDX_KT // 64, 8, Q_BLOCK), axis=1)
            return acc + jnp.sum(part, axis=0)
        part = lax.fori_loop(0, nlive, body, jnp.zeros((8, Q_BLOCK), F32))
        return jnp.sum(part, axis=0, keepdims=True)

    def key_to_f32(key):
        bits = jnp.where(key < 0, key ^ jnp.int32(0x7FFFFFFF), key)
        return lax.bitcast_convert_type(bits, F32)

    def bit_step(it, ans):
        cand = ans | lax.shift_left(jnp.int32(1), 31 - it)
        thr = key_to_f32(cand ^ jnp.int32(INT_MIN))
        cnt = count(lambda sc, kt: sc >= thr)
        return jnp.where(cnt >= kq, cand, ans)

    ans = lax.fori_loop(0, 32, bit_step, jnp.zeros((1, Q_BLOCK), I32))
    thr = key_to_f32(ans ^ jnp.int32(INT_MIN))
    n_ge = count(lambda sc, kt: sc >= thr)
    nbits = int(math.log2(seq))

    def tie_cut(_):
        need = kq - count(lambda sc, kt: sc > thr)

        def tie_step(it, ans2):
            cand = ans2 | lax.shift_left(jnp.int32(1), nbits - 1 - it)
            cnt = count(lambda sc, kt: (sc == thr) & (kt * IDX_KT + krow < cand))
            return jnp.where(cnt < need, cand, ans2)

        return lax.fori_loop(0, nbits, tie_step, jnp.zeros((1, Q_BLOCK), I32))

    excess = jnp.max(n_ge - kq) > 0.0
    jcut = lax.cond(excess, tie_cut, lambda _: jnp.full((1, Q_BLOCK), seq, I32), 0)

    def emit(kt, c):
        sc = sc_scr[kt]
        sel = (sc > thr) | ((sc == thr) & (kt * IDX_KT + krow <= jcut))
        mask_ref[0, 0, kt] = jnp.where(sel, 1.0, 0.0).T.astype(BF16)
        return c

    lax.fori_loop(0, nlive, emit, 0)

    def emit_dead(kt, c):
        mask_ref[0, 0, kt] = jnp.zeros((Q_BLOCK, IDX_KT), BF16)
        return c

    lax.fori_loop(nlive, nkt, emit_dead, 0)


def _dsa_select(qi, ki, wi, bn, seq):
    topk = min(TOPK_MAX, seq // 4)
    nqb = seq // Q_BLOCK
    nkt = seq // IDX_KT
    return pl.pallas_call(
        functools.partial(_idx_body, topk),
        grid=(bn, nqb),
        in_specs=[pl.BlockSpec((1, Q_BLOCK, IDX_HEADS * IDX_DIM), lambda b, i: (b, i, 0)),
                  pl.BlockSpec((1, seq, LANES), lambda b, i: (b, 0, 0)),
                  pl.BlockSpec((1, Q_BLOCK, LANES), lambda b, i: (b, i, 0))],
        out_specs=pl.BlockSpec((1, 1, nkt, Q_BLOCK, IDX_KT), lambda b, i: (b, i, 0, 0, 0)),
        out_shape=jax.ShapeDtypeStruct((bn, nqb, nkt, Q_BLOCK, IDX_KT), BF16),
        scratch_shapes=[pltpu.VMEM((nkt, IDX_KT, Q_BLOCK), F32)],
        compiler_params=_cparams(("parallel", "arbitrary")),
        name="dsa_select",
    )(qi.reshape(bn, seq, -1), ki.reshape(bn, seq, -1), wi.reshape(bn, seq, -1))


ATT_RB = 512


def _dsa_attn_body(qa_ref, qan_ref, kp_ref, mask_ref, wuk_ref, wuv_ref, o_ref, qp_scr, m_scr, l_scr,
                   acc_scr, s_scr):
    i = pl.program_id(1)
    rows = C_HEADS * Q_BLOCK
    nblk = rows // ATT_RB
    cur = lax.rem(i, 2)
    nxt = 1 - cur
    lane = lax.broadcasted_iota(I32, (1, C_HEADS * ROPE_DIM), 1)

    def prep(src_ref, slot):
        qr = src_ref[0, :, 0:P1_QN]
        for h in range(C_HEADS):
            p = h // 2
            qn = src_ref[0, :, P1_QN + p * LANES:P1_QN + (p + 1) * LANES]
            qlat = _dot(qn, wuk_ref[h])
            qp_scr[slot, h * Q_BLOCK:(h + 1) * Q_BLOCK, 0:C_LATENT] = qlat.astype(BF16)
            qp_scr[slot, h * Q_BLOCK:(h + 1) * Q_BLOCK, C_LATENT:2 * C_LATENT] = jnp.where(
                (lane // ROPE_DIM) == h, qr, jnp.zeros_like(qr))

    def ktile(kt):
        return kp_ref[0, pl.ds(pl.multiple_of(kt * IDX_KT, IDX_KT), IDX_KT), :]

    def tile_bias(kt):
        bias = (mask_ref[0, 0, kt].astype(F32) - 1.0) * (-NEG_BIG)
        return jnp.concatenate([bias] * (ATT_RB // Q_BLOCK), axis=0)

    def softmax_pv(b, s, bias, vv):
        rs = slice(b * ATT_RB, (b + 1) * ATT_RB)
        s = s + bias
        m_prev = m_scr[rs, :]
        m_next = jnp.maximum(m_prev, jnp.max(s, axis=1, keepdims=True))
        alpha = jnp.exp2(m_prev - m_next)
        pr = jnp.exp2(s - jnp.concatenate([m_next] * (IDX_KT // LANES), axis=1))
        l_scr[rs, :] = alpha * l_scr[rs, :] + jnp.sum(pr, axis=1, keepdims=True)
        m_scr[rs, :] = m_next
        acc_scr[rs, :] = (jnp.concatenate([alpha] * (C_LATENT // LANES), axis=1) * acc_scr[rs, :]
                          + _dot(pr.astype(BF16), vv))

    def tile_step(kt, slot_next, kk_next):
        vv = ktile(kt)[:, 0:C_LATENT]
        bias = tile_bias(kt)
        for b in range(nblk):
            rs = slice(b * ATT_RB, (b + 1) * ATT_RB)
            s = s_scr[rs, :]
            s_scr[rs, :] = _dot_nt(qp_scr[slot_next, rs, :], kk_next)
            softmax_pv(b, s, bias, vv)

    @pl.when(i == 0)
    def _():
        prep(qa_ref, 0)
        kk0 = ktile(0)
        for b in range(nblk):
            rs = slice(b * ATT_RB, (b + 1) * ATT_RB)
            s_scr[rs, :] = _dot_nt(qp_scr[0, rs, :], kk0)

    m_scr[...] = jnp.full(m_scr.shape, NEG_BIG, F32)
    l_scr[...] = jnp.zeros_like(l_scr)
    acc_scr[...] = jnp.zeros_like(acc_scr)
    last = (i * Q_BLOCK) // IDX_KT

    def step(kt, carry):
        tile_step(kt, cur, ktile(kt + 1))
        return carry

    lax.fori_loop(0, last, step, 0)
    prep(qan_ref, nxt)
    tile_step(last, nxt, ktile(0))
    olat = (acc_scr[...] / jnp.concatenate([l_scr[...]] * (C_LATENT // LANES), axis=1)).astype(BF16)
    for p in range(C_HEADS // 2):
        o = (_dot(olat[(2 * p) * Q_BLOCK:(2 * p + 1) * Q_BLOCK], wuv_ref[2 * p])
             + _dot(olat[(2 * p + 1) * Q_BLOCK:(2 * p + 2) * Q_BLOCK], wuv_ref[2 * p + 1]))
        o_ref[0, :, p * LANES:(p + 1) * LANES] = o.astype(o_ref.dtype)


def _dsa_attention(qa, kp, mask, wuk_x, wuv_x, bn, seq):
    nqb = seq // Q_BLOCK
    nkt = seq // IDX_KT
    rows = C_HEADS * Q_BLOCK
    return pl.pallas_call(
        _dsa_attn_body,
        grid=(bn, nqb),
        in_specs=[pl.BlockSpec((1, Q_BLOCK, P1_CKV), lambda b, i: (b, i, 0)),
                  pl.BlockSpec((1, Q_BLOCK, P1_CKV), lambda b, i: (b, jnp.minimum(i + 1, nqb - 1), 0)),
                  pl.BlockSpec((1, seq, 2 * C_LATENT), lambda b, i: (b, 0, 0)),
                  pl.BlockSpec((1, 1, nkt, Q_BLOCK, IDX_KT), lambda b, i: (b, i, 0, 0, 0)),
                  pl.BlockSpec((C_HEADS, LANES, C_LATENT), lambda b, i: (0, 0, 0)),
                  pl.BlockSpec((C_HEADS, C_LATENT, LANES), lambda b, i: (0, 0, 0))],
        out_specs=pl.BlockSpec((1, Q_BLOCK, C_HEADS * HEAD_DIM), lambda b, i: (b, i, 0)),
        out_shape=jax.ShapeDtypeStruct((bn, seq, C_HEADS * HEAD_DIM), BF16),
        scratch_shapes=[pltpu.VMEM((2, rows, 2 * C_LATENT), BF16),
                        pltpu.VMEM((rows, LANES), F32), pltpu.VMEM((rows, LANES), F32),
                        pltpu.VMEM((rows, C_LATENT), F32),
                        pltpu.VMEM((rows, IDX_KT), F32)],
        compiler_params=_cparams(("arbitrary", "arbitrary")),
        name="dsa_attn",
    )(qa.reshape(bn, seq, -1), qa.reshape(bn, seq, -1), kp.reshape(bn, seq, -1), mask, wuk_x, wuv_x)


def _mla_weights(w_uk, w_uv):
    wuk = jnp.transpose(w_uk, (1, 2, 0))
    wuk_x = jnp.zeros((C_HEADS, LANES, C_LATENT), F32)
    wuv = jnp.transpose(w_uv, (1, 0, 2))
    wuv_x = jnp.zeros((C_HEADS, C_LATENT, LANES), F32)
    for h in range(C_HEADS):
        o = (h % 2) * HEAD_DIM
        wuk_x = wuk_x.at[h, o:o + C_NOPE, :].set(wuk[h])
        wuv_x = wuv_x.at[h, :, o:o + HEAD_DIM].set(wuv[h])
    return wuk_x.astype(BF16), wuv_x.astype(BF16)


def _router_body(h_ref, g_ref, sh_ref, sc_ref, wr_ref, br_ref, route_ref, up_ref, cnt_ref, carry_scr):
    i = pl.program_id(0)
    tm = h_ref.shape[0]

    @pl.when(i == 0)
    def _():
        carry_scr[...] = jnp.zeros_like(carry_scr)

    u = _norm_mod(h_ref[...], g_ref[...], sh_ref[0], sc_ref[0])
    logits = jnp.dot(u, wr_ref[...], preferred_element_type=F32, precision=HIGHEST) + br_ref[...]
    lane = lax.broadcasted_iota(I32, (tm, LANES), 1).astype(F32)
    m1 = jnp.max(logits, axis=1, keepdims=True)
    e1 = jnp.min(jnp.where(logits == m1, lane, float(LANES)), axis=1, keepdims=True)
    rest = jnp.where(lane == e1, NEG_BIG * 2, logits)
    m2 = jnp.max(rest, axis=1, keepdims=True)
    e2 = jnp.min(jnp.where(rest == m2, lane, float(LANES)), axis=1, keepdims=True)
    ex = jnp.exp(m2 - m1)
    g1 = 1.0 / (1.0 + ex)
    g2 = ex / (1.0 + ex)
    onehot = ((lane == e1) | (lane == e2))
    oh_bf = jnp.where(onehot, 1.0, 0.0).astype(BF16)
    r = lax.broadcasted_iota(I32, (tm, tm), 0)
    c = lax.broadcasted_iota(I32, (tm, tm), 1)
    tri = jnp.where(c < r, 1.0, 0.0).astype(BF16)
    prefix = _dot(tri, oh_bf) + carry_scr[...]
    rank1 = jnp.sum(jnp.where(lane == e1, prefix, 0.0), axis=1, keepdims=True)
    rank2 = jnp.sum(jnp.where(lane == e2, prefix, 0.0), axis=1, keepdims=True)
    carry_scr[...] = carry_scr[...] + jnp.sum(jnp.where(onehot, 1.0, 0.0), axis=0, keepdims=True)
    cnt_ref[...] = carry_scr[...]
    vals = [e1, e2, g1, g2, rank1, rank2]
    route = jnp.zeros((tm, LANES), F32)
    for k, v in enumerate(vals):
        route = jnp.where(lane == float(k), v, route)
    route_ref[...] = route
    up_ref[...] = u


def _router(h2, g, shift, scale, w_router, b_router, seq, tm=512):
    t, d = h2.shape
    tpb = seq // tm
    wr = jnp.pad(w_router.astype(F32), ((0, 0), (0, LANES - N_EXPERTS)))
    br = jnp.pad(b_router.astype(F32), (0, LANES - N_EXPERTS), constant_values=NEG_BIG).reshape(1, LANES)
    mod_spec = pl.BlockSpec((1, 1, d), lambda i: (i // tpb, 0, 0))
    return pl.pallas_call(
        _router_body,
        grid=(t // tm,),
        in_specs=[pl.BlockSpec((tm, d), lambda i: (i, 0)),
                  pl.BlockSpec((1, d), lambda i: (0, 0)),
                  mod_spec, mod_spec,
                  pl.BlockSpec((d, LANES), lambda i: (0, 0)),
                  pl.BlockSpec((1, LANES), lambda i: (0, 0))],
        out_specs=[pl.BlockSpec((tm, LANES), lambda i: (i, 0)),
                   pl.BlockSpec((tm, d), lambda i: (i, 0)),
                   pl.BlockSpec((1, LANES), lambda i: (0, 0))],
        out_shape=[jax.ShapeDtypeStruct((t, LANES), F32),
                   jax.ShapeDtypeStruct((t, d), F32),
                   jax.ShapeDtypeStruct((1, LANES), F32)],
        scratch_shapes=[pltpu.VMEM((1, LANES), F32)],
        compiler_params=_cparams(("arbitrary",)),
        name="moe_router",
    )(h2, g, shift, scale, wr, br)


def _dispatch_body(p1_ref, p2_ref, up_ref, xs_in_ref, xs_ref, sem):
    del xs_in_ref
    tm = up_ref.shape[0]

    def copy(r, dst):
        return pltpu.make_async_copy(up_ref.at[pl.ds(r, 1)], xs_ref.at[pl.ds(dst, 1)], sem)

    def start(r, c):
        copy(r, p1_ref[r]).start(priority=0)
        copy(r, p2_ref[r]).start(priority=1)
        return c

    lax.fori_loop(0, tm, start, 0, unroll=DMA_UNROLL)

    def wait(r, c):
        copy(r, p1_ref[r]).wait()
        copy(r, p2_ref[r]).wait()
        return c

    lax.fori_loop(0, tm, wait, 0, unroll=DMA_UNROLL)


def _dispatch(pos1, pos2, up, nrows, tm=256):
    t, w = up.shape
    xs0 = jnp.zeros((nrows, w), F32)
    smem_spec = pl.BlockSpec((tm,), lambda i: (i,), memory_space=pltpu.SMEM)
    return pl.pallas_call(
        _dispatch_body,
        grid=(t // tm,),
        in_specs=[smem_spec, smem_spec,
                  pl.BlockSpec((tm, w), lambda i: (i, 0)),
                  pl.BlockSpec(memory_space=pl.ANY)],
        out_specs=pl.BlockSpec(memory_space=pl.ANY),
        out_shape=jax.ShapeDtypeStruct((nrows, w), F32),
        scratch_shapes=[pltpu.SemaphoreType.DMA(())],
        input_output_aliases={3: 0},
        compiler_params=_cparams(("arbitrary",)),
        name="moe_dispatch",
    )(pos1, pos2, up, xs0)


def _expert_body(te_ref, nv_ref, xs_ref, wg_ref, wu_ref, wd_ref, y_ref, x_scr, acc_scr):
    j = pl.program_id(0)
    f = pl.program_id(1)
    live = j < nv_ref[0]

    @pl.when(f == 0)
    def _():
        x_scr[...] = xs_ref[...].astype(BF16)
        acc_scr[...] = jnp.zeros_like(acc_scr)

    @pl.when(live)
    def _():
        x = x_scr[...]
        gp = _dot(x, wg_ref[0])
        up = _dot(x, wu_ref[0])
        a = (gp * jax.nn.sigmoid(gp) * up).astype(BF16)
        acc_scr[...] += _dot(a, wd_ref[0])

    @pl.when(f == pl.num_programs(1) - 1)
    def _():
        y_ref[...] = acc_scr[...]


def _experts(tile_expert, n_live, xs, wg, wu, wd, tm, tf=MOE_TF):
    nrows, d = xs.shape
    nf = wg.shape[2] // tf
    nt = nrows // tm

    def f_eff(j, f, nv):
        return jnp.where(j < nv[0], f, nf - 1)

    grid_spec = pltpu.PrefetchScalarGridSpec(
        num_scalar_prefetch=2,
        grid=(nt, nf),
        in_specs=[pl.BlockSpec((tm, d), lambda j, f, te, nv: (j, 0)),
                  pl.BlockSpec((1, d, tf), lambda j, f, te, nv: (te[j], 0, f_eff(j, f, nv))),
                  pl.BlockSpec((1, d, tf), lambda j, f, te, nv: (te[j], 0, f_eff(j, f, nv))),
                  pl.BlockSpec((1, tf, d), lambda j, f, te, nv: (te[j], f_eff(j, f, nv), 0))],
        out_specs=pl.BlockSpec((tm, d), lambda j, f, te, nv: (j, 0)),
        scratch_shapes=[pltpu.VMEM((tm, d), BF16), pltpu.VMEM((tm, d), F32)])
    return pl.pallas_call(
        _expert_body,
        grid_spec=grid_spec,
        out_shape=jax.ShapeDtypeStruct((nrows, d), F32),
        compiler_params=_cparams(("arbitrary", "arbitrary")),
        name="moe_experts",
    )(tile_expert, n_live, xs, wg, wu, wd)


def _combine_body(p1_ref, p2_ref, route_ref, h_ref, gt_ref, gf_ref, y_ref, o_ref, y1_scr, y2_scr, sem):
    tm = h_ref.shape[0]

    def copy(src, dst_scr, r):
        return pltpu.make_async_copy(y_ref.at[pl.ds(src, 1)], dst_scr.at[pl.ds(r, 1)], sem)

    def start(r, c):
        copy(p1_ref[r], y1_scr, r).start(priority=0)
        copy(p2_ref[r], y2_scr, r).start(priority=1)
        return c

    lax.fori_loop(0, tm, start, 0, unroll=DMA_UNROLL)

    def wait(r, c):
        copy(p1_ref[r], y1_scr, r).wait()
        copy(p2_ref[r], y2_scr, r).wait()
        return c

    lax.fori_loop(0, tm, wait, 0, unroll=DMA_UNROLL)
    route = route_ref[...]
    g1, g2 = route[:, 2:3], route[:, 3:4]
    y = g1 * y1_scr[...] + g2 * y2_scr[...]
    hn = h_ref[...] + gt_ref[0] * y
    ms = jnp.mean(hn * hn, axis=-1, keepdims=True)
    o_ref[...] = hn * lax.rsqrt(ms + NORM_EPS) * gf_ref[...]


def _combine(pos1, pos2, route, h2, gate, g_final, y_sorted, seq, tm=256):
    t, d = h2.shape
    tpb = seq // tm
    smem_spec = pl.BlockSpec((tm,), lambda i: (i,), memory_space=pltpu.SMEM)
    return pl.pallas_call(
        _combine_body,
        grid=(t // tm,),
        in_specs=[smem_spec, smem_spec,
                  pl.BlockSpec((tm, LANES), lambda i: (i, 0)),
                  pl.BlockSpec((tm, d), lambda i: (i, 0)),
                  pl.BlockSpec((1, 1, d), lambda i: (i // tpb, 0, 0)),
                  pl.BlockSpec((1, d), lambda i: (0, 0)),
                  pl.BlockSpec(memory_space=pl.ANY)],
        out_specs=pl.BlockSpec((tm, d), lambda i: (i, 0)),
        out_shape=jax.ShapeDtypeStruct((t, d), F32),
        scratch_shapes=[pltpu.VMEM((tm, d), F32), pltpu.VMEM((tm, d), F32),
                        pltpu.SemaphoreType.DMA(())],
        compiler_params=_cparams(("arbitrary",)),
        name="moe_combine",
    )(pos1, pos2, route, h2, gate, g_final, y_sorted)


MOE_TILE = 512


def _moe_layout(route, counts):
    e1 = route[:, 0].astype(I32)
    e2 = route[:, 1].astype(I32)
    r1 = route[:, 4].astype(I32)
    r2 = route[:, 5].astype(I32)
    cnt = counts[0, :N_EXPERTS].astype(I32)
    tiles = (cnt + MOE_TILE - 1) // MOE_TILE
    tile_end = jnp.cumsum(tiles)
    start = (tile_end - tiles) * MOE_TILE
    pos1 = start[e1] + r1
    pos2 = start[e2] + r2
    nt = route.shape[0] * 2 // MOE_TILE + N_EXPERTS
    n_live = tile_end[-1]
    tid = jnp.minimum(jnp.arange(nt, dtype=I32), n_live - 1)
    tile_expert = jnp.sum((tid[:, None] >= tile_end[None, :]).astype(I32), axis=1)
    return pos1, pos2, tile_expert.astype(I32), n_live.reshape(1).astype(I32), nt * MOE_TILE


def kernel(x, c, w_ada, b_ada, g_mix, g_ffn, g_final, e_w_in, e_w_out, s5_a_re, s5_a_im, s5_log_dt,
           s5_b_re, s5_b_im, s5_c_re, s5_c_im, s5_d, s5_w_glu, s5_b_glu, ff_w_gate, ff_w_up,
           ff_w_down, o_w_in, o_w_out, mla_g_kv, mla_w_uk, mla_w_uv, moe_w_router, moe_b_router,
           moe_w_gate, moe_w_up, moe_w_down):
    bn, seq, d = x.shape
    t = bn * seq
    mod = _adaln(c, w_ada, b_ada)

    def mod_vec(layer, k):
        return mod[layer, :, k * d:(k + 1) * d].reshape(bn, 1, d)

    tabs64 = _rope_tables(seq, HEAD_DIM)
    tabs16 = _rope_tables(seq, ROPE_DIM)
    h = x.reshape(t, d)

    *qkv_views, s_in = _proj0(h, g_mix[0].reshape(1, d), mod_vec(0, 0), mod_vec(0, 1),
                              e_w_in[0].astype(BF16), tabs64, seq)
    y_a = _dilated_attention(qkv_views, bn, seq)
    prm = _s5_prepare(s5_a_re[0], s5_a_im[0], s5_log_dt[0], s5_b_re[0], s5_b_im[0],
                      s5_c_re[0], s5_c_im[0], s5_d[0], seq // S5_CHUNK)
    y_b = _s5_mixer(s_in, prm, s5_w_glu[0].astype(BF16), s5_b_glu[0], bn, seq)
    h = _outproj([y_a], [y_b], e_w_out[0].astype(BF16), h, mod_vec(0, 2), seq)
    h = _ffn_dense(h, g_ffn[0].reshape(1, d), mod_vec(0, 3), mod_vec(0, 4), mod_vec(0, 5),
                   ff_w_gate[0].astype(BF16), ff_w_up[0].astype(BF16), ff_w_down[0].astype(BF16), seq)

    w1 = _proj1_layout(o_w_in[0]).astype(BF16)
    qa, kp, qi, ki, wi = _proj1(h, g_mix[1].reshape(1, d), mod_vec(1, 0), mod_vec(1, 1), w1,
                                mla_g_kv[0].reshape(1, C_LATENT).astype(F32), tabs16, tabs64, seq)
    mask = _dsa_select(qi, ki, wi, bn, seq)
    wuk_x, wuv_x = _mla_weights(mla_w_uk[0], mla_w_uv[0])
    o_attn = _dsa_attention(qa, kp, mask, wuk_x, wuv_x, bn, seq)
    h = _outproj([o_attn.reshape(t, C_HEADS * HEAD_DIM)], [], o_w_out[0].astype(BF16), h, mod_vec(1, 2), seq)

    route, up, counts = _router(h, g_ffn[1].reshape(1, d), mod_vec(1, 3), mod_vec(1, 4),
                                moe_w_router[0], moe_b_router[0], seq)
    pos1, pos2, tile_expert, n_live, nrows = _moe_layout(route, counts)
    xs = _dispatch(pos1, pos2, up, nrows)
    y_sorted = _experts(tile_expert, n_live, xs, moe_w_gate[0].astype(BF16), moe_w_up[0].astype(BF16),
                        moe_w_down[0].astype(BF16), MOE_TILE)
    out = _combine(pos1, pos2, route, h, mod_vec(1, 5), g_final.reshape(1, d), y_sorted, seq)
    return out.reshape(bn, seq, d)
```

```python
import functools
import math

import jax
import jax.numpy as jnp
import numpy as np
from jax import lax
from jax.experimental import pallas as pl
from jax.experimental.pallas import tpu as pltpu

F32 = jnp.float32
BF16 = jnp.bfloat16
I32 = jnp.int32
HIGHEST = lax.Precision.HIGHEST

D_MODEL = 1024
HEAD_DIM = 64
ROPE_DIM = 16
ROPE_THETA = 500000.0
NORM_EPS = 1e-6
A_HEADS = 8
A_WIDTH = 512
A_PATTERNS = ((128, 1), (512, 4), (2048, 16))
B_WIDTH = 512
B_GROUP = 16
B_GROUPS = 32
B_STATE = 64
C_HEADS = 16
C_NOPE = 48
C_LATENT = 256
IDX_HEADS = 8
IDX_DIM = 64
TOPK_MAX = 256
FF_DENSE = 2816
N_EXPERTS = 8
FF_EXPERT = 3584

LANES = 128
Q_BLOCK = 128
S5_CHUNK = 16
MOE_TF = 512
FFN_TF = 256
DMA_UNROLL = 8
VMEM_LIMIT = 56 * 1024 * 1024
NEG_BIG = -1e30
INT_MIN = -(2 ** 31)


def _cparams(sem, vmem=VMEM_LIMIT):
    return pltpu.CompilerParams(dimension_semantics=sem, vmem_limit_bytes=vmem)


def _dot(a, b):
    return jnp.dot(a, b, preferred_element_type=F32)


def _dot_nt(a, b):
    return lax.dot_general(a, b, (((1,), (1,)), ((), ())), preferred_element_type=F32)


def _norm_mod(x, g, shift, scale):
    ms = jnp.mean(x * x, axis=-1, keepdims=True)
    y = x * lax.rsqrt(ms + NORM_EPS) * g
    return y * (1.0 + scale) + shift


def _rope128(x, ct, sa, sb):
    return x * ct + pltpu.roll(x, LANES - ROPE_DIM // 2, 1) * sa + pltpu.roll(x, ROPE_DIM // 2, 1) * sb


def _adaln_body(c_ref, w_ref, b_ref, o_ref):
    c = c_ref[...]
    ca = c * jax.nn.sigmoid(c)
    o_ref[0] = jnp.dot(ca, w_ref[0], preferred_element_type=F32, precision=HIGHEST) + b_ref[0]


def _adaln(c, w_ada, b_ada):
    depth, d, d6 = w_ada.shape
    bn = c.shape[0]
    rows = 8
    cp = jnp.zeros((rows, d), F32).at[:bn].set(c)
    tn = 1536
    out = pl.pallas_call(
        _adaln_body,
        grid=(depth, d6 // tn),
        in_specs=[pl.BlockSpec((rows, d), lambda l, j: (0, 0)),
                  pl.BlockSpec((1, d, tn), lambda l, j: (l, 0, j)),
                  pl.BlockSpec((1, 1, tn), lambda l, j: (l, 0, j))],
        out_specs=pl.BlockSpec((1, rows, tn), lambda l, j: (l, 0, j)),
        out_shape=jax.ShapeDtypeStruct((depth, rows, d6), F32),
        compiler_params=_cparams(("arbitrary", "arbitrary")),
        name="adaln",
    )(cp, w_ada, b_ada.reshape(depth, 1, d6))
    return out[:, :bn]


def _rope_tables(seq, period):
    half = ROPE_DIM // 2
    pos = jnp.arange(seq, dtype=F32)
    inv = ROPE_THETA ** (-jnp.arange(0, ROPE_DIM, 2, dtype=F32) / ROPE_DIM)
    ang = pos[:, None] * inv[None, :]
    cos, sin = jnp.cos(ang), jnp.sin(ang)
    lane = np.arange(LANES) % period
    first = lane < half
    second = (lane >= half) & (lane < ROPE_DIM)
    idx = np.where(first, lane, np.where(second, lane - half, 0))
    cos_l, sin_l = cos[:, idx], sin[:, idx]
    ct = jnp.where(first | second, cos_l, 1.0)
    sa = jnp.where(first, -sin_l, 0.0)
    sb = jnp.where(second, sin_l, 0.0)
    return ct, sa, sb


def _proj0_body(x_ref, g_ref, sh_ref, sc_ref, w_ref, ct_ref, sa_ref, sb_ref, *rest):
    qkv_refs, s_ref, acc_scr = rest[:-2], rest[-2], rest[-1]
    tm = x_ref.shape[0]
    qw = 3 * A_WIDTH
    u = _norm_mod(x_ref[...], g_ref[...], sh_ref[0], sc_ref[0]).astype(BF16)
    ct, sa, sb = ct_ref[...], sa_ref[...], sb_ref[...]
    for j in range(3):
        acc = _dot(u, w_ref[:, j * A_WIDTH:(j + 1) * A_WIDTH])
        for c in range(A_WIDTH // LANES):
            a = acc[:, c * LANES:(c + 1) * LANES]
            if j < 2:
                a = _rope128(a, ct, sa, sb)
            if j == 0:
                a = a * (HEAD_DIM ** -0.5)
            acc_scr[j * (A_WIDTH // LANES) + c] = a
    acc = _dot(u, w_ref[:, qw:])
    nq = qw // LANES
    for c in range(B_WIDTH // LANES):
        acc_scr[nq + c] = acc[:, c * LANES:(c + 1) * LANES]
    for (_, dil), ref in zip(A_PATTERNS, qkv_refs):
        for r in range(dil):
            for c in range(nq):
                ref[:, r * qw + c * LANES:r * qw + (c + 1) * LANES] = (
                    acc_scr[c, pl.ds(r, tm // dil, stride=dil), :].astype(BF16))
    for r in range(S5_CHUNK):
        for c in range(B_WIDTH // LANES):
            s_ref[:, r * B_WIDTH + c * LANES:r * B_WIDTH + (c + 1) * LANES] = (
                acc_scr[nq + c, pl.ds(r, tm // S5_CHUNK, stride=S5_CHUNK), :])


def _proj0(x2, g, shift, scale, w_bf, tabs, seq, tm=512):
    t, d = x2.shape
    n = w_bf.shape[1]
    tpb = seq // tm
    ct, sa, sb = tabs
    tab_spec = pl.BlockSpec((tm, LANES), lambda i: (i % tpb, 0))
    mod_spec = pl.BlockSpec((1, 1, d), lambda i: (i // tpb, 0, 0))
    qw = 3 * A_WIDTH
    dils = [dil for _, dil in A_PATTERNS]
    out_specs = [pl.BlockSpec((tm // dil, dil * qw), lambda i: (i, 0)) for dil in dils]
    out_shape = [jax.ShapeDtypeStruct((t // dil, dil * qw), BF16) for dil in dils]
    out_specs.append(pl.BlockSpec((tm // S5_CHUNK, S5_CHUNK * B_WIDTH), lambda i: (i, 0)))
    out_shape.append(jax.ShapeDtypeStruct((t // S5_CHUNK, S5_CHUNK * B_WIDTH), F32))
    return pl.pallas_call(
        _proj0_body,
        grid=(t // tm,),
        in_specs=[pl.BlockSpec((tm, d), lambda i: (i, 0)),
                  pl.BlockSpec((1, d), lambda i: (0, 0)),
                  mod_spec, mod_spec,
                  pl.BlockSpec((d, n), lambda i: (0, 0)),
                  tab_spec, tab_spec, tab_spec],
        out_specs=out_specs,
        out_shape=out_shape,
        scratch_shapes=[pltpu.VMEM((n // LANES, tm, LANES), F32)],
        compiler_params=_cparams(("parallel",)),
        name="proj0",
    )(x2, g, shift, scale, w_bf, ct, sa, sb)


DIL_QB_MAX = 4


def _dil_body(q_ref, kp_ref, kc_ref, vp_ref, vc_ref, lse_ref, o_ref):
    i = pl.program_id(2)
    nqb = q_ref.shape[1] // Q_BLOCK
    kall = jnp.concatenate([kp_ref[0], kc_ref[0]], axis=0)
    vall = jnp.concatenate([vp_ref[0], vc_ref[0]], axis=0)
    row = lax.broadcasted_iota(I32, (Q_BLOCK, 2 * Q_BLOCK), 0)
    col = lax.broadcasted_iota(I32, (Q_BLOCK, 2 * Q_BLOCK), 1)
    rel = row + Q_BLOCK - col
    band = (rel >= 0) & (rel <= Q_BLOCK)
    lane = lax.broadcasted_iota(I32, (1, LANES), 1)
    npair = A_WIDTH // LANES
    hpp = LANES // HEAD_DIM
    scores = []
    for j in range(nqb):
        valid = band if j > 0 else band & ((col >= Q_BLOCK) | (i > 0))
        qj = q_ref[0, j * Q_BLOCK:(j + 1) * Q_BLOCK, :]
        kj = kall[j * Q_BLOCK:(j + 2) * Q_BLOCK]
        for p in range(npair):
            sl = slice(p * LANES, (p + 1) * LANES)
            for hh in range(hpp):
                hm = (lane // HEAD_DIM) == hh
                qh = jnp.where(hm, qj[:, sl], jnp.zeros_like(qj[:, sl]))
                scores.append(jnp.where(valid, _dot_nt(qh, kj[:, sl]), -jnp.inf))
    s = jnp.concatenate(scores, axis=0)
    m = jnp.max(s, axis=1, keepdims=True)
    pr = jnp.exp(s - m)
    l = jnp.sum(pr, axis=1, keepdims=True)
    prb = pr.astype(BF16)
    for j in range(nqb):
        vj = vall[j * Q_BLOCK:(j + 2) * Q_BLOCK]
        qs = slice(j * Q_BLOCK, (j + 1) * Q_BLOCK)
        for p in range(npair):
            sl = slice(p * LANES, (p + 1) * LANES)
            m_pair = l_pair = o_pair = None
            for hh in range(hpp):
                h = (j * npair + p) * hpp + hh
                rs = slice(h * Q_BLOCK, (h + 1) * Q_BLOCK)
                hm = (lane // HEAD_DIM) == hh
                o = _dot(prb[rs], vj[:, sl])
                if hh == 0:
                    m_pair = jnp.broadcast_to(m[rs], (Q_BLOCK, LANES))
                    l_pair = jnp.broadcast_to(l[rs], (Q_BLOCK, LANES))
                    o_pair = o
                else:
                    m_pair = jnp.where(hm, m[rs], m_pair)
                    l_pair = jnp.where(hm, l[rs], l_pair)
                    o_pair = jnp.where(hm, o, o_pair)
            lse_ref[0, qs, sl] = m_pair + jnp.log(l_pair)
            o_ref[0, qs, sl] = (o_pair / l_pair).astype(o_ref.dtype)


def _dilated_branch(qkv_d, dil, bn, seq):
    sd = seq // dil
    nqb = min(DIL_QB_MAX, sd // Q_BLOCK)
    nb = sd // (nqb * Q_BLOCK)
    cur = (1, nqb * Q_BLOCK, A_WIDTH)
    prev = (1, Q_BLOCK, A_WIDTH)
    q_spec = pl.BlockSpec(cur, lambda b, r, i: (b, i, r * 3))
    kp_spec = pl.BlockSpec(prev, lambda b, r, i: (b, jnp.maximum(nqb * i - 1, 0), r * 3 + 1))
    kc_spec = pl.BlockSpec(cur, lambda b, r, i: (b, i, r * 3 + 1))
    vp_spec = pl.BlockSpec(prev, lambda b, r, i: (b, jnp.maximum(nqb * i - 1, 0), r * 3 + 2))
    vc_spec = pl.BlockSpec(cur, lambda b, r, i: (b, i, r * 3 + 2))
    st_spec = pl.BlockSpec(cur, lambda b, r, i: (b, i, r))
    lse, out = pl.pallas_call(
        _dil_body,
        grid=(bn, dil, nb),
        in_specs=[q_spec, kp_spec, kc_spec, vp_spec, vc_spec],
        out_specs=[st_spec] * 2,
        out_shape=[jax.ShapeDtypeStruct((bn, sd, dil * A_WIDTH), F32),
                   jax.ShapeDtypeStruct((bn, sd, dil * A_WIDTH), BF16)],
        compiler_params=_cparams(("parallel", "parallel", "arbitrary")),
        name=f"dilated_d{dil}",
    )(*([qkv_d.reshape(bn, sd, dil * 3 * A_WIDTH)] * 5))
    return [lse.reshape(bn * sd, dil * A_WIDTH), out.reshape(bn * sd, dil * A_WIDTH)]


def _dil_merge_body(*refs):
    nbr = len(A_PATTERNS)
    stats = refs[:2 * nbr]
    y_ref = refs[2 * nbr]
    scr = refs[2 * nbr + 1:]
    tm = y_ref.shape[0]
    nc = A_WIDTH // LANES
    lses, outs = [], []
    si = 0
    for gi, (_, dil) in enumerate(A_PATTERNS):
        pair = []
        for ref in stats[2 * gi:2 * gi + 2]:
            if dil == 1:
                pair.append(ref[...].astype(F32))
            else:
                for r in range(dil):
                    for c in range(nc):
                        scr[si][c, pl.ds(r, tm // dil, stride=dil), :] = (
                            ref[:, r * A_WIDTH + c * LANES:r * A_WIDTH + (c + 1) * LANES].astype(F32))
                pair.append(jnp.concatenate([scr[si][c] for c in range(nc)], axis=1))
                si += 1
        lses.append(pair[0])
        outs.append(pair[1])
    top = functools.reduce(jnp.maximum, lses)
    den = jnp.zeros_like(top)
    num = jnp.zeros_like(top)
    for lse, o in zip(lses, outs):
        w = jnp.exp(lse - top)
        den = den + w
        num = num + w * o
    y_ref[...] = (num / den).astype(y_ref.dtype)


def _dilated_attention(qkv_views, bn, seq, tm=512):
    t = bn * seq
    stats, in_specs = [], []
    nscr = 0
    for (_, dil), qkv_d in zip(A_PATTERNS, qkv_views):
        stats += _dilated_branch(qkv_d, dil, bn, seq)
        in_specs += [pl.BlockSpec((tm // dil, dil * A_WIDTH), lambda i: (i, 0))] * 2
        nscr += 2 if dil > 1 else 0
    return pl.pallas_call(
        _dil_merge_body,
        grid=(t // tm,),
        in_specs=in_specs,
        out_specs=pl.BlockSpec((tm, A_WIDTH), lambda i: (i, 0)),
        out_shape=jax.ShapeDtypeStruct((t, A_WIDTH), BF16),
        scratch_shapes=[pltpu.VMEM((A_WIDTH // LANES, tm, LANES), F32)] * nscr,
        compiler_params=_cparams(("parallel",)),
        name="dilated_merge",
    )(*stats)


def _s5_prepare(a_re, a_im, log_dt, b_re, b_im, c_re, c_im, d_skip, nchunk):
    dt = jnp.exp(log_dt.astype(F32))[:, None]
    lr, li = a_re.astype(F32), a_im.astype(F32)
    mag = jnp.exp(lr * dt)
    ar = mag * jnp.cos(li * dt)
    ai = mag * jnp.sin(li * dt)
    den = lr * lr + li * li
    nr = ar - 1.0
    fr = (nr * lr + ai * li) / den
    fi = (ai * lr - nr * li) / den
    br, bi = b_re.astype(F32), b_im.astype(F32)
    bbr = fr[..., None] * br - fi[..., None] * bi
    bbi = fr[..., None] * bi + fi[..., None] * br
    nblk = B_WIDTH // LANES
    gpb = B_GROUPS // nblk
    eye = jnp.eye(gpb, dtype=F32)

    def bdiag_in(m):
        m = jnp.swapaxes(m.reshape(nblk, gpb, B_STATE, B_GROUP), 2, 3)
        m = m[:, :, :, None, :] * eye[None, :, None, :, None]
        return m.reshape(nblk, gpb * B_GROUP, gpb * B_STATE)

    def bdiag_out(m):
        m = jnp.swapaxes(m.reshape(nblk, gpb, B_GROUP, B_STATE), 2, 3)
        m = m[:, :, :, None, :] * eye[None, :, None, :, None]
        return m.reshape(nblk, gpb * B_STATE, gpb * B_GROUP)

    pr, pi = ar, ai
    for _ in range(int(math.log2(S5_CHUNK))):
        pr, pi = pr * pr - pi * pi, 2.0 * pr * pi
    pows_r, pows_i = [], []
    for _ in range(max(1, int(math.ceil(math.log2(nchunk))))):
        pows_r.append(pr.reshape(1, -1))
        pows_i.append(pi.reshape(1, -1))
        pr, pi = pr * pr - pi * pi, 2.0 * pr * pi
    return dict(
        ar=ar.reshape(1, -1), ai=ai.reshape(1, -1),
        b_re=bdiag_in(bbr).astype(BF16), b_im=bdiag_in(bbi).astype(BF16),
        c_re=bdiag_out(c_re.astype(F32)).astype(BF16), c_im=bdiag_out(-c_im.astype(F32)).astype(BF16),
        d=d_skip.astype(F32).reshape(1, B_WIDTH),
        pows_r=jnp.concatenate(pows_r, axis=0), pows_i=jnp.concatenate(pows_i, axis=0))


def _s5_local_body(u_ref, bre_ref, bim_ref, ar_ref, ai_ref, sre_ref, sim_ref):
    tr = u_ref.shape[0]
    sw = bre_ref.shape[2]
    for blk in range(B_WIDTH // LANES):
        arb = ar_ref[:, blk * sw:(blk + 1) * sw]
        aib = ai_ref[:, blk * sw:(blk + 1) * sw]
        sr = jnp.zeros((tr, sw), F32)
        si = jnp.zeros((tr, sw), F32)
        for t in range(S5_CHUNK):
            ub = u_ref[:, t * B_WIDTH + blk * LANES:t * B_WIDTH + (blk + 1) * LANES].astype(BF16)
            sr, si = (arb * sr - aib * si + _dot(ub, bre_ref[blk]),
                      arb * si + aib * sr + _dot(ub, bim_ref[blk]))
        sre_ref[:, blk * sw:(blk + 1) * sw] = sr
        sim_ref[:, blk * sw:(blk + 1) * sw] = si


def _s5_scan_body(nsteps, sre_ref, sim_ref, pr_ref, pi_ref, xre_ref, xim_ref):
    nc = sre_ref.shape[0]
    sw = 512
    row = lax.broadcasted_iota(I32, (nc, 1), 0)
    for cb in range(sre_ref.shape[1] // sw):
        sl = slice(cb * sw, (cb + 1) * sw)
        xr, xi = sre_ref[:, sl], sim_ref[:, sl]
        for s in range(nsteps):
            sh = 1 << s
            ok = row >= sh
            rr = jnp.where(ok, pltpu.roll(xr, sh, 0), 0.0)
            ri = jnp.where(ok, pltpu.roll(xi, sh, 0), 0.0)
            pr, pi = pr_ref[s:s + 1, sl], pi_ref[s:s + 1, sl]
            xr, xi = xr + pr * rr - pi * ri, xi + pr * ri + pi * rr
        ok = row >= 1
        xre_ref[:, sl] = jnp.where(ok, pltpu.roll(xr, 1, 0), 0.0)
        xim_ref[:, sl] = jnp.where(ok, pltpu.roll(xi, 1, 0), 0.0)


def _s5_out_body(u_ref, xre_ref, xim_ref, bre_ref, bim_ref, cre_ref, cim_ref, ar_ref, ai_ref,
                 d_ref, wglu_ref, bglu_ref, out_ref, y_scr):
    sw = bre_ref.shape[2]
    for blk in range(B_WIDTH // LANES):
        arb = ar_ref[:, blk * sw:(blk + 1) * sw]
        aib = ai_ref[:, blk * sw:(blk + 1) * sw]
        xr = xre_ref[:, blk * sw:(blk + 1) * sw]
        xi = xim_ref[:, blk * sw:(blk + 1) * sw]
        db = d_ref[:, blk * LANES:(blk + 1) * LANES]
        for t in range(S5_CHUNK):
            cs = slice(t * B_WIDTH + blk * LANES, t * B_WIDTH + (blk + 1) * LANES)
            uf = u_ref[:, cs]
            ub = uf.astype(BF16)
            xr, xi = (arb * xr - aib * xi + _dot(ub, bre_ref[blk]),
                      arb * xi + aib * xr + _dot(ub, bim_ref[blk]))
            y = _dot(xr.astype(BF16), cre_ref[blk]) + _dot(xi.astype(BF16), cim_ref[blk]) + db * uf
            y_scr[:, cs] = jax.nn.gelu(y, approximate=True)
    for t in range(S5_CHUNK):
        cs = slice(t * B_WIDTH, (t + 1) * B_WIDTH)
        y = y_scr[:, cs]
        z = _dot(y.astype(BF16), wglu_ref[...]) + bglu_ref[...]
        out_ref[:, cs] = (y * jax.nn.sigmoid(z)).astype(out_ref.dtype)


def _s5_mixer(s_in, prm, w_glu_bf, b_glu, bn, seq, tr=128, tr_out=256):
    t = bn * seq
    nrow = t // S5_CHUNK
    ncb = seq // S5_CHUNK
    width = S5_CHUNK * B_WIDTH
    sdim = B_GROUPS * B_STATE
    uc = s_in.reshape(nrow, width)
    nblk = B_WIDTH // LANES
    const3 = lambda i: (0, 0, 0)
    const2 = lambda i: (0, 0)
    b_spec = pl.BlockSpec((nblk, LANES, sdim // nblk), const3)
    c_spec = pl.BlockSpec((nblk, sdim // nblk, LANES), const3)
    a_spec = pl.BlockSpec((1, sdim), const2)
    s_re, s_im = pl.pallas_call(
        _s5_local_body,
        grid=(nrow // tr,),
        in_specs=[pl.BlockSpec((tr, width), lambda i: (i, 0)), b_spec, b_spec, a_spec, a_spec],
        out_specs=[pl.BlockSpec((tr, sdim), lambda i: (i, 0))] * 2,
        out_shape=[jax.ShapeDtypeStruct((nrow, sdim), F32)] * 2,
        compiler_params=_cparams(("parallel",)),
        name="s5_local",
    )(uc, prm['b_re'], prm['b_im'], prm['ar'], prm['ai'])
    nsteps = prm['pows_r'].shape[0]
    x_re, x_im = pl.pallas_call(
        functools.partial(_s5_scan_body, nsteps),
        grid=(bn,),
        in_specs=[pl.BlockSpec((ncb, sdim), lambda b: (b, 0))] * 2
                 + [pl.BlockSpec((nsteps, sdim), lambda b: (0, 0))] * 2,
        out_specs=[pl.BlockSpec((ncb, sdim), lambda b: (b, 0))] * 2,
        out_shape=[jax.ShapeDtypeStruct((nrow, sdim), F32)] * 2,
        compiler_params=_cparams(("parallel",)),
        name="s5_scan",
    )(s_re, s_im, prm['pows_r'], prm['pows_i'])
    y = pl.pallas_call(
        _s5_out_body,
        grid=(nrow // tr_out,),
        in_specs=[pl.BlockSpec((tr_out, width), lambda i: (i, 0)),
                  pl.BlockSpec((tr_out, sdim), lambda i: (i, 0)),
                  pl.BlockSpec((tr_out, sdim), lambda i: (i, 0)),
                  b_spec, b_spec, c_spec, c_spec, a_spec, a_spec,
                  pl.BlockSpec((1, B_WIDTH), const2),
                  pl.BlockSpec((B_WIDTH, B_WIDTH), const2),
                  pl.BlockSpec((1, B_WIDTH), const2)],
        out_specs=pl.BlockSpec((tr_out, width), lambda i: (i, 0)),
        out_shape=jax.ShapeDtypeStruct((nrow, width), BF16),
        scratch_shapes=[pltpu.VMEM((tr_out, width), F32)],
        compiler_params=_cparams(("parallel",)),
        name="s5_out",
    )(uc, x_re, x_im, prm['b_re'], prm['b_im'], prm['c_re'], prm['c_im'], prm['ar'], prm['ai'],
      prm['d'], w_glu_bf, b_glu.reshape(1, B_WIDTH).astype(F32))
    return y


def _outproj_body(nparts, nflat, *refs):
    parts = refs[:nparts]
    flats = refs[nparts:nparts + nflat]
    w_ref, h_ref, gt_ref, o_ref = refs[nparts + nflat:nparts + nflat + 4]
    scr = refs[nparts + nflat + 4:]
    tm = h_ref.shape[0]
    acc = None
    off = 0
    for p in parts:
        k = p.shape[1]
        d = _dot(p[...].astype(BF16), w_ref[off:off + k, :])
        acc = d if acc is None else acc + d
        off += k
    for p, s in zip(flats, scr):
        nc = s.shape[0]
        k = nc * LANES
        for r in range(S5_CHUNK):
            for c in range(nc):
                s[c, pl.ds(r, tm // S5_CHUNK, stride=S5_CHUNK), :] = (
                    p[:, r * k + c * LANES:r * k + (c + 1) * LANES].astype(F32))
        rows = jnp.concatenate([s[c] for c in range(nc)], axis=1)
        d = _dot(rows.astype(BF16), w_ref[off:off + k, :])
        acc = d if acc is None else acc + d
        off += k
    o_ref[...] = h_ref[...] + gt_ref[0] * acc


def _outproj(parts, flat_parts, w_bf, h2, gate, seq, tm=512):
    t, d = h2.shape
    tpb = seq // tm
    in_specs = [pl.BlockSpec((tm, p.shape[1]), lambda i: (i, 0)) for p in parts]
    in_specs += [pl.BlockSpec((tm // S5_CHUNK, p.shape[1]), lambda i: (i, 0)) for p in flat_parts]
    in_specs += [pl.BlockSpec(w_bf.shape, lambda i: (0, 0)),
                 pl.BlockSpec((tm, d), lambda i: (i, 0)),
                 pl.BlockSpec((1, 1, d), lambda i: (i // tpb, 0, 0))]
    return pl.pallas_call(
        functools.partial(_outproj_body, len(parts), len(flat_parts)),
        grid=(t // tm,),
        in_specs=in_specs,
        out_specs=pl.BlockSpec((tm, d), lambda i: (i, 0)),
        out_shape=jax.ShapeDtypeStruct((t, d), F32),
        scratch_shapes=[pltpu.VMEM((p.shape[1] // S5_CHUNK // LANES, tm, LANES), F32) for p in flat_parts],
        compiler_params=_cparams(("parallel",)),
        name="outproj",
    )(*parts, *flat_parts, w_bf, h2, gate)


def _ffn_body(h_ref, g_ref, sh_ref, sc_ref, gt_ref, wg_ref, wu_ref, wd_ref, o_ref, u_scr, acc_scr):
    f = pl.program_id(1)

    @pl.when(f == 0)
    def _():
        u_scr[...] = _norm_mod(h_ref[...], g_ref[...], sh_ref[0], sc_ref[0]).astype(BF16)
        acc_scr[...] = jnp.zeros_like(acc_scr)

    u = u_scr[...]
    gp = _dot(u, wg_ref[...])
    up = _dot(u, wu_ref[...])
    a = (gp * jax.nn.sigmoid(gp) * up).astype(BF16)
    acc_scr[...] += _dot(a, wd_ref[...])

    @pl.when(f == pl.num_programs(1) - 1)
    def _():
        o_ref[...] = h_ref[...] + gt_ref[0] * acc_scr[...]


def _ffn_dense(h2, g, shift, scale, gate, wg, wu, wd, seq, tm=1024, tf=FFN_TF):
    t, d = h2.shape
    ff = wg.shape[1]
    tpb = seq // tm
    mod_spec = pl.BlockSpec((1, 1, d), lambda i, f: (i // tpb, 0, 0))
    return pl.pallas_call(
        _ffn_body,
        grid=(t // tm, ff // tf),
        in_specs=[pl.BlockSpec((tm, d), lambda i, f: (i, 0)),
                  pl.BlockSpec((1, d), lambda i, f: (0, 0)),
                  mod_spec, mod_spec, mod_spec,
                  pl.BlockSpec((d, tf), lambda i, f: (0, f)),
                  pl.BlockSpec((d, tf), lambda i, f: (0, f)),
                  pl.BlockSpec((tf, d), lambda i, f: (f, 0))],
        out_specs=pl.BlockSpec((tm, d), lambda i, f: (i, 0)),
        out_shape=jax.ShapeDtypeStruct((t, d), F32),
        scratch_shapes=[pltpu.VMEM((tm, d), BF16), pltpu.VMEM((tm, d), F32)],
        compiler_params=_cparams(("parallel", "arbitrary")),
        name="ffn_dense",
    )(h2, g, shift, scale, gate, wg, wu, wd)


P1_QR = 0
P1_QN = 256
P1_CKV = 1280
P1_KR = 1536
P1_QI = 1792
P1_KI = 2304
P1_WI = 2432
P1_COLS = 2560


def _proj1_layout(w_in):
    d = w_in.shape[0]
    c0 = 0
    w_qr = w_in[:, c0:c0 + C_HEADS * ROPE_DIM]; c0 += C_HEADS * ROPE_DIM
    w_qn = w_in[:, c0:c0 + C_HEADS * C_NOPE]; c0 += C_HEADS * C_NOPE
    w_ckv = w_in[:, c0:c0 + C_LATENT]; c0 += C_LATENT
    w_kr = w_in[:, c0:c0 + ROPE_DIM]; c0 += ROPE_DIM
    w_qi = w_in[:, c0:c0 + IDX_HEADS * IDX_DIM]; c0 += IDX_HEADS * IDX_DIM
    w_ki = w_in[:, c0:c0 + IDX_DIM]; c0 += IDX_DIM
    w_wi = w_in[:, c0:c0 + IDX_HEADS]
    w_qn = jnp.pad(w_qn.reshape(d, C_HEADS, C_NOPE), ((0, 0), (0, 0), (0, HEAD_DIM - C_NOPE)))
    w_qn = w_qn.reshape(d, C_HEADS * HEAD_DIM)
    w_kr = jnp.tile(w_kr, (1, C_HEADS))
    w_ki = jnp.tile(w_ki, (1, 2))
    w_wi = jnp.pad(w_wi, ((0, 0), (0, LANES - IDX_HEADS)))
    return jnp.concatenate([w_qr, w_qn, w_ckv, w_kr, w_qi, w_ki, w_wi], axis=1)


def _proj1_body(x_ref, g_ref, sh_ref, sc_ref, w_ref, gkv_ref,
                ct16_ref, sa16_ref, sb16_ref, ct64_ref, sa64_ref, sb64_ref,
                qa_ref, kp_ref, qi_ref, ki_ref, wi_ref):
    u = _norm_mod(x_ref[...], g_ref[...], sh_ref[0], sc_ref[0]).astype(BF16)
    t16 = (ct16_ref[...], sa16_ref[...], sb16_ref[...])
    t64 = (ct64_ref[...], sa64_ref[...], sb64_ref[...])
    qscale = HEAD_DIM ** -0.5 * math.log2(math.e)

    def cols(lo, hi):
        return _dot(u, w_ref[:, lo:hi])

    a = cols(P1_QR, P1_QN)
    for c in range(2):
        blk = _rope128(a[:, c * LANES:(c + 1) * LANES], *t16) * qscale
        qa_ref[:, c * LANES:(c + 1) * LANES] = blk.astype(BF16)
    qa_ref[:, P1_QN:P1_CKV] = (cols(P1_QN, P1_CKV) * qscale).astype(BF16)
    ckv = cols(P1_CKV, P1_KR)
    ms = jnp.mean(ckv * ckv, axis=-1, keepdims=True)
    kp_ref[:, 0:C_LATENT] = (ckv * lax.rsqrt(ms + NORM_EPS) * gkv_ref[...]).astype(BF16)
    a = cols(P1_KR, P1_QI)
    for c in range(2):
        blk = _rope128(a[:, c * LANES:(c + 1) * LANES], *t16)
        kp_ref[:, C_LATENT + c * LANES:C_LATENT + (c + 1) * LANES] = blk.astype(BF16)
    a = cols(P1_QI, P1_KI)
    for c in range(4):
        qi_ref[:, c * LANES:(c + 1) * LANES] = _rope128(a[:, c * LANES:(c + 1) * LANES], *t64).astype(BF16)
    ki_ref[...] = _rope128(cols(P1_KI, P1_WI), *t64).astype(BF16)
    wi_ref[...] = cols(P1_WI, P1_COLS) * (IDX_HEADS ** -0.5) * (IDX_DIM ** -0.5)


def _proj1(x2, g, shift, scale, w_bf, gkv, t16, t64, seq, tm=512):
    t, d = x2.shape
    tpb = seq // tm
    tab_spec = pl.BlockSpec((tm, LANES), lambda i: (i % tpb, 0))
    mod_spec = pl.BlockSpec((1, 1, d), lambda i: (i // tpb, 0, 0))
    widths = (P1_CKV, 2 * C_LATENT, IDX_HEADS * IDX_DIM, LANES, LANES)
    dtypes = (BF16, BF16, BF16, BF16, F32)
    return pl.pallas_call(
        _proj1_body,
        grid=(t // tm,),
        in_specs=[pl.BlockSpec((tm, d), lambda i: (i, 0)),
                  pl.BlockSpec((1, d), lambda i: (0, 0)),
                  mod_spec, mod_spec,
                  pl.BlockSpec((d, P1_COLS), lambda i: (0, 0)),
                  pl.BlockSpec((1, C_LATENT), lambda i: (0, 0))] + [tab_spec] * 6,
        out_specs=[pl.BlockSpec((tm, w), lambda i: (i, 0)) for w in widths],
        out_shape=[jax.ShapeDtypeStruct((t, w), dt) for w, dt in zip(widths, dtypes)],
        compiler_params=_cparams(("parallel",)),
        name="proj1",
    )(x2, g, shift, scale, w_bf, gkv, *t16, *t64)


IDX_KT = 512


def _idx_body(topk, qi_ref, ki_ref, wi_ref, mask_ref, sc_scr):
    i = pl.program_id(1)
    seq = ki_ref.shape[1]
    nkt = seq // IDX_KT
    nlive = (i * Q_BLOCK) // IDX_KT + 1
    q = qi_ref[0]
    wt = wi_ref[0].T
    lane = lax.broadcasted_iota(I32, (1, LANES), 1)
    qpos = i * Q_BLOCK + lane
    krow = lax.broadcasted_iota(I32, (IDX_KT, 1), 0)
    qpair = [jnp.concatenate([q[:, (2 * g) * LANES:(2 * g + 1) * LANES],
                              q[:, (2 * g + 1) * LANES:(2 * g + 2) * LANES]], axis=0) for g in range(2)]

    def score_tile(kt, c):
        kk = ki_ref[0, pl.ds(pl.multiple_of(kt * IDX_KT, IDX_KT), IDX_KT), :]
        zero = jnp.zeros_like(kk)
        kpart = [jnp.where(lane < IDX_DIM, kk, zero), jnp.where(lane >= IDX_DIM, kk, zero)]
        acc = jnp.zeros((IDX_KT, Q_BLOCK), F32)
        for g in range(2):
            for part in range(2):
                res = jnp.maximum(_dot_nt(kpart[part], qpair[g]), 0.0)
                for j in range(2):
                    h = 2 * (2 * g + j) + part
                    acc = acc + wt[h:h + 1, :] * res[:, j * Q_BLOCK:(j + 1) * Q_BLOCK]
        acc = acc + 0.0
        sc_scr[kt] = jnp.where(kt * IDX_KT + krow <= qpos, acc, -jnp.inf)
        return c

    lax.fori_loop(0, nlive, score_tile, 0)
    kq = jnp.minimum(qpos + 1, topk).astype(F32)

    def count(pred_fn):
        def body(kt, acc):
            ones = jnp.where(pred_fn(sc_scr[kt], kt), 1.0, 0.0)
            part = jnp.sum(ones.reshape(8, IDX_KT // 64, 8, Q_BLOCK), axis=1)
            return acc + jnp.sum(part, axis=0)
        part = lax.fori_loop(0, nlive, body, jnp.zeros((8, Q_BLOCK), F32))
        return jnp.sum(part, axis=0, keepdims=True)

    def key_to_f32(key):
        bits = jnp.where(key < 0, key ^ jnp.int32(0x7FFFFFFF), key)
        return lax.bitcast_convert_type(bits, F32)

    def bit_step(it, ans):
        cand = ans | lax.shift_left(jnp.int32(1), 31 - it)
        thr = key_to_f32(cand ^ jnp.int32(INT_MIN))
        cnt = count(lambda sc, kt: sc >= thr)
        return jnp.where(cnt >= kq, cand, ans)

    ans = lax.fori_loop(0, 32, bit_step, jnp.zeros((1, Q_BLOCK), I32))
    thr = key_to_f32(ans ^ jnp.int32(INT_MIN))
    n_ge = count(lambda sc, kt: sc >= thr)
    nbits = int(math.log2(seq))

    def tie_cut(_):
        need = kq - count(lambda sc, kt: sc > thr)

        def tie_step(it, ans2):
            cand = ans2 | lax.shift_left(jnp.int32(1), nbits - 1 - it)
            cnt = count(lambda sc, kt: (sc == thr) & (kt * IDX_KT + krow < cand))
            return jnp.where(cnt < need, cand, ans2)

        return lax.fori_loop(0, nbits, tie_step, jnp.zeros((1, Q_BLOCK), I32))

    excess = jnp.max(n_ge - kq) > 0.0
    jcut = lax.cond(excess, tie_cut, lambda _: jnp.full((1, Q_BLOCK), seq, I32), 0)

    def emit(kt, c):
        sc = sc_scr[kt]
        sel = (sc > thr) | ((sc == thr) & (kt * IDX_KT + krow <= jcut))
        mask_ref[0, 0, kt] = jnp.where(sel, 1.0, 0.0).T.astype(BF16)
        return c

    lax.fori_loop(0, nlive, emit, 0)

    def emit_dead(kt, c):
        mask_ref[0, 0, kt] = jnp.zeros((Q_BLOCK, IDX_KT), BF16)
        return c

    lax.fori_loop(nlive, nkt, emit_dead, 0)


def _dsa_select(qi, ki, wi, bn, seq):
    topk = min(TOPK_MAX, seq // 4)
    nqb = seq // Q_BLOCK
    nkt = seq // IDX_KT
    return pl.pallas_call(
        functools.partial(_idx_body, topk),
        grid=(bn, nqb),
        in_specs=[pl.BlockSpec((1, Q_BLOCK, IDX_HEADS * IDX_DIM), lambda b, i: (b, i, 0)),
                  pl.BlockSpec((1, seq, LANES), lambda b, i: (b, 0, 0)),
                  pl.BlockSpec((1, Q_BLOCK, LANES), lambda b, i: (b, i, 0))],
        out_specs=pl.BlockSpec((1, 1, nkt, Q_BLOCK, IDX_KT), lambda b, i: (b, i, 0, 0, 0)),
        out_shape=jax.ShapeDtypeStruct((bn, nqb, nkt, Q_BLOCK, IDX_KT), BF16),
        scratch_shapes=[pltpu.VMEM((nkt, IDX_KT, Q_BLOCK), F32)],
        compiler_params=_cparams(("parallel", "arbitrary")),
        name="dsa_select",
    )(qi.reshape(bn, seq, -1), ki.reshape(bn, seq, -1), wi.reshape(bn, seq, -1))


ATT_RB = 512


def _dsa_attn_body(qa_ref, qan_ref, kp_ref, mask_ref, wuk_ref, wuv_ref, o_ref, qp_scr, m_scr, l_scr,
                   acc_scr, s_scr):
    i = pl.program_id(1)
    rows = C_HEADS * Q_BLOCK
    nblk = rows // ATT_RB
    cur = lax.rem(i, 2)
    nxt = 1 - cur
    lane = lax.broadcasted_iota(I32, (1, C_HEADS * ROPE_DIM), 1)

    def prep(src_ref, slot):
        qr = src_ref[0, :, 0:P1_QN]
        for h in range(C_HEADS):
            p = h // 2
            qn = src_ref[0, :, P1_QN + p * LANES:P1_QN + (p + 1) * LANES]
            qlat = _dot(qn, wuk_ref[h])
            qp_scr[slot, h * Q_BLOCK:(h + 1) * Q_BLOCK, 0:C_LATENT] = qlat.astype(BF16)
            qp_scr[slot, h * Q_BLOCK:(h + 1) * Q_BLOCK, C_LATENT:2 * C_LATENT] = jnp.where(
                (lane // ROPE_DIM) == h, qr, jnp.zeros_like(qr))

    def ktile(kt):
        return kp_ref[0, pl.ds(pl.multiple_of(kt * IDX_KT, IDX_KT), IDX_KT), :]

    def tile_bias(kt):
        bias = (mask_ref[0, 0, kt].astype(F32) - 1.0) * (-NEG_BIG)
        return jnp.concatenate([bias] * (ATT_RB // Q_BLOCK), axis=0)

    def softmax_pv(b, s, bias, vv):
        rs = slice(b * ATT_RB, (b + 1) * ATT_RB)
        s = s + bias
        m_prev = m_scr[rs, :]
        m_next = jnp.maximum(m_prev, jnp.max(s, axis=1, keepdims=True))
        alpha = jnp.exp2(m_prev - m_next)
        pr = jnp.exp2(s - jnp.concatenate([m_next] * (IDX_KT // LANES), axis=1))
        l_scr[rs, :] = alpha * l_scr[rs, :] + jnp.sum(pr, axis=1, keepdims=True)
        m_scr[rs, :] = m_next
        acc_scr[rs, :] = (jnp.concatenate([alpha] * (C_LATENT // LANES), axis=1) * acc_scr[rs, :]
                          + _dot(pr.astype(BF16), vv))

    def tile_step(kt, slot_next, kk_next):
        vv = ktile(kt)[:, 0:C_LATENT]
        bias = tile_bias(kt)
        for b in range(nblk):
            rs = slice(b * ATT_RB, (b + 1) * ATT_RB)
            s = s_scr[rs, :]
            s_scr[rs, :] = _dot_nt(qp_scr[slot_next, rs, :], kk_next)
            softmax_pv(b, s, bias, vv)

    @pl.when(i == 0)
    def _():
        prep(qa_ref, 0)
        kk0 = ktile(0)
        for b in range(nblk):
            rs = slice(b * ATT_RB, (b + 1) * ATT_RB)
            s_scr[rs, :] = _dot_nt(qp_scr[0, rs, :], kk0)

    m_scr[...] = jnp.full(m_scr.shape, NEG_BIG, F32)
    l_scr[...] = jnp.zeros_like(l_scr)
    acc_scr[...] = jnp.zeros_like(acc_scr)
    last = (i * Q_BLOCK) // IDX_KT

    def step(kt, carry):
        tile_step(kt, cur, ktile(kt + 1))
        return carry

    lax.fori_loop(0, last, step, 0)
    prep(qan_ref, nxt)
    tile_step(last, nxt, ktile(0))
    olat = (acc_scr[...] / jnp.concatenate([l_scr[...]] * (C_LATENT // LANES), axis=1)).astype(BF16)
    for p in range(C_HEADS // 2):
        o = (_dot(olat[(2 * p) * Q_BLOCK:(2 * p + 1) * Q_BLOCK], wuv_ref[2 * p])
             + _dot(olat[(2 * p + 1) * Q_BLOCK:(2 * p + 2) * Q_BLOCK], wuv_ref[2 * p + 1]))
        o_ref[0, :, p * LANES:(p + 1) * LANES] = o.astype(o_ref.dtype)


def _dsa_attention(qa, kp, mask, wuk_x, wuv_x, bn, seq):
    nqb = seq // Q_BLOCK
    nkt = seq // IDX_KT
    rows = C_HEADS * Q_BLOCK
    return pl.pallas_call(
        _dsa_attn_body,
        grid=(bn, nqb),
        in_specs=[pl.BlockSpec((1, Q_BLOCK, P1_CKV), lambda b, i: (b, i, 0)),
                  pl.BlockSpec((1, Q_BLOCK, P1_CKV), lambda b, i: (b, jnp.minimum(i + 1, nqb - 1), 0)),
                  pl.BlockSpec((1, seq, 2 * C_LATENT), lambda b, i: (b, 0, 0)),
                  pl.BlockSpec((1, 1, nkt, Q_BLOCK, IDX_KT), lambda b, i: (b, i, 0, 0, 0)),
                  pl.BlockSpec((C_HEADS, LANES, C_LATENT), lambda b, i: (0, 0, 0)),
                  pl.BlockSpec((C_HEADS, C_LATENT, LANES), lambda b, i: (0, 0, 0))],
        out_specs=pl.BlockSpec((1, Q_BLOCK, C_HEADS * HEAD_DIM), lambda b, i: (b, i, 0)),
        out_shape=jax.ShapeDtypeStruct((bn, seq, C_HEADS * HEAD_DIM), BF16),
        scratch_shapes=[pltpu.VMEM((2, rows, 2 * C_LATENT), BF16),
                        pltpu.VMEM((rows, LANES), F32), pltpu.VMEM((rows, LANES), F32),
                        pltpu.VMEM((rows, C_LATENT), F32),
                        pltpu.VMEM((rows, IDX_KT), F32)],
        compiler_params=_cparams(("arbitrary", "arbitrary")),
        name="dsa_attn",
    )(qa.reshape(bn, seq, -1), qa.reshape(bn, seq, -1), kp.reshape(bn, seq, -1), mask, wuk_x, wuv_x)


def _mla_weights(w_uk, w_uv):
    wuk = jnp.transpose(w_uk, (1, 2, 0))
    wuk_x = jnp.zeros((C_HEADS, LANES, C_LATENT), F32)
    wuv = jnp.transpose(w_uv, (1, 0, 2))
    wuv_x = jnp.zeros((C_HEADS, C_LATENT, LANES), F32)
    for h in range(C_HEADS):
        o = (h % 2) * HEAD_DIM
        wuk_x = wuk_x.at[h, o:o + C_NOPE, :].set(wuk[h])
        wuv_x = wuv_x.at[h, :, o:o + HEAD_DIM].set(wuv[h])
    return wuk_x.astype(BF16), wuv_x.astype(BF16)


def _router_body(h_ref, g_ref, sh_ref, sc_ref, wr_ref, br_ref, route_ref, up_ref, cnt_ref, carry_scr):
    i = pl.program_id(0)
    tm = h_ref.shape[0]

    @pl.when(i == 0)
    def _():
        carry_scr[...] = jnp.zeros_like(carry_scr)

    u = _norm_mod(h_ref[...], g_ref[...], sh_ref[0], sc_ref[0])
    logits = jnp.dot(u, wr_ref[...], preferred_element_type=F32, precision=HIGHEST) + br_ref[...]
    lane = lax.broadcasted_iota(I32, (tm, LANES), 1).astype(F32)
    m1 = jnp.max(logits, axis=1, keepdims=True)
    e1 = jnp.min(jnp.where(logits == m1, lane, float(LANES)), axis=1, keepdims=True)
    rest = jnp.where(lane == e1, NEG_BIG * 2, logits)
    m2 = jnp.max(rest, axis=1, keepdims=True)
    e2 = jnp.min(jnp.where(rest == m2, lane, float(LANES)), axis=1, keepdims=True)
    ex = jnp.exp(m2 - m1)
    g1 = 1.0 / (1.0 + ex)
    g2 = ex / (1.0 + ex)
    onehot = ((lane == e1) | (lane == e2))
    oh_bf = jnp.where(onehot, 1.0, 0.0).astype(BF16)
    r = lax.broadcasted_iota(I32, (tm, tm), 0)
    c = lax.broadcasted_iota(I32, (tm, tm), 1)
    tri = jnp.where(c < r, 1.0, 0.0).astype(BF16)
    prefix = _dot(tri, oh_bf) + carry_scr[...]
    rank1 = jnp.sum(jnp.where(lane == e1, prefix, 0.0), axis=1, keepdims=True)
    rank2 = jnp.sum(jnp.where(lane == e2, prefix, 0.0), axis=1, keepdims=True)
    carry_scr[...] = carry_scr[...] + jnp.sum(jnp.where(onehot, 1.0, 0.0), axis=0, keepdims=True)
    cnt_ref[...] = carry_scr[...]
    vals = [e1, e2, g1, g2, rank1, rank2]
    route = jnp.zeros((tm, LANES), F32)
    for k, v in enumerate(vals):
        route = jnp.where(lane == float(k), v, route)
    route_ref[...] = route
    up_ref[...] = u


def _router(h2, g, shift, scale, w_router, b_router, seq, tm=512):
    t, d = h2.shape
    tpb = seq // tm
    wr = jnp.pad(w_router.astype(F32), ((0, 0), (0, LANES - N_EXPERTS)))
    br = jnp.pad(b_router.astype(F32), (0, LANES - N_EXPERTS), constant_values=NEG_BIG).reshape(1, LANES)
    mod_spec = pl.BlockSpec((1, 1, d), lambda i: (i // tpb, 0, 0))
    return pl.pallas_call(
        _router_body,
        grid=(t // tm,),
        in_specs=[pl.BlockSpec((tm, d), lambda i: (i, 0)),
                  pl.BlockSpec((1, d), lambda i: (0, 0)),
                  mod_spec, mod_spec,
                  pl.BlockSpec((d, LANES), lambda i: (0, 0)),
                  pl.BlockSpec((1, LANES), lambda i: (0, 0))],
        out_specs=[pl.BlockSpec((tm, LANES), lambda i: (i, 0)),
                   pl.BlockSpec((tm, d), lambda i: (i, 0)),
                   pl.BlockSpec((1, LANES), lambda i: (0, 0))],
        out_shape=[jax.ShapeDtypeStruct((t, LANES), F32),
                   jax.ShapeDtypeStruct((t, d), F32),
                   jax.ShapeDtypeStruct((1, LANES), F32)],
        scratch_shapes=[pltpu.VMEM((1, LANES), F32)],
        compiler_params=_cparams(("arbitrary",)),
        name="moe_router",
    )(h2, g, shift, scale, wr, br)


def _dispatch_body(p1_ref, p2_ref, up_ref, xs_in_ref, xs_ref, sem):
    del xs_in_ref
    tm = up_ref.shape[0]

    def copy(r, dst):
        return pltpu.make_async_copy(up_ref.at[pl.ds(r, 1)], xs_ref.at[pl.ds(dst, 1)], sem)

    def start(r, c):
        copy(r, p1_ref[r]).start(priority=0)
        copy(r, p2_ref[r]).start(priority=1)
        return c

    lax.fori_loop(0, tm, start, 0, unroll=DMA_UNROLL)

    def wait(r, c):
        copy(r, p1_ref[r]).wait()
        copy(r, p2_ref[r]).wait()
        return c

    lax.fori_loop(0, tm, wait, 0, unroll=DMA_UNROLL)


def _dispatch(pos1, pos2, up, nrows, tm=256):
    t, w = up.shape
    xs0 = jnp.zeros((nrows, w), F32)
    smem_spec = pl.BlockSpec((tm,), lambda i: (i,), memory_space=pltpu.SMEM)
    return pl.pallas_call(
        _dispatch_body,
        grid=(t // tm,),
        in_specs=[smem_spec, smem_spec,
                  pl.BlockSpec((tm, w), lambda i: (i, 0)),
                  pl.BlockSpec(memory_space=pl.ANY)],
        out_specs=pl.BlockSpec(memory_space=pl.ANY),
        out_shape=jax.ShapeDtypeStruct((nrows, w), F32),
        scratch_shapes=[pltpu.SemaphoreType.DMA(())],
        input_output_aliases={3: 0},
        compiler_params=_cparams(("arbitrary",)),
        name="moe_dispatch",
    )(pos1, pos2, up, xs0)


def _expert_body(te_ref, nv_ref, xs_ref, wg_ref, wu_ref, wd_ref, y_ref, x_scr, acc_scr):
    j = pl.program_id(0)
    f = pl.program_id(1)
    live = j < nv_ref[0]

    @pl.when(f == 0)
    def _():
        x_scr[...] = xs_ref[...].astype(BF16)
        acc_scr[...] = jnp.zeros_like(acc_scr)

    @pl.when(live)
    def _():
        x = x_scr[...]
        gp = _dot(x, wg_ref[0])
        up = _dot(x, wu_ref[0])
        a = (gp * jax.nn.sigmoid(gp) * up).astype(BF16)
        acc_scr[...] += _dot(a, wd_ref[0])

    @pl.when(f == pl.num_programs(1) - 1)
    def _():
        y_ref[...] = acc_scr[...]


def _experts(tile_expert, n_live, xs, wg, wu, wd, tm, tf=MOE_TF):
    nrows, d = xs.shape
    nf = wg.shape[2] // tf
    nt = nrows // tm

    def f_eff(j, f, nv):
        return jnp.where(j < nv[0], f, nf - 1)

    grid_spec = pltpu.PrefetchScalarGridSpec(
        num_scalar_prefetch=2,
        grid=(nt, nf),
        in_specs=[pl.BlockSpec((tm, d), lambda j, f, te, nv: (j, 0)),
                  pl.BlockSpec((1, d, tf), lambda j, f, te, nv: (te[j], 0, f_eff(j, f, nv))),
                  pl.BlockSpec((1, d, tf), lambda j, f, te, nv: (te[j], 0, f_eff(j, f, nv))),
                  pl.BlockSpec((1, tf, d), lambda j, f, te, nv: (te[j], f_eff(j, f, nv), 0))],
        out_specs=pl.BlockSpec((tm, d), lambda j, f, te, nv: (j, 0)),
        scratch_shapes=[pltpu.VMEM((tm, d), BF16), pltpu.VMEM((tm, d), F32)])
    return pl.pallas_call(
        _expert_body,
        grid_spec=grid_spec,
        out_shape=jax.ShapeDtypeStruct((nrows, d), F32),
        compiler_params=_cparams(("arbitrary", "arbitrary")),
        name="moe_experts",
    )(tile_expert, n_live, xs, wg, wu, wd)


def _combine_body(p1_ref, p2_ref, route_ref, h_ref, gt_ref, gf_ref, y_ref, o_ref, y1_scr, y2_scr, sem):
    tm = h_ref.shape[0]

    def copy(src, dst_scr, r):
        return pltpu.make_async_copy(y_ref.at[pl.ds(src, 1)], dst_scr.at[pl.ds(r, 1)], sem)

    def start(r, c):
        copy(p1_ref[r], y1_scr, r).start(priority=0)
        copy(p2_ref[r], y2_scr, r).start(priority=1)
        return c

    lax.fori_loop(0, tm, start, 0, unroll=DMA_UNROLL)

    def wait(r, c):
        copy(p1_ref[r], y1_scr, r).wait()
        copy(p2_ref[r], y2_scr, r).wait()
        return c

    lax.fori_loop(0, tm, wait, 0, unroll=DMA_UNROLL)
    route = route_ref[...]
    g1, g2 = route[:, 2:3], route[:, 3:4]
    y = g1 * y1_scr[...] + g2 * y2_scr[...]
    hn = h_ref[...] + gt_ref[0] * y
    ms = jnp.mean(hn * hn, axis=-1, keepdims=True)
    o_ref[...] = hn * lax.rsqrt(ms + NORM_EPS) * gf_ref[...]


def _combine(pos1, pos2, route, h2, gate, g_final, y_sorted, seq, tm=256):
    t, d = h2.shape
    tpb = seq // tm
    smem_spec = pl.BlockSpec((tm,), lambda i: (i,), memory_space=pltpu.SMEM)
    return pl.pallas_call(
        _combine_body,
        grid=(t // tm,),
        in_specs=[smem_spec, smem_spec,
                  pl.BlockSpec((tm, LANES), lambda i: (i, 0)),
                  pl.BlockSpec((tm, d), lambda i: (i, 0)),
                  pl.BlockSpec((1, 1, d), lambda i: (i // tpb, 0, 0)),
                  pl.BlockSpec((1, d), lambda i: (0, 0)),
                  pl.BlockSpec(memory_space=pl.ANY)],
        out_specs=pl.BlockSpec((tm, d), lambda i: (i, 0)),
        out_shape=jax.ShapeDtypeStruct((t, d), F32),
        scratch_shapes=[pltpu.VMEM((tm, d), F32), pltpu.VMEM((tm, d), F32),
                        pltpu.SemaphoreType.DMA(())],
        compiler_params=_cparams(("arbitrary",)),
        name="moe_combine",
    )(pos1, pos2, route, h2, gate, g_final, y_sorted)


MOE_TILE = 512


def _moe_layout(route, counts):
    e1 = route[:, 0].astype(I32)
    e2 = route[:, 1].astype(I32)
    r1 = route[:, 4].astype(I32)
    r2 = route[:, 5].astype(I32)
    cnt = counts[0, :N_EXPERTS].astype(I32)
    tiles = (cnt + MOE_TILE - 1) // MOE_TILE
    tile_end = jnp.cumsum(tiles)
    start = (tile_end - tiles) * MOE_TILE
    pos1 = start[e1] + r1
    pos2 = start[e2] + r2
    nt = route.shape[0] * 2 // MOE_TILE + N_EXPERTS
    n_live = tile_end[-1]
    tid = jnp.minimum(jnp.arange(nt, dtype=I32), n_live - 1)
    tile_expert = jnp.sum((tid[:, None] >= tile_end[None, :]).astype(I32), axis=1)
    return pos1, pos2, tile_expert.astype(I32), n_live.reshape(1).astype(I32), nt * MOE_TILE


def kernel(x, c, w_ada, b_ada, g_mix, g_ffn, g_final, e_w_in, e_w_out, s5_a_re, s5_a_im, s5_log_dt,
           s5_b_re, s5_b_im, s5_c_re, s5_c_im, s5_d, s5_w_glu, s5_b_glu, ff_w_gate, ff_w_up,
           ff_w_down, o_w_in, o_w_out, mla_g_kv, mla_w_uk, mla_w_uv, moe_w_router, moe_b_router,
           moe_w_gate, moe_w_up, moe_w_down):
    bn, seq, d = x.shape
    t = bn * seq
    mod = _adaln(c, w_ada, b_ada)

    def mod_vec(layer, k):
        return mod[layer, :, k * d:(k + 1) * d].reshape(bn, 1, d)

    tabs64 = _rope_tables(seq, HEAD_DIM)
    tabs16 = _rope_tables(seq, ROPE_DIM)
    h = x.reshape(t, d)

    *qkv_views, s_in = _proj0(h, g_mix[0].reshape(1, d), mod_vec(0, 0), mod_vec(0, 1),
                              e_w_in[0].astype(BF16), tabs64, seq)
    y_a = _dilated_attention(qkv_views, bn, seq)
    prm = _s5_prepare(s5_a_re[0], s5_a_im[0], s5_log_dt[0], s5_b_re[0], s5_b_im[0],
                      s5_c_re[0], s5_c_im[0], s5_d[0], seq // S5_CHUNK)
    y_b = _s5_mixer(s_in, prm, s5_w_glu[0].astype(BF16), s5_b_glu[0], bn, seq)
    h = _outproj([y_a], [y_b], e_w_out[0].astype(BF16), h, mod_vec(0, 2), seq)
    h = _ffn_dense(h, g_ffn[0].reshape(1, d), mod_vec(0, 3), mod_vec(0, 4), mod_vec(0, 5),
                   ff_w_gate[0].astype(BF16), ff_w_up[0].astype(BF16), ff_w_down[0].astype(BF16), seq)

    w1 = _proj1_layout(o_w_in[0]).astype(BF16)
    qa, kp, qi, ki, wi = _proj1(h, g_mix[1].reshape(1, d), mod_vec(1, 0), mod_vec(1, 1), w1,
                                mla_g_kv[0].reshape(1, C_LATENT).astype(F32), tabs16, tabs64, seq)
    mask = _dsa_select(qi, ki, wi, bn, seq)
    wuk_x, wuv_x = _mla_weights(mla_w_uk[0], mla_w_uv[0])
    o_attn = _dsa_attention(qa, kp, mask, wuk_x, wuv_x, bn, seq)
    h = _outproj([o_attn.reshape(t, C_HEADS * HEAD_DIM)], [], o_w_out[0].astype(BF16), h, mod_vec(1, 2), seq)

    route, up, counts = _router(h, g_ffn[1].reshape(1, d), mod_vec(1, 3), mod_vec(1, 4),
                                moe_w_router[0], moe_b_router[0], seq)
    pos1, pos2, tile_expert, n_live, nrows = _moe_layout(route, counts)
    xs = _dispatch(pos1, pos2, up, nrows)
    y_sorted = _experts(tile_expert, n_live, xs, moe_w_gate[0].astype(BF16), moe_w_up[0].astype(BF16),
                        moe_w_down[0].astype(BF16), MOE_TILE)
    out = _combine(pos1, pos2, route, h, mod_vec(1, 5), g_final.reshape(1, d), y_sorted, seq)
    return out.reshape(bn, seq, d)
```

```python
import functools
import math

import jax
import jax.numpy as jnp
import numpy as np
from jax import lax
from jax.experimental import pallas as pl
from jax.experimental.pallas import tpu as pltpu

F32 = jnp.float32
BF16 = jnp.bfloat16
I32 = jnp.int32
HIGHEST = lax.Precision.HIGHEST

D_MODEL = 1024
HEAD_DIM = 64
ROPE_DIM = 16
ROPE_THETA = 500000.0
NORM_EPS = 1e-6
A_HEADS = 8
A_WIDTH = 512
A_PATTERNS = ((128, 1), (512, 4), (2048, 16))
B_WIDTH = 512
B_GROUP = 16
B_GROUPS = 32
B_STATE = 64
C_HEADS = 16
C_NOPE = 48
C_LATENT = 256
IDX_HEADS = 8
IDX_DIM = 64
TOPK_MAX = 256
FF_DENSE = 2816
N_EXPERTS = 8
FF_EXPERT = 3584

LANES = 128
Q_BLOCK = 128
S5_CHUNK = 16
MOE_TF = 512
FFN_TF = 256
DMA_UNROLL = 8
VMEM_LIMIT = 56 * 1024 * 1024
NEG_BIG = -1e30
INT_MIN = -(2 ** 31)


def _cparams(sem, vmem=VMEM_LIMIT):
    return pltpu.CompilerParams(dimension_semantics=sem, vmem_limit_bytes=vmem)


def _dot(a, b):
    return jnp.dot(a, b, preferred_element_type=F32)


def _dot_nt(a, b):
    return lax.dot_general(a, b, (((1,), (1,)), ((), ())), preferred_element_type=F32)


def _norm_mod(x, g, shift, scale):
    ms = jnp.mean(x * x, axis=-1, keepdims=True)
    y = x * lax.rsqrt(ms + NORM_EPS) * g
    return y * (1.0 + scale) + shift


def _rope128(x, ct, sa, sb):
    return x * ct + pltpu.roll(x, LANES - ROPE_DIM // 2, 1) * sa + pltpu.roll(x, ROPE_DIM // 2, 1) * sb


def _adaln_body(c_ref, w_ref, b_ref, o_ref):
    c = c_ref[...]
    ca = c * jax.nn.sigmoid(c)
    o_ref[0] = jnp.dot(ca, w_ref[0], preferred_element_type=F32, precision=HIGHEST) + b_ref[0]


def _adaln(c, w_ada, b_ada):
    depth, d, d6 = w_ada.shape
    bn = c.shape[0]
    rows = 8
    cp = jnp.zeros((rows, d), F32).at[:bn].set(c)
    tn = 1536
    out = pl.pallas_call(
        _adaln_body,
        grid=(depth, d6 // tn),
        in_specs=[pl.BlockSpec((rows, d), lambda l, j: (0, 0)),
                  pl.BlockSpec((1, d, tn), lambda l, j: (l, 0, j)),
                  pl.BlockSpec((1, 1, tn), lambda l, j: (l, 0, j))],
        out_specs=pl.BlockSpec((1, rows, tn), lambda l, j: (l, 0, j)),
        out_shape=jax.ShapeDtypeStruct((depth, rows, d6), F32),
        compiler_params=_cparams(("arbitrary", "arbitrary")),
        name="adaln",
    )(cp, w_ada, b_ada.reshape(depth, 1, d6))
    return out[:, :bn]


def _rope_tables(seq, period):
    half = ROPE_DIM // 2
    pos = jnp.arange(seq, dtype=F32)
    inv = ROPE_THETA ** (-jnp.arange(0, ROPE_DIM, 2, dtype=F32) / ROPE_DIM)
    ang = pos[:, None] * inv[None, :]
    cos, sin = jnp.cos(ang), jnp.sin(ang)
    lane = np.arange(LANES) % period
    first = lane < half
    second = (lane >= half) & (lane < ROPE_DIM)
    idx = np.where(first, lane, np.where(second, lane - half, 0))
    cos_l, sin_l = cos[:, idx], sin[:, idx]
    ct = jnp.where(first | second, cos_l, 1.0)
    sa = jnp.where(first, -sin_l, 0.0)
    sb = jnp.where(second, sin_l, 0.0)
    return ct, sa, sb


def _proj0_body(x_ref, g_ref, sh_ref, sc_ref, w_ref, ct_ref, sa_ref, sb_ref, *rest):
    qkv_refs, s_ref, acc_scr = rest[:-2], rest[-2], rest[-1]
    tm = x_ref.shape[0]
    qw = 3 * A_WIDTH
    u = _norm_mod(x_ref[...], g_ref[...], sh_ref[0], sc_ref[0]).astype(BF16)
    ct, sa, sb = ct_ref[...], sa_ref[...], sb_ref[...]
    for j in range(3):
        acc = _dot(u, w_ref[:, j * A_WIDTH:(j + 1) * A_WIDTH])
        for c in range(A_WIDTH // LANES):
            a = acc[:, c * LANES:(c + 1) * LANES]
            if j < 2:
                a = _rope128(a, ct, sa, sb)
            if j == 0:
                a = a * (HEAD_DIM ** -0.5)
            acc_scr[j * (A_WIDTH // LANES) + c] = a
    acc = _dot(u, w_ref[:, qw:])
    nq = qw // LANES
    for c in range(B_WIDTH // LANES):
        acc_scr[nq + c] = acc[:, c * LANES:(c + 1) * LANES]
    for (_, dil), ref in zip(A_PATTERNS, qkv_refs):
        for r in range(dil):
            for c in range(nq):
                ref[:, r * qw + c * LANES:r * qw + (c + 1) * LANES] = (
                    acc_scr[c, pl.ds(r, tm // dil, stride=dil), :].astype(BF16))
    for r in range(S5_CHUNK):
        for c in range(B_WIDTH // LANES):
            s_ref[:, r * B_WIDTH + c * LANES:r * B_WIDTH + (c + 1) * LANES] = (
                acc_scr[nq + c, pl.ds(r, tm // S5_CHUNK, stride=S5_CHUNK), :])


def _proj0(x2, g, shift, scale, w_bf, tabs, seq, tm=512):
    t, d = x2.shape
    n = w_bf.shape[1]
    tpb = seq // tm
    ct, sa, sb = tabs
    tab_spec = pl.BlockSpec((tm, LANES), lambda i: (i % tpb, 0))
    mod_spec = pl.BlockSpec((1, 1, d), lambda i: (i // tpb, 0, 0))
    qw = 3 * A_WIDTH
    dils = [dil for _, dil in A_PATTERNS]
    out_specs = [pl.BlockSpec((tm // dil, dil * qw), lambda i: (i, 0)) for dil in dils]
    out_shape = [jax.ShapeDtypeStruct((t // dil, dil * qw), BF16) for dil in dils]
    out_specs.append(pl.BlockSpec((tm // S5_CHUNK, S5_CHUNK * B_WIDTH), lambda i: (i, 0)))
    out_shape.append(jax.ShapeDtypeStruct((t // S5_CHUNK, S5_CHUNK * B_WIDTH), F32))
    return pl.pallas_call(
        _proj0_body,
        grid=(t // tm,),
        in_specs=[pl.BlockSpec((tm, d), lambda i: (i, 0)),
                  pl.BlockSpec((1, d), lambda i: (0, 0)),
                  mod_spec, mod_spec,
                  pl.BlockSpec((d, n), lambda i: (0, 0)),
                  tab_spec, tab_spec, tab_spec],
        out_specs=out_specs,
        out_shape=out_shape,
        scratch_shapes=[pltpu.VMEM((n // LANES, tm, LANES), F32)],
        compiler_params=_cparams(("parallel",)),
        name="proj0",
    )(x2, g, shift, scale, w_bf, ct, sa, sb)


DIL_QB_MAX = 4


def _dil_body(q_ref, kp_ref, kc_ref, vp_ref, vc_ref, lse_ref, o_ref):
    i = pl.program_id(2)
    nqb = q_ref.shape[1] // Q_BLOCK
    kall = jnp.concatenate([kp_ref[0], kc_ref[0]], axis=0)
    vall = jnp.concatenate([vp_ref[0], vc_ref[0]], axis=0)
    row = lax.broadcasted_iota(I32, (Q_BLOCK, 2 * Q_BLOCK), 0)
    col = lax.broadcasted_iota(I32, (Q_BLOCK, 2 * Q_BLOCK), 1)
    rel = row + Q_BLOCK - col
    band = (rel >= 0) & (rel <= Q_BLOCK)
    lane = lax.broadcasted_iota(I32, (1, LANES), 1)
    npair = A_WIDTH // LANES
    hpp = LANES // HEAD_DIM
    scores = []
    for j in range(nqb):
        valid = band if j > 0 else band & ((col >= Q_BLOCK) | (i > 0))
        qj = q_ref[0, j * Q_BLOCK:(j + 1) * Q_BLOCK, :]
        kj = kall[j * Q_BLOCK:(j + 2) * Q_BLOCK]
        for p in range(npair):
            sl = slice(p * LANES, (p + 1) * LANES)
            for hh in range(hpp):
                hm = (lane // HEAD_DIM) == hh
                qh = jnp.where(hm, qj[:, sl], jnp.zeros_like(qj[:, sl]))
                scores.append(jnp.where(valid, _dot_nt(qh, kj[:, sl]), -jnp.inf))
    s = jnp.concatenate(scores, axis=0)
    m = jnp.max(s, axis=1, keepdims=True)
    pr = jnp.exp(s - m)
    l = jnp.sum(pr, axis=1, keepdims=True)
    prb = pr.astype(BF16)
    for j in range(nqb):
        vj = vall[j * Q_BLOCK:(j + 2) * Q_BLOCK]
        qs = slice(j * Q_BLOCK, (j + 1) * Q_BLOCK)
        for p in range(npair):
            sl = slice(p * LANES, (p + 1) * LANES)
            m_pair = l_pair = o_pair = None
            for hh in range(hpp):
                h = (j * npair + p) * hpp + hh
                rs = slice(h * Q_BLOCK, (h + 1) * Q_BLOCK)
                hm = (lane // HEAD_DIM) == hh
                o = _dot(prb[rs], vj[:, sl])
                if hh == 0:
                    m_pair = jnp.broadcast_to(m[rs], (Q_BLOCK, LANES))
                    l_pair = jnp.broadcast_to(l[rs], (Q_BLOCK, LANES))
                    o_pair = o
                else:
                    m_pair = jnp.where(hm, m[rs], m_pair)
                    l_pair = jnp.where(hm, l[rs], l_pair)
                    o_pair = jnp.where(hm, o, o_pair)
            lse_ref[0, qs, sl] = m_pair + jnp.log(l_pair)
            o_ref[0, qs, sl] = (o_pair / l_pair).astype(o_ref.dtype)


def _dilated_branch(qkv_d, dil, bn, seq):
    sd = seq // dil
    nqb = min(DIL_QB_MAX, sd // Q_BLOCK)
    nb = sd // (nqb * Q_BLOCK)
    cur = (1, nqb * Q_BLOCK, A_WIDTH)
    prev = (1, Q_BLOCK, A_WIDTH)
    q_spec = pl.BlockSpec(cur, lambda b, r, i: (b, i, r * 3))
    kp_spec = pl.BlockSpec(prev, lambda b, r, i: (b, jnp.maximum(nqb * i - 1, 0), r * 3 + 1))
    kc_spec = pl.BlockSpec(cur, lambda b, r, i: (b, i, r * 3 + 1))
    vp_spec = pl.BlockSpec(prev, lambda b, r, i: (b, jnp.maximum(nqb * i - 1, 0), r * 3 + 2))
    vc_spec = pl.BlockSpec(cur, lambda b, r, i: (b, i, r * 3 + 2))
    st_spec = pl.BlockSpec(cur, lambda b, r, i: (b, i, r))
    lse, out = pl.pallas_call(
        _dil_body,
        grid=(bn, dil, nb),
        in_specs=[q_spec, kp_spec, kc_spec, vp_spec, vc_spec],
        out_specs=[st_spec] * 2,
        out_shape=[jax.ShapeDtypeStruct((bn, sd, dil * A_WIDTH), F32),
                   jax.ShapeDtypeStruct((bn, sd, dil * A_WIDTH), BF16)],
        compiler_params=_cparams(("parallel", "parallel", "arbitrary")),
        name=f"dilated_d{dil}",
    )(*([qkv_d.reshape(bn, sd, dil * 3 * A_WIDTH)] * 5))
    return [lse.reshape(bn * sd, dil * A_WIDTH), out.reshape(bn * sd, dil * A_WIDTH)]


def _dil_merge_body(*refs):
    nbr = len(A_PATTERNS)
    stats = refs[:2 * nbr]
    y_ref = refs[2 * nbr]
    scr = refs[2 * nbr + 1:]
    tm = y_ref.shape[0]
    nc = A_WIDTH // LANES
    lses, outs = [], []
    si = 0
    for gi, (_, dil) in enumerate(A_PATTERNS):
        pair = []
        for ref in stats[2 * gi:2 * gi + 2]:
            if dil == 1:
                pair.append(ref[...].astype(F32))
            else:
                for r in range(dil):
                    for c in range(nc):
                        scr[si][c, pl.ds(r, tm // dil, stride=dil), :] = (
                            ref[:, r * A_WIDTH + c * LANES:r * A_WIDTH + (c + 1) * LANES].astype(F32))
                pair.append(jnp.concatenate([scr[si][c] for c in range(nc)], axis=1))
                si += 1
        lses.append(pair[0])
        outs.append(pair[1])
    top = functools.reduce(jnp.maximum, lses)
    den = jnp.zeros_like(top)
    num = jnp.zeros_like(top)
    for lse, o in zip(lses, outs):
        w = jnp.exp(lse - top)
        den = den + w
        num = num + w * o
    y_ref[...] = (num / den).astype(y_ref.dtype)


def _dilated_attention(qkv_views, bn, seq, tm=512):
    t = bn * seq
    stats, in_specs = [], []
    nscr = 0
    for (_, dil), qkv_d in zip(A_PATTERNS, qkv_views):
        stats += _dilated_branch(qkv_d, dil, bn, seq)
        in_specs += [pl.BlockSpec((tm // dil, dil * A_WIDTH), lambda i: (i, 0))] * 2
        nscr += 2 if dil > 1 else 0
    return pl.pallas_call(
        _dil_merge_body,
        grid=(t // tm,),
        in_specs=in_specs,
        out_specs=pl.BlockSpec((tm, A_WIDTH), lambda i: (i, 0)),
        out_shape=jax.ShapeDtypeStruct((t, A_WIDTH), BF16),
        scratch_shapes=[pltpu.VMEM((A_WIDTH // LANES, tm, LANES), F32)] * nscr,
        compiler_params=_cparams(("parallel",)),
        name="dilated_merge",
    )(*stats)


def _s5_prepare(a_re, a_im, log_dt, b_re, b_im, c_re, c_im, d_skip, nchunk):
    dt = jnp.exp(log_dt.astype(F32))[:, None]
    lr, li = a_re.astype(F32), a_im.astype(F32)
    mag = jnp.exp(lr * dt)
    ar = mag * jnp.cos(li * dt)
    ai = mag * jnp.sin(li * dt)
    den = lr * lr + li * li
    nr = ar - 1.0
    fr = (nr * lr + ai * li) / den
    fi = (ai * lr - nr * li) / den
    br, bi = b_re.astype(F32), b_im.astype(F32)
    bbr = fr[..., None] * br - fi[..., None] * bi
    bbi = fr[..., None] * bi + fi[..., None] * br
    nblk = B_WIDTH // LANES
    gpb = B_GROUPS // nblk
    eye = jnp.eye(gpb, dtype=F32)

    def bdiag_in(m):
        m = jnp.swapaxes(m.reshape(nblk, gpb, B_STATE, B_GROUP), 2, 3)
        m = m[:, :, :, None, :] * eye[None, :, None, :, None]
        return m.reshape(nblk, gpb * B_GROUP, gpb * B_STATE)

    def bdiag_out(m):
        m = jnp.swapaxes(m.reshape(nblk, gpb, B_GROUP, B_STATE), 2, 3)
        m = m[:, :, :, None, :] * eye[None, :, None, :, None]
        return m.reshape(nblk, gpb * B_STATE, gpb * B_GROUP)

    pr, pi = ar, ai
    for _ in range(int(math.log2(S5_CHUNK))):
        pr, pi = pr * pr - pi * pi, 2.0 * pr * pi
    pows_r, pows_i = [], []
    for _ in range(max(1, int(math.ceil(math.log2(nchunk))))):
        pows_r.append(pr.reshape(1, -1))
        pows_i.append(pi.reshape(1, -1))
        pr, pi = pr * pr - pi * pi, 2.0 * pr * pi
    return dict(
        ar=ar.reshape(1, -1), ai=ai.reshape(1, -1),
        b_re=bdiag_in(bbr).astype(BF16), b_im=bdiag_in(bbi).astype(BF16),
        c_re=bdiag_out(c_re.astype(F32)).astype(BF16), c_im=bdiag_out(-c_im.astype(F32)).astype(BF16),
        d=d_skip.astype(F32).reshape(1, B_WIDTH),
        pows_r=jnp.concatenate(pows_r, axis=0), pows_i=jnp.concatenate(pows_i, axis=0))


def _s5_local_body(u_ref, bre_ref, bim_ref, ar_ref, ai_ref, sre_ref, sim_ref):
    tr = u_ref.shape[0]
    sw = bre_ref.shape[2]
    for blk in range(B_WIDTH // LANES):
        arb = ar_ref[:, blk * sw:(blk + 1) * sw]
        aib = ai_ref[:, blk * sw:(blk + 1) * sw]
        sr = jnp.zeros((tr, sw), F32)
        si = jnp.zeros((tr, sw), F32)
        for t in range(S5_CHUNK):
            ub = u_ref[:, t * B_WIDTH + blk * LANES:t * B_WIDTH + (blk + 1) * LANES].astype(BF16)
            sr, si = (arb * sr - aib * si + _dot(ub, bre_ref[blk]),
                      arb * si + aib * sr + _dot(ub, bim_ref[blk]))
        sre_ref[:, blk * sw:(blk + 1) * sw] = sr
        sim_ref[:, blk * sw:(blk + 1) * sw] = si


def _s5_scan_body(nsteps, sre_ref, sim_ref, pr_ref, pi_ref, xre_ref, xim_ref):
    nc = sre_ref.shape[0]
    sw = 512
    row = lax.broadcasted_iota(I32, (nc, 1), 0)
    for cb in range(sre_ref.shape[1] // sw):
        sl = slice(cb * sw, (cb + 1) * sw)
        xr, xi = sre_ref[:, sl], sim_ref[:, sl]
        for s in range(nsteps):
            sh = 1 << s
            ok = row >= sh
            rr = jnp.where(ok, pltpu.roll(xr, sh, 0), 0.0)
            ri = jnp.where(ok, pltpu.roll(xi, sh, 0), 0.0)
            pr, pi = pr_ref[s:s + 1, sl], pi_ref[s:s + 1, sl]
            xr, xi = xr + pr * rr - pi * ri, xi + pr * ri + pi * rr
        ok = row >= 1
        xre_ref[:, sl] = jnp.where(ok, pltpu.roll(xr, 1, 0), 0.0)
        xim_ref[:, sl] = jnp.where(ok, pltpu.roll(xi, 1, 0), 0.0)


def _s5_out_body(u_ref, xre_ref, xim_ref, bre_ref, bim_ref, cre_ref, cim_ref, ar_ref, ai_ref,
                 d_ref, wglu_ref, bglu_ref, out_ref, y_scr):
    sw = bre_ref.shape[2]
    for blk in range(B_WIDTH // LANES):
        arb = ar_ref[:, blk * sw:(blk + 1) * sw]
        aib = ai_ref[:, blk * sw:(blk + 1) * sw]
        xr = xre_ref[:, blk * sw:(blk + 1) * sw]
        xi = xim_ref[:, blk * sw:(blk + 1) * sw]
        db = d_ref[:, blk * LANES:(blk + 1) * LANES]
        for t in range(S5_CHUNK):
            cs = slice(t * B_WIDTH + blk * LANES, t * B_WIDTH + (blk + 1) * LANES)
            uf = u_ref[:, cs]
            ub = uf.astype(BF16)
            xr, xi = (arb * xr - aib * xi + _dot(ub, bre_ref[blk]),
                      arb * xi + aib * xr + _dot(ub, bim_ref[blk]))
            y = _dot(xr.astype(BF16), cre_ref[blk]) + _dot(xi.astype(BF16), cim_ref[blk]) + db * uf
            y_scr[:, cs] = jax.nn.gelu(y, approximate=True)
    for t in range(S5_CHUNK):
        cs = slice(t * B_WIDTH, (t + 1) * B_WIDTH)
        y = y_scr[:, cs]
        z = _dot(y.astype(BF16), wglu_ref[...]) + bglu_ref[...]
        out_ref[:, cs] = (y * jax.nn.sigmoid(z)).astype(out_ref.dtype)


def _s5_mixer(s_in, prm, w_glu_bf, b_glu, bn, seq, tr=128, tr_out=256):
    t = bn * seq
    nrow = t // S5_CHUNK
    ncb = seq // S5_CHUNK
    width = S5_CHUNK * B_WIDTH
    sdim = B_GROUPS * B_STATE
    uc = s_in.reshape(nrow, width)
    nblk = B_WIDTH // LANES
    const3 = lambda i: (0, 0, 0)
    const2 = lambda i: (0, 0)
    b_spec = pl.BlockSpec((nblk, LANES, sdim // nblk), const3)
    c_spec = pl.BlockSpec((nblk, sdim // nblk, LANES), const3)
    a_spec = pl.BlockSpec((1, sdim), const2)
    s_re, s_im = pl.pallas_call(
        _s5_local_body,
        grid=(nrow // tr,),
        in_specs=[pl.BlockSpec((tr, width), lambda i: (i, 0)), b_spec, b_spec, a_spec, a_spec],
        out_specs=[pl.BlockSpec((tr, sdim), lambda i: (i, 0))] * 2,
        out_shape=[jax.ShapeDtypeStruct((nrow, sdim), F32)] * 2,
        compiler_params=_cparams(("parallel",)),
        name="s5_local",
    )(uc, prm['b_re'], prm['b_im'], prm['ar'], prm['ai'])
    nsteps = prm['pows_r'].shape[0]
    x_re, x_im = pl.pallas_call(
        functools.partial(_s5_scan_body, nsteps),
        grid=(bn,),
        in_specs=[pl.BlockSpec((ncb, sdim), lambda b: (b, 0))] * 2
                 + [pl.BlockSpec((nsteps, sdim), lambda b: (0, 0))] * 2,
        out_specs=[pl.BlockSpec((ncb, sdim), lambda b: (b, 0))] * 2,
        out_shape=[jax.ShapeDtypeStruct((nrow, sdim), F32)] * 2,
        compiler_params=_cparams(("parallel",)),
        name="s5_scan",
    )(s_re, s_im, prm['pows_r'], prm['pows_i'])
    y = pl.pallas_call(
        _s5_out_body,
        grid=(nrow // tr_out,),
        in_specs=[pl.BlockSpec((tr_out, width), lambda i: (i, 0)),
                  pl.BlockSpec((tr_out, sdim), lambda i: (i, 0)),
                  pl.BlockSpec((tr_out, sdim), lambda i: (i, 0)),
                  b_spec, b_spec, c_spec, c_spec, a_spec, a_spec,
                  pl.BlockSpec((1, B_WIDTH), const2),
                  pl.BlockSpec((B_WIDTH, B_WIDTH), const2),
                  pl.BlockSpec((1, B_WIDTH), const2)],
        out_specs=pl.BlockSpec((tr_out, width), lambda i: (i, 0)),
        out_shape=jax.ShapeDtypeStruct((nrow, width), BF16),
        scratch_shapes=[pltpu.VMEM((tr_out, width), F32)],
        compiler_params=_cparams(("parallel",)),
        name="s5_out",
    )(uc, x_re, x_im, prm['b_re'], prm['b_im'], prm['c_re'], prm['c_im'], prm['ar'], prm['ai'],
      prm['d'], w_glu_bf, b_glu.reshape(1, B_WIDTH).astype(F32))
    return y


def _outproj_body(nparts, nflat, *refs):
    parts = refs[:nparts]
    flats = refs[nparts:nparts + nflat]
    w_ref, h_ref, gt_ref, o_ref = refs[nparts + nflat:nparts + nflat + 4]
    scr = refs[nparts + nflat + 4:]
    tm = h_ref.shape[0]
    acc = None
    off = 0
    for p in parts:
        k = p.shape[1]
        d = _dot(p[...].astype(BF16), w_ref[off:off + k, :])
        acc = d if acc is None else acc + d
        off += k
    for p, s in zip(flats, scr):
        nc = s.shape[0]
        k = nc * LANES
        for r in range(S5_CHUNK):
            for c in range(nc):
                s[c, pl.ds(r, tm // S5_CHUNK, stride=S5_CHUNK), :] = (
                    p[:, r * k + c * LANES:r * k + (c + 1) * LANES].astype(F32))
        rows = jnp.concatenate([s[c] for c in range(nc)], axis=1)
        d = _dot(rows.astype(BF16), w_ref[off:off + k, :])
        acc = d if acc is None else acc + d
        off += k
    o_ref[...] = h_ref[...] + gt_ref[0] * acc


def _outproj(parts, flat_parts, w_bf, h2, gate, seq, tm=512):
    t, d = h2.shape
    tpb = seq // tm
    in_specs = [pl.BlockSpec((tm, p.shape[1]), lambda i: (i, 0)) for p in parts]
    in_specs += [pl.BlockSpec((tm // S5_CHUNK, p.shape[1]), lambda i: (i, 0)) for p in flat_parts]
    in_specs += [pl.BlockSpec(w_bf.shape, lambda i: (0, 0)),
                 pl.BlockSpec((tm, d), lambda i: (i, 0)),
                 pl.BlockSpec((1, 1, d), lambda i: (i // tpb, 0, 0))]
    return pl.pallas_call(
        functools.partial(_outproj_body, len(parts), len(flat_parts)),
        grid=(t // tm,),
        in_specs=in_specs,
        out_specs=pl.BlockSpec((tm, d), lambda i: (i, 0)),
        out_shape=jax.ShapeDtypeStruct((t, d), F32),
        scratch_shapes=[pltpu.VMEM((p.shape[1] // S5_CHUNK // LANES, tm, LANES), F32) for p in flat_parts],
        compiler_params=_cparams(("parallel",)),
        name="outproj",
    )(*parts, *flat_parts, w_bf, h2, gate)


def _ffn_body(h_ref, g_ref, sh_ref, sc_ref, gt_ref, wg_ref, wu_ref, wd_ref, o_ref, u_scr, acc_scr):
    f = pl.program_id(1)

    @pl.when(f == 0)
    def _():
        u_scr[...] = _norm_mod(h_ref[...], g_ref[...], sh_ref[0], sc_ref[0]).astype(BF16)
        acc_scr[...] = jnp.zeros_like(acc_scr)

    u = u_scr[...]
    gp = _dot(u, wg_ref[...])
    up = _dot(u, wu_ref[...])
    a = (gp * jax.nn.sigmoid(gp) * up).astype(BF16)
    acc_scr[...] += _dot(a, wd_ref[...])

    @pl.when(f == pl.num_programs(1) - 1)
    def _():
        o_ref[...] = h_ref[...] + gt_ref[0] * acc_scr[...]


def _ffn_dense(h2, g, shift, scale, gate, wg, wu, wd, seq, tm=1024, tf=FFN_TF):
    t, d = h2.shape
    ff = wg.shape[1]
    tpb = seq // tm
    mod_spec = pl.BlockSpec((1, 1, d), lambda i, f: (i // tpb, 0, 0))
    return pl.pallas_call(
        _ffn_body,
        grid=(t // tm, ff // tf),
        in_specs=[pl.BlockSpec((tm, d), lambda i, f: (i, 0)),
                  pl.BlockSpec((1, d), lambda i, f: (0, 0)),
                  mod_spec, mod_spec, mod_spec,
                  pl.BlockSpec((d, tf), lambda i, f: (0, f)),
                  pl.BlockSpec((d, tf), lambda i, f: (0, f)),
                  pl.BlockSpec((tf, d), lambda i, f: (f, 0))],
        out_specs=pl.BlockSpec((tm, d), lambda i, f: (i, 0)),
        out_shape=jax.ShapeDtypeStruct((t, d), F32),
        scratch_shapes=[pltpu.VMEM((tm, d), BF16), pltpu.VMEM((tm, d), F32)],
        compiler_params=_cparams(("parallel", "arbitrary")),
        name="ffn_dense",
    )(h2, g, shift, scale, gate, wg, wu, wd)


P1_QR = 0
P1_QN = 256
P1_CKV = 1280
P1_KR = 1536
P1_QI = 1792
P1_KI = 2304
P1_WI = 2432
P1_COLS = 2560


def _proj1_layout(w_in):
    d = w_in.shape[0]
    c0 = 0
    w_qr = w_in[:, c0:c0 + C_HEADS * ROPE_DIM]; c0 += C_HEADS * ROPE_DIM
    w_qn = w_in[:, c0:c0 + C_HEADS * C_NOPE]; c0 += C_HEADS * C_NOPE
    w_ckv = w_in[:, c0:c0 + C_LATENT]; c0 += C_LATENT
    w_kr = w_in[:, c0:c0 + ROPE_DIM]; c0 += ROPE_DIM
    w_qi = w_in[:, c0:c0 + IDX_HEADS * IDX_DIM]; c0 += IDX_HEADS * IDX_DIM
    w_ki = w_in[:, c0:c0 + IDX_DIM]; c0 += IDX_DIM
    w_wi = w_in[:, c0:c0 + IDX_HEADS]
    w_qn = jnp.pad(w_qn.reshape(d, C_HEADS, C_NOPE), ((0, 0), (0, 0), (0, HEAD_DIM - C_NOPE)))
    w_qn = w_qn.reshape(d, C_HEADS * HEAD_DIM)
    w_kr = jnp.tile(w_kr, (1, C_HEADS))
    w_ki = jnp.tile(w_ki, (1, 2))
    w_wi = jnp.pad(w_wi, ((0, 0), (0, LANES - IDX_HEADS)))
    return jnp.concatenate([w_qr, w_qn, w_ckv, w_kr, w_qi, w_ki, w_wi], axis=1)


def _proj1_body(x_ref, g_ref, sh_ref, sc_ref, w_ref, gkv_ref,
                ct16_ref, sa16_ref, sb16_ref, ct64_ref, sa64_ref, sb64_ref,
                qa_ref, kp_ref, qi_ref, ki_ref, wi_ref):
    u = _norm_mod(x_ref[...], g_ref[...], sh_ref[0], sc_ref[0]).astype(BF16)
    t16 = (ct16_ref[...], sa16_ref[...], sb16_ref[...])
    t64 = (ct64_ref[...], sa64_ref[...], sb64_ref[...])
    qscale = HEAD_DIM ** -0.5 * math.log2(math.e)

    def cols(lo, hi):
        return _dot(u, w_ref[:, lo:hi])

    a = cols(P1_QR, P1_QN)
    for c in range(2):
        blk = _rope128(a[:, c * LANES:(c + 1) * LANES], *t16) * qscale
        qa_ref[:, c * LANES:(c + 1) * LANES] = blk.astype(BF16)
    qa_ref[:, P1_QN:P1_CKV] = (cols(P1_QN, P1_CKV) * qscale).astype(BF16)
    ckv = cols(P1_CKV, P1_KR)
    ms = jnp.mean(ckv * ckv, axis=-1, keepdims=True)
    kp_ref[:, 0:C_LATENT] = (ckv * lax.rsqrt(ms + NORM_EPS) * gkv_ref[...]).astype(BF16)
    a = cols(P1_KR, P1_QI)
    for c in range(2):
        blk = _rope128(a[:, c * LANES:(c + 1) * LANES], *t16)
        kp_ref[:, C_LATENT + c * LANES:C_LATENT + (c + 1) * LANES] = blk.astype(BF16)
    a = cols(P1_QI, P1_KI)
    for c in range(4):
        qi_ref[:, c * LANES:(c + 1) * LANES] = _rope128(a[:, c * LANES:(c + 1) * LANES], *t64).astype(BF16)
    ki_ref[...] = _rope128(cols(P1_KI, P1_WI), *t64).astype(BF16)
    wi_ref[...] = cols(P1_WI, P1_COLS) * (IDX_HEADS ** -0.5) * (IDX_DIM ** -0.5)


def _proj1(x2, g, shift, scale, w_bf, gkv, t16, t64, seq, tm=512):
    t, d = x2.shape
    tpb = seq // tm
    tab_spec = pl.BlockSpec((tm, LANES), lambda i: (i % tpb, 0))
    mod_spec = pl.BlockSpec((1, 1, d), lambda i: (i // tpb, 0, 0))
    widths = (P1_CKV, 2 * C_LATENT, IDX_HEADS * IDX_DIM, LANES, LANES)
    dtypes = (BF16, BF16, BF16, BF16, F32)
    return pl.pallas_call(
        _proj1_body,
        grid=(t // tm,),
        in_specs=[pl.BlockSpec((tm, d), lambda i: (i, 0)),
                  pl.BlockSpec((1, d), lambda i: (0, 0)),
                  mod_spec, mod_spec,
                  pl.BlockSpec((d, P1_COLS), lambda i: (0, 0)),
                  pl.BlockSpec((1, C_LATENT), lambda i: (0, 0))] + [tab_spec] * 6,
        out_specs=[pl.BlockSpec((tm, w), lambda i: (i, 0)) for w in widths],
        out_shape=[jax.ShapeDtypeStruct((t, w), dt) for w, dt in zip(widths, dtypes)],
        compiler_params=_cparams(("parallel",)),
        name="proj1",
    )(x2, g, shift, scale, w_bf, gkv, *t16, *t64)


IDX_KT = 512


def _idx_body(topk, qi_ref, ki_ref, wi_ref, mask_ref, sc_scr):
    i = pl.program_id(1)
    seq = ki_ref.shape[1]
    nkt = seq // IDX_KT
    nlive = (i * Q_BLOCK) // IDX_KT + 1
    q = qi_ref[0]
    wt = wi_ref[0].T
    lane = lax.broadcasted_iota(I32, (1, LANES), 1)
    qpos = i * Q_BLOCK + lane
    krow = lax.broadcasted_iota(I32, (IDX_KT, 1), 0)
    qpair = [jnp.concatenate([q[:, (2 * g) * LANES:(2 * g + 1) * LANES],
                              q[:, (2 * g + 1) * LANES:(2 * g + 2) * LANES]], axis=0) for g in range(2)]

    def score_tile(kt, c):
        kk = ki_ref[0, pl.ds(pl.multiple_of(kt * IDX_KT, IDX_KT), IDX_KT), :]
        zero = jnp.zeros_like(kk)
        kpart = [jnp.where(lane < IDX_DIM, kk, zero), jnp.where(lane >= IDX_DIM, kk, zero)]
        acc = jnp.zeros((IDX_KT, Q_BLOCK), F32)
        for g in range(2):
            for part in range(2):
                res = jnp.maximum(_dot_nt(kpart[part], qpair[g]), 0.0)
                for j in range(2):
                    h = 2 * (2 * g + j) + part
                    acc = acc + wt[h:h + 1, :] * res[:, j * Q_BLOCK:(j + 1) * Q_BLOCK]
        acc = acc + 0.0
        sc_scr[kt] = jnp.where(kt * IDX_KT + krow <= qpos, acc, -jnp.inf)
        return c

    lax.fori_loop(0, nlive, score_tile, 0)
    kq = jnp.minimum(qpos + 1, topk).astype(F32)

    def count(pred_fn):
        def body(kt, acc):
            ones = jnp.where(pred_fn(sc_scr[kt], kt), 1.0, 0.0)
            part = jnp.sum(ones.reshape(8, IDX_KT // 64, 8, Q_BLOCK), axis=1)
            return acc + jnp.sum(part, axis=0)
        part = lax.fori_loop(0, nlive, body, jnp.zeros((8, Q_BLOCK), F32))
        return jnp.sum(part, axis=0, keepdims=True)

    def key_to_f32(key):
        bits = jnp.where(key < 0, key ^ jnp.int32(0x7FFFFFFF), key)
        return lax.bitcast_convert_type(bits, F32)

    def bit_step(it, ans):
        cand = ans | lax.shift_left(jnp.int32(1), 31 - it)
        thr = key_to_f32(cand ^ jnp.int32(INT_MIN))
        cnt = count(lambda sc, kt: sc >= thr)
        return jnp.where(cnt >= kq, cand, ans)

    ans = lax.fori_loop(0, 32, bit_step, jnp.zeros((1, Q_BLOCK), I32))
    thr = key_to_f32(ans ^ jnp.int32(INT_MIN))
    n_ge = count(lambda sc, kt: sc >= thr)
    nbits = int(math.log2(seq))

    def tie_cut(_):
        need = kq - count(lambda sc, kt: sc > thr)

        def tie_step(it, ans2):
            cand = ans2 | lax.shift_left(jnp.int32(1), nbits - 1 - it)
            cnt = count(lambda sc, kt: (sc == thr) & (kt * IDX_KT + krow < cand))
            return jnp.where(cnt < need, cand, ans2)

        return lax.fori_loop(0, nbits, tie_step, jnp.zeros((1, Q_BLOCK), I32))

    excess = jnp.max(n_ge - kq) > 0.0
    jcut = lax.cond(excess, tie_cut, lambda _: jnp.full((1, Q_BLOCK), seq, I32), 0)

    def emit(kt, c):
        sc = sc_scr[kt]
        sel = (sc > thr) | ((sc == thr) & (kt * IDX_KT + krow <= jcut))
        mask_ref[0, 0, kt] = jnp.where(sel, 1.0, 0.0).T.astype(BF16)
        return c

    lax.fori_loop(0, nlive, emit, 0)

    def emit_dead(kt, c):
        mask_ref[0, 0, kt] = jnp.zeros((Q_BLOCK, IDX_KT), BF16)
        return c

    lax.fori_loop(nlive, nkt, emit_dead, 0)


def _dsa_select(qi, ki, wi, bn, seq):
    topk = min(TOPK_MAX, seq // 4)
    nqb = seq // Q_BLOCK
    nkt = seq // IDX_KT
    return pl.pallas_call(
        functools.partial(_idx_body, topk),
        grid=(bn, nqb),
        in_specs=[pl.BlockSpec((1, Q_BLOCK, IDX_HEADS * IDX_DIM), lambda b, i: (b, i, 0)),
                  pl.BlockSpec((1, seq, LANES), lambda b, i: (b, 0, 0)),
                  pl.BlockSpec((1, Q_BLOCK, LANES), lambda b, i: (b, i, 0))],
        out_specs=pl.BlockSpec((1, 1, nkt, Q_BLOCK, IDX_KT), lambda b, i: (b, i, 0, 0, 0)),
        out_shape=jax.ShapeDtypeStruct((bn, nqb, nkt, Q_BLOCK, IDX_KT), BF16),
        scratch_shapes=[pltpu.VMEM((nkt, IDX_KT, Q_BLOCK), F32)],
        compiler_params=_cparams(("parallel", "arbitrary")),
        name="dsa_select",
    )(qi.reshape(bn, seq, -1), ki.reshape(bn, seq, -1), wi.reshape(bn, seq, -1))


ATT_RB = 512


def _dsa_attn_body(qa_ref, qan_ref, kp_ref, mask_ref, wuk_ref, wuv_ref, o_ref, qp_scr, m_scr, l_scr,
                   acc_scr, s_scr):
    i = pl.program_id(1)
    rows = C_HEADS * Q_BLOCK
    nblk = rows // ATT_RB
    cur = lax.rem(i, 2)
    nxt = 1 - cur
    lane = lax.broadcasted_iota(I32, (1, C_HEADS * ROPE_DIM), 1)

    def prep(src_ref, slot):
        qr = src_ref[0, :, 0:P1_QN]
        for h in range(C_HEADS):
            p = h // 2
            qn = src_ref[0, :, P1_QN + p * LANES:P1_QN + (p + 1) * LANES]
            qlat = _dot(qn, wuk_ref[h])
            qp_scr[slot, h * Q_BLOCK:(h + 1) * Q_BLOCK, 0:C_LATENT] = qlat.astype(BF16)
            qp_scr[slot, h * Q_BLOCK:(h + 1) * Q_BLOCK, C_LATENT:2 * C_LATENT] = jnp.where(
                (lane // ROPE_DIM) == h, qr, jnp.zeros_like(qr))

    def ktile(kt):
        return kp_ref[0, pl.ds(pl.multiple_of(kt * IDX_KT, IDX_KT), IDX_KT), :]

    def tile_bias(kt):
        bias = (mask_ref[0, 0, kt].astype(F32) - 1.0) * (-NEG_BIG)
        return jnp.concatenate([bias] * (ATT_RB // Q_BLOCK), axis=0)

    def softmax_pv(b, s, bias, vv):
        rs = slice(b * ATT_RB, (b + 1) * ATT_RB)
        s = s + bias
        m_prev = m_scr[rs, :]
        m_next = jnp.maximum(m_prev, jnp.max(s, axis=1, keepdims=True))
        alpha = jnp.exp2(m_prev - m_next)
        pr = jnp.exp2(s - jnp.concatenate([m_next] * (IDX_KT // LANES), axis=1))
        l_scr[rs, :] = alpha * l_scr[rs, :] + jnp.sum(pr, axis=1, keepdims=True)
        m_scr[rs, :] = m_next
        acc_scr[rs, :] = (jnp.concatenate([alpha] * (C_LATENT // LANES), axis=1) * acc_scr[rs, :]
                          + _dot(pr.astype(BF16), vv))

    def tile_step(kt, slot_next, kk_next):
        vv = ktile(kt)[:, 0:C_LATENT]
        bias = tile_bias(kt)
        for b in range(nblk):
            rs = slice(b * ATT_RB, (b + 1) * ATT_RB)
            s = s_scr[rs, :]
            s_scr[rs, :] = _dot_nt(qp_scr[slot_next, rs, :], kk_next)
            softmax_pv(b, s, bias, vv)

    @pl.when(i == 0)
    def _():
        prep(qa_ref, 0)
        kk0 = ktile(0)
        for b in range(nblk):
            rs = slice(b * ATT_RB, (b + 1) * ATT_RB)
            s_scr[rs, :] = _dot_nt(qp_scr[0, rs, :], kk0)

    m_scr[...] = jnp.full(m_scr.shape, NEG_BIG, F32)
    l_scr[...] = jnp.zeros_like(l_scr)
    acc_scr[...] = jnp.zeros_like(acc_scr)
    last = (i * Q_BLOCK) // IDX_KT

    def step(kt, carry):
        tile_step(kt, cur, ktile(kt + 1))
        return carry

    lax.fori_loop(0, last, step, 0)
    prep(qan_ref, nxt)
    tile_step(last, nxt, ktile(0))
    olat = (acc_scr[...] / jnp.concatenate([l_scr[...]] * (C_LATENT // LANES), axis=1)).astype(BF16)
    for p in range(C_HEADS // 2):
        o = (_dot(olat[(2 * p) * Q_BLOCK:(2 * p + 1) * Q_BLOCK], wuv_ref[2 * p])
             + _dot(olat[(2 * p + 1) * Q_BLOCK:(2 * p + 2) * Q_BLOCK], wuv_ref[2 * p + 1]))
        o_ref[0, :, p * LANES:(p + 1) * LANES] = o.astype(o_ref.dtype)


def _dsa_attention(qa, kp, mask, wuk_x, wuv_x, bn, seq):
    nqb = seq // Q_BLOCK
    nkt = seq // IDX_KT
    rows = C_HEADS * Q_BLOCK
    return pl.pallas_call(
        _dsa_attn_body,
        grid=(bn, nqb),
        in_specs=[pl.BlockSpec((1, Q_BLOCK, P1_CKV), lambda b, i: (b, i, 0)),
                  pl.BlockSpec((1, Q_BLOCK, P1_CKV), lambda b, i: (b, jnp.minimum(i + 1, nqb - 1), 0)),
                  pl.BlockSpec((1, seq, 2 * C_LATENT), lambda b, i: (b, 0, 0)),
                  pl.BlockSpec((1, 1, nkt, Q_BLOCK, IDX_KT), lambda b, i: (b, i, 0, 0, 0)),
                  pl.BlockSpec((C_HEADS, LANES, C_LATENT), lambda b, i: (0, 0, 0)),
                  pl.BlockSpec((C_HEADS, C_LATENT, LANES), lambda b, i: (0, 0, 0))],
        out_specs=pl.BlockSpec((1, Q_BLOCK, C_HEADS * HEAD_DIM), lambda b, i: (b, i, 0)),
        out_shape=jax.ShapeDtypeStruct((bn, seq, C_HEADS * HEAD_DIM), BF16),
        scratch_shapes=[pltpu.VMEM((2, rows, 2 * C_LATENT), BF16),
                        pltpu.VMEM((rows, LANES), F32), pltpu.VMEM((rows, LANES), F32),
                        pltpu.VMEM((rows, C_LATENT), F32),
                        pltpu.VMEM((rows, IDX_KT), F32)],
        compiler_params=_cparams(("arbitrary", "arbitrary")),
        name="dsa_attn",
    )(qa.reshape(bn, seq, -1), qa.reshape(bn, seq, -1), kp.reshape(bn, seq, -1), mask, wuk_x, wuv_x)


def _mla_weights(w_uk, w_uv):
    wuk = jnp.transpose(w_uk, (1, 2, 0))
    wuk_x = jnp.zeros((C_HEADS, LANES, C_LATENT), F32)
    wuv = jnp.transpose(w_uv, (1, 0, 2))
    wuv_x = jnp.zeros((C_HEADS, C_LATENT, LANES), F32)
    for h in range(C_HEADS):
        o = (h % 2) * HEAD_DIM
        wuk_x = wuk_x.at[h, o:o + C_NOPE, :].set(wuk[h])
        wuv_x = wuv_x.at[h, :, o:o + HEAD_DIM].set(wuv[h])
    return wuk_x.astype(BF16), wuv_x.astype(BF16)


def _router_body(h_ref, g_ref, sh_ref, sc_ref, wr_ref, br_ref, route_ref, up_ref, cnt_ref, carry_scr):
    i = pl.program_id(0)
    tm = h_ref.shape[0]

    @pl.when(i == 0)
    def _():
        carry_scr[...] = jnp.zeros_like(carry_scr)

    u = _norm_mod(h_ref[...], g_ref[...], sh_ref[0], sc_ref[0])
    logits = jnp.dot(u, wr_ref[...], preferred_element_type=F32, precision=HIGHEST) + br_ref[...]
    lane = lax.broadcasted_iota(I32, (tm, LANES), 1).astype(F32)
    m1 = jnp.max(logits, axis=1, keepdims=True)
    e1 = jnp.min(jnp.where(logits == m1, lane, float(LANES)), axis=1, keepdims=True)
    rest = jnp.where(lane == e1, NEG_BIG * 2, logits)
    m2 = jnp.max(rest, axis=1, keepdims=True)
    e2 = jnp.min(jnp.where(rest == m2, lane, float(LANES)), axis=1, keepdims=True)
    ex = jnp.exp(m2 - m1)
    g1 = 1.0 / (1.0 + ex)
    g2 = ex / (1.0 + ex)
    onehot = ((lane == e1) | (lane == e2))
    oh_bf = jnp.where(onehot, 1.0, 0.0).astype(BF16)
    r = lax.broadcasted_iota(I32, (tm, tm), 0)
    c = lax.broadcasted_iota(I32, (tm, tm), 1)
    tri = jnp.where(c < r, 1.0, 0.0).astype(BF16)
    prefix = _dot(tri, oh_bf) + carry_scr[...]
    rank1 = jnp.sum(jnp.where(lane == e1, prefix, 0.0), axis=1, keepdims=True)
    rank2 = jnp.sum(jnp.where(lane == e2, prefix, 0.0), axis=1, keepdims=True)
    carry_scr[...] = carry_scr[...] + jnp.sum(jnp.where(onehot, 1.0, 0.0), axis=0, keepdims=True)
    cnt_ref[...] = carry_scr[...]
    vals = [e1, e2, g1, g2, rank1, rank2]
    route = jnp.zeros((tm, LANES), F32)
    for k, v in enumerate(vals):
        route = jnp.where(lane == float(k), v, route)
    route_ref[...] = route
    up_ref[...] = u


def _router(h2, g, shift, scale, w_router, b_router, seq, tm=512):
    t, d = h2.shape
    tpb = seq // tm
    wr = jnp.pad(w_router.astype(F32), ((0, 0), (0, LANES - N_EXPERTS)))
    br = jnp.pad(b_router.astype(F32), (0, LANES - N_EXPERTS), constant_values=NEG_BIG).reshape(1, LANES)
    mod_spec = pl.BlockSpec((1, 1, d), lambda i: (i // tpb, 0, 0))
    return pl.pallas_call(
        _router_body,
        grid=(t // tm,),
        in_specs=[pl.BlockSpec((tm, d), lambda i: (i, 0)),
                  pl.BlockSpec((1, d), lambda i: (0, 0)),
                  mod_spec, mod_spec,
                  pl.BlockSpec((d, LANES), lambda i: (0, 0)),
                  pl.BlockSpec((1, LANES), lambda i: (0, 0))],
        out_specs=[pl.BlockSpec((tm, LANES), lambda i: (i, 0)),
                   pl.BlockSpec((tm, d), lambda i: (i, 0)),
                   pl.BlockSpec((1, LANES), lambda i: (0, 0))],
        out_shape=[jax.ShapeDtypeStruct((t, LANES), F32),
                   jax.ShapeDtypeStruct((t, d), F32),
                   jax.ShapeDtypeStruct((1, LANES), F32)],
        scratch_shapes=[pltpu.VMEM((1, LANES), F32)],
        compiler_params=_cparams(("arbitrary",)),
        name="moe_router",
    )(h2, g, shift, scale, wr, br)


def _dispatch_body(p1_ref, p2_ref, up_ref, xs_in_ref, xs_ref, sem):
    del xs_in_ref
    tm = up_ref.shape[0]

    def copy(r, dst):
        return pltpu.make_async_copy(up_ref.at[pl.ds(r, 1)], xs_ref.at[pl.ds(dst, 1)], sem)

    def start(r, c):
        copy(r, p1_ref[r]).start(priority=0)
        copy(r, p2_ref[r]).start(priority=1)
        return c

    lax.fori_loop(0, tm, start, 0, unroll=DMA_UNROLL)

    def wait(r, c):
        copy(r, p1_ref[r]).wait()
        copy(r, p2_ref[r]).wait()
        return c

    lax.fori_loop(0, tm, wait, 0, unroll=DMA_UNROLL)


def _dispatch(pos1, pos2, up, nrows, tm=512):
    t, w = up.shape
    xs0 = jnp.zeros((nrows, w), F32)
    smem_spec = pl.BlockSpec((tm,), lambda i: (i,), memory_space=pltpu.SMEM)
    return pl.pallas_call(
        _dispatch_body,
        grid=(t // tm,),
        in_specs=[smem_spec, smem_spec,
                  pl.BlockSpec((tm, w), lambda i: (i, 0)),
                  pl.BlockSpec(memory_space=pl.ANY)],
        out_specs=pl.BlockSpec(memory_space=pl.ANY),
        out_shape=jax.ShapeDtypeStruct((nrows, w), F32),
        scratch_shapes=[pltpu.SemaphoreType.DMA(())],
        input_output_aliases={3: 0},
        compiler_params=_cparams(("arbitrary",)),
        name="moe_dispatch",
    )(pos1, pos2, up, xs0)


def _expert_body(te_ref, nv_ref, xs_ref, wg_ref, wu_ref, wd_ref, y_ref, x_scr, acc_scr):
    j = pl.program_id(0)
    f = pl.program_id(1)
    live = j < nv_ref[0]

    @pl.when(f == 0)
    def _():
        x_scr[...] = xs_ref[...].astype(BF16)
        acc_scr[...] = jnp.zeros_like(acc_scr)

    @pl.when(live)
    def _():
        x = x_scr[...]
        gp = _dot(x, wg_ref[0])
        up = _dot(x, wu_ref[0])
        a = (gp * jax.nn.sigmoid(gp) * up).astype(BF16)
        acc_scr[...] += _dot(a, wd_ref[0])

    @pl.when(f == pl.num_programs(1) - 1)
    def _():
        y_ref[...] = acc_scr[...]


def _experts(tile_expert, n_live, xs, wg, wu, wd, tm, tf=MOE_TF):
    nrows, d = xs.shape
    nf = wg.shape[2] // tf
    nt = nrows // tm

    def f_eff(j, f, nv):
        return jnp.where(j < nv[0], f, nf - 1)

    grid_spec = pltpu.PrefetchScalarGridSpec(
        num_scalar_prefetch=2,
        grid=(nt, nf),
        in_specs=[pl.BlockSpec((tm, d), lambda j, f, te, nv: (j, 0)),
                  pl.BlockSpec((1, d, tf), lambda j, f, te, nv: (te[j], 0, f_eff(j, f, nv))),
                  pl.BlockSpec((1, d, tf), lambda j, f, te, nv: (te[j], 0, f_eff(j, f, nv))),
                  pl.BlockSpec((1, tf, d), lambda j, f, te, nv: (te[j], f_eff(j, f, nv), 0))],
        out_specs=pl.BlockSpec((tm, d), lambda j, f, te, nv: (j, 0)),
        scratch_shapes=[pltpu.VMEM((tm, d), BF16), pltpu.VMEM((tm, d), F32)])
    return pl.pallas_call(
        _expert_body,
        grid_spec=grid_spec,
        out_shape=jax.ShapeDtypeStruct((nrows, d), F32),
        compiler_params=_cparams(("arbitrary", "arbitrary")),
        name="moe_experts",
    )(tile_expert, n_live, xs, wg, wu, wd)


def _combine_body(p1_ref, p2_ref, route_ref, h_ref, gt_ref, gf_ref, y_ref, o_ref, y1_scr, y2_scr, sem):
    tm = h_ref.shape[0]

    def copy(src, dst_scr, r):
        return pltpu.make_async_copy(y_ref.at[pl.ds(src, 1)], dst_scr.at[pl.ds(r, 1)], sem)

    def start(r, c):
        copy(p1_ref[r], y1_scr, r).start(priority=0)
        copy(p2_ref[r], y2_scr, r).start(priority=1)
        return c

    lax.fori_loop(0, tm, start, 0, unroll=DMA_UNROLL)

    def wait(r, c):
        copy(p1_ref[r], y1_scr, r).wait()
        copy(p2_ref[r], y2_scr, r).wait()
        return c

    lax.fori_loop(0, tm, wait, 0, unroll=DMA_UNROLL)
    route = route_ref[...]
    g1, g2 = route[:, 2:3], route[:, 3:4]
    y = g1 * y1_scr[...] + g2 * y2_scr[...]
    hn = h_ref[...] + gt_ref[0] * y
    ms = jnp.mean(hn * hn, axis=-1, keepdims=True)
    o_ref[...] = hn * lax.rsqrt(ms + NORM_EPS) * gf_ref[...]


def _combine(pos1, pos2, route, h2, gate, g_final, y_sorted, seq, tm=512):
    t, d = h2.shape
    tpb = seq // tm
    smem_spec = pl.BlockSpec((tm,), lambda i: (i,), memory_space=pltpu.SMEM)
    return pl.pallas_call(
        _combine_body,
        grid=(t // tm,),
        in_specs=[smem_spec, smem_spec,
                  pl.BlockSpec((tm, LANES), lambda i: (i, 0)),
                  pl.BlockSpec((tm, d), lambda i: (i, 0)),
                  pl.BlockSpec((1, 1, d), lambda i: (i // tpb, 0, 0)),
                  pl.BlockSpec((1, d), lambda i: (0, 0)),
                  pl.BlockSpec(memory_space=pl.ANY)],
        out_specs=pl.BlockSpec((tm, d), lambda i: (i, 0)),
        out_shape=jax.ShapeDtypeStruct((t, d), F32),
        scratch_shapes=[pltpu.VMEM((tm, d), F32), pltpu.VMEM((tm, d), F32),
                        pltpu.SemaphoreType.DMA(())],
        compiler_params=_cparams(("arbitrary",)),
        name="moe_combine",
    )(pos1, pos2, route, h2, gate, g_final, y_sorted)


MOE_TILE = 512


def _moe_layout(route, counts):
    e1 = route[:, 0].astype(I32)
    e2 = route[:, 1].astype(I32)
    r1 = route[:, 4].astype(I32)
    r2 = route[:, 5].astype(I32)
    cnt = counts[0, :N_EXPERTS].astype(I32)
    tiles = (cnt + MOE_TILE - 1) // MOE_TILE
    tile_end = jnp.cumsum(tiles)
    start = (tile_end - tiles) * MOE_TILE
    pos1 = start[e1] + r1
    pos2 = start[e2] + r2
    nt = route.shape[0] * 2 // MOE_TILE + N_EXPERTS
    n_live = tile_end[-1]
    tid = jnp.minimum(jnp.arange(nt, dtype=I32), n_live - 1)
    tile_expert = jnp.sum((tid[:, None] >= tile_end[None, :]).astype(I32), axis=1)
    return pos1, pos2, tile_expert.astype(I32), n_live.reshape(1).astype(I32), nt * MOE_TILE


def kernel(x, c, w_ada, b_ada, g_mix, g_ffn, g_final, e_w_in, e_w_out, s5_a_re, s5_a_im, s5_log_dt,
           s5_b_re, s5_b_im, s5_c_re, s5_c_im, s5_d, s5_w_glu, s5_b_glu, ff_w_gate, ff_w_up,
           ff_w_down, o_w_in, o_w_out, mla_g_kv, mla_w_uk, mla_w_uv, moe_w_router, moe_b_router,
           moe_w_gate, moe_w_up, moe_w_down):
    bn, seq, d = x.shape
    t = bn * seq
    mod = _adaln(c, w_ada, b_ada)

    def mod_vec(layer, k):
        return mod[layer, :, k * d:(k + 1) * d].reshape(bn, 1, d)

    tabs64 = _rope_tables(seq, HEAD_DIM)
    tabs16 = _rope_tables(seq, ROPE_DIM)
    h = x.reshape(t, d)

    *qkv_views, s_in = _proj0(h, g_mix[0].reshape(1, d), mod_vec(0, 0), mod_vec(0, 1),
                              e_w_in[0].astype(BF16), tabs64, seq)
    y_a = _dilated_attention(qkv_views, bn, seq)
    prm = _s5_prepare(s5_a_re[0], s5_a_im[0], s5_log_dt[0], s5_b_re[0], s5_b_im[0],
                      s5_c_re[0], s5_c_im[0], s5_d[0], seq // S5_CHUNK)
    y_b = _s5_mixer(s_in, prm, s5_w_glu[0].astype(BF16), s5_b_glu[0], bn, seq)
    h = _outproj([y_a], [y_b], e_w_out[0].astype(BF16), h, mod_vec(0, 2), seq)
    h = _ffn_dense(h, g_ffn[0].reshape(1, d), mod_vec(0, 3), mod_vec(0, 4), mod_vec(0, 5),
                   ff_w_gate[0].astype(BF16), ff_w_up[0].astype(BF16), ff_w_down[0].astype(BF16), seq)

    w1 = _proj1_layout(o_w_in[0]).astype(BF16)
    qa, kp, qi, ki, wi = _proj1(h, g_mix[1].reshape(1, d), mod_vec(1, 0), mod_vec(1, 1), w1,
                                mla_g_kv[0].reshape(1, C_LATENT).astype(F32), tabs16, tabs64, seq)
    mask = _dsa_select(qi, ki, wi, bn, seq)
    wuk_x, wuv_x = _mla_weights(mla_w_uk[0], mla_w_uv[0])
    o_attn = _dsa_attention(qa, kp, mask, wuk_x, wuv_x, bn, seq)
    h = _outproj([o_attn.reshape(t, C_HEADS * HEAD_DIM)], [], o_w_out[0].astype(BF16), h, mod_vec(1, 2), seq)

    route, up, counts = _router(h, g_ffn[1].reshape(1, d), mod_vec(1, 3), mod_vec(1, 4),
                                moe_w_router[0], moe_b_router[0], seq)
    pos1, pos2, tile_expert, n_live, nrows = _moe_layout(route, counts)
    xs = _dispatch(pos1, pos2, up, nrows)
    y_sorted = _experts(tile_expert, n_live, xs, moe_w_gate[0].astype(BF16), moe_w_up[0].astype(BF16),
                        moe_w_down[0].astype(BF16), MOE_TILE)
    out = _combine(pos1, pos2, route, h, mod_vec(1, 5), g_final.reshape(1, d), y_sorted, seq)
    return out.reshape(bn, seq, d)
```
